```python
import math
import jax, jax.numpy as jnp
from jax import lax
import numpy as np

D_MODEL = 1024
BATCH = 8
SEQ = 2048
DEPTH = 4
DEC_BATCH = 128
DEC_SEQ = 1
PAST_LEN = 16384
PAGE_SIZE = 128

MIX_W = D_MODEL
H_A = D_MODEL // 256
DK = 128
DV = 128
W_A = H_A * DV
QKV_DIM = 2 * H_A * DK + H_A * DV
CONV_W = 4
DN_CHUNK = 64
W_B = MIX_W - W_A
H_B = 4
DH_B = W_B // H_B
GM_CHUNK = 128
D_FF = 4 * D_MODEL
OFF_G = QKV_DIM
OFF_A = OFF_G + W_A
OFF_B = OFF_A + H_A
OFF_U = OFF_B + H_A
OFF_V = OFF_U + W_B
PROJ_DIM = OFF_V + W_B
EPS = 1e-6

kernel_name = "hymba_gdn_gmlp_decode_step"


def rmsnorm(x, g):
    xf = x.astype(jnp.float32)
    y = xf * lax.rsqrt(jnp.mean(xf * xf, axis=-1, keepdims=True) + EPS)
    return (y * g.astype(jnp.float32)).astype(x.dtype)


def l2norm(x):
    return x * lax.rsqrt(jnp.sum(x * x, axis=-1, keepdims=True) + EPS)


def causal_conv(x, buf, w):
    L = x.shape[1]
    xp = jnp.concatenate([buf.astype(x.dtype), x], axis=1)
    y = xp[:, 0:L] * w[0]
    for j in range(1, CONV_W):
        y = y + xp[:, j:j + L] * w[j]
    return y, xp[:, L:]


def gated_delta_chunked(q, k, v, gdec, beta, S0):
    B, L, H, _ = q.shape
    C = DN_CHUNK
    N = L // C
    q = q.reshape(B, N, C, H, DK)
    k = k.reshape(B, N, C, H, DK)
    v = v.reshape(B, N, C, H, DV)
    beta = beta.reshape(B, N, C, H)
    gc = jnp.cumsum(gdec.reshape(B, N, C, H), axis=2)
    gct = jnp.swapaxes(gc, 2, 3)
    idx = jnp.arange(C)
    lower = idx[:, None] >= idx[None, :]
    strict = idx[:, None] > idx[None, :]
    diff = gct[..., :, None] - gct[..., None, :]
    decay = jnp.where(lower, jnp.exp(jnp.where(lower, diff, 0.0)), 0.0)
    k_beta = k * beta[..., None]
    v_beta = v * beta[..., None]
    kk = jnp.einsum('bnihd,bnjhd->bnhij', k_beta, k) * decay
    eye = jnp.eye(C, dtype=jnp.float32)
    A = eye + jnp.where(strict, kk, 0.0)
    T = lax.linalg.triangular_solve(A, jnp.broadcast_to(eye, A.shape), left_side=True,
                                    lower=True, unit_diagonal=True)
    u = jnp.einsum('bnhij,bnjhd->bnhid', T, v_beta)
    w = jnp.einsum('bnhij,bnjhd->bnhid', T, k_beta * jnp.exp(gc)[..., None])
    qk = jnp.einsum('bnihd,bnjhd->bnhij', q, k) * decay

    def step(S, inp):
        q_n, k_n, u_n, w_n, qk_n, g_n = inp
        v_new = u_n - jnp.einsum('bhid,bhde->bhie', w_n, S)
        o = (jnp.einsum('bihd,bhde->bhie', q_n, S) * jnp.exp(g_n)[..., None]
             + jnp.einsum('bhij,bhje->bhie', qk_n, v_new))
        g_last = g_n[..., -1]
        k_dec = k_n * jnp.swapaxes(jnp.exp(g_last[..., None] - g_n), 1, 2)[..., None]
        S = S * jnp.exp(g_last)[..., None, None] + jnp.einsum('bihd,bhie->bhde', k_dec, v_new)
        return S, o

    xs = (jnp.moveaxis(q, 1, 0), jnp.moveaxis(k, 1, 0), jnp.moveaxis(u, 1, 0),
          jnp.moveaxis(w, 1, 0), jnp.moveaxis(qk, 1, 0), jnp.moveaxis(gct, 1, 0))
    S, o = lax.scan(step, S0, xs)
    o = jnp.transpose(o, (1, 0, 3, 2, 4)).reshape(B, L, H, DV)
    return o, S


def gated_delta_recurrent(q, k, v, gdec, beta, S0):
    def step(S, inp):
        q_t, k_t, v_t, g_t, b_t = inp
        S = S * jnp.exp(g_t)[..., None, None]
        kv = jnp.einsum('bhd,bhde->bhe', k_t, S)
        delta = (v_t - kv) * b_t[..., None]
        S = S + jnp.einsum('bhd,bhe->bhde', k_t, delta)
        o = jnp.einsum('bhd,bhde->bhe', q_t, S)
        return S, o

    xs = (jnp.moveaxis(q, 1, 0), jnp.moveaxis(k, 1, 0), jnp.moveaxis(v, 1, 0),
          jnp.moveaxis(gdec, 1, 0), jnp.moveaxis(beta, 1, 0))
    S, o = lax.scan(step, S0, xs)
    return jnp.moveaxis(o, 0, 1), S


def spatial_gate(u, vb, w_s, b_s):
    B, L, _ = u.shape
    c = min(L, GM_CHUNK)
    idx = jnp.arange(c)
    wm = jnp.where(idx[:, None] >= idx[None, :], w_s[:, :c, :c], 0.0).astype(vb.dtype)
    vc = vb.reshape(B, L // c, c, H_B, DH_B)
    s = jnp.einsum('hij,bnjhd->bnihd', wm, vc) + jnp.swapaxes(b_s[:, :c], 0, 1)[:, :, None].astype(vb.dtype)
    return u * s.reshape(B, L, W_B)


def mixer(h, conv_buf, S0, w_in, conv_w, A_log, dt_bias, o_norm_g, v_norm_g, w_s, b_s, w_o, chunked):
    B, L, _ = h.shape
    p = h @ w_in
    qkv, new_buf = causal_conv(p[..., :QKV_DIM], conv_buf, conv_w)
    qkv = jax.nn.silu(qkv).astype(jnp.float32)
    q = l2norm(qkv[..., :H_A * DK].reshape(B, L, H_A, DK)) * (DK ** -0.5)
    k = l2norm(qkv[..., H_A * DK:2 * H_A * DK].reshape(B, L, H_A, DK))
    v = qkv[..., 2 * H_A * DK:].reshape(B, L, H_A, DV)
    a = p[..., OFF_A:OFF_B].astype(jnp.float32)
    bl = p[..., OFF_B:OFF_U].astype(jnp.float32)
    gdec = -jnp.exp(A_log.astype(jnp.float32)) * jax.nn.softplus(a + dt_bias.astype(jnp.float32))
    beta = jax.nn.sigmoid(bl)
    S0f = S0.astype(jnp.float32)
    if chunked:
        o, S = gated_delta_chunked(q, k, v, gdec, beta, S0f)
    else:
        o, S = gated_delta_recurrent(q, k, v, gdec, beta, S0f)
    gate = p[..., OFF_G:OFF_A].astype(jnp.float32).reshape(B, L, H_A, DV)
    o_a = (rmsnorm(o, o_norm_g) * jax.nn.silu(gate)).reshape(B, L, W_A).astype(h.dtype)
    u = p[..., OFF_U:OFF_V]
    vb = rmsnorm(p[..., OFF_V:], v_norm_g)
    o_b = spatial_gate(u, vb, w_s, b_s)
    y = jnp.concatenate([o_a, o_b], axis=-1) @ w_o
    return y, new_buf, S.astype(S0.dtype), vb


def ffn(h, w_up, w_down):
    return jnp.square(jax.nn.relu(h @ w_up)) @ w_down


def setup_inputs(seed: int = 0) -> dict:
    key = jax.random.key(seed)
    ks = jax.random.split(key, 20)
    f32 = jnp.float32

    def nrm(k, shape, scale):
        return jax.random.normal(k, shape, f32) * scale

    dt = jnp.exp(jax.random.uniform(ks[8], (DEPTH, H_A), f32, math.log(1e-3), math.log(1e-1)))
    return {
        'x_prompt': nrm(ks[0], (BATCH, SEQ, D_MODEL), 1.0),
        'x_sample': nrm(ks[1], (DEC_BATCH, DEC_SEQ, D_MODEL), 1.0),
        'state_delta': nrm(ks[2], (DEPTH, DEC_BATCH, H_A, DK, DV), 0.05),
        'state_conv': nrm(ks[3], (DEPTH, DEC_BATCH, CONV_W - 1, QKV_DIM), 1.0),
        'norm_mix_g': 1.0 + nrm(ks[4], (DEPTH, D_MODEL), 0.05),
        'w_in': nrm(ks[5], (DEPTH, D_MODEL, PROJ_DIM), D_MODEL ** -0.5),
        'conv_w': nrm(ks[6], (DEPTH, CONV_W, QKV_DIM), CONV_W ** -0.5),
        'A_log': jnp.log(jax.random.uniform(ks[7], (DEPTH, H_A), f32, 1.0, 16.0)),
        'dt_bias': dt + jnp.log(-jnp.expm1(-dt)),
        'o_norm_g': 1.0 + nrm(ks[9], (DEPTH, DV), 0.05),
        'v_norm_g': 1.0 + nrm(ks[10], (DEPTH, W_B), 0.05),
        'w_s': nrm(ks[11], (DEPTH, H_B, GM_CHUNK, GM_CHUNK), GM_CHUNK ** -0.5),
        'b_s': 1.0 + nrm(ks[12], (DEPTH, H_B, GM_CHUNK), 0.1),
        'w_o': nrm(ks[13], (DEPTH, MIX_W, D_MODEL), MIX_W ** -0.5),
        'norm_ffn_g': 1.0 + nrm(ks[14], (DEPTH, D_MODEL), 0.05),
        'w_up': nrm(ks[15], (DEPTH, D_MODEL, D_FF), D_MODEL ** -0.5),
        'w_down': nrm(ks[16], (DEPTH, D_FF, D_MODEL), D_FF ** -0.5),
        'norm_f_g': 1.0 + nrm(ks[17], (D_MODEL,), 0.05),
    }


def reference(x_prompt, x_sample, state_delta, state_conv, norm_mix_g, w_in, conv_w, A_log,
              dt_bias, o_norm_g, v_norm_g, w_s, b_s, w_o, norm_ffn_g, w_up, w_down, norm_f_g):
    xp, xs = x_prompt, x_sample
    conv0 = jnp.zeros((xp.shape[0], CONV_W - 1, QKV_DIM), xp.dtype)
    S0p = jnp.zeros((xp.shape[0], H_A, DK, DV), state_delta.dtype)
    dp_list, cp_list, ds_list, cs_list, vs_list = [], [], [], [], []
    for l in range(DEPTH):
        lp = (w_in[l], conv_w[l], A_log[l], dt_bias[l], o_norm_g[l], v_norm_g[l], w_s[l], b_s[l], w_o[l])
        yp, cp, Sp, _ = mixer(rmsnorm(xp, norm_mix_g[l]), conv0, S0p, *lp, chunked=True)
        ys, cs, Ss, vrows = mixer(rmsnorm(xs, norm_mix_g[l]), state_conv[l], state_delta[l], *lp, chunked=False)
        xp = xp + yp
        xs = xs + ys
        xp = xp + ffn(rmsnorm(xp, norm_ffn_g[l]), w_up[l], w_down[l])
        xs = xs + ffn(rmsnorm(xs, norm_ffn_g[l]), w_up[l], w_down[l])
        dp_list.append(Sp)
        cp_list.append(cp)
        ds_list.append(Ss)
        cs_list.append(cs)
        vs_list.append(vrows)
    y_prompt = rmsnorm(xp, norm_f_g)
    y_sample = rmsnorm(xs, norm_f_g)
    new_delta_prompt = jnp.stack(dp_list)
    new_conv_prompt = jnp.stack(cp_list)
    new_delta_sample = jnp.stack(ds_list)
    new_conv_sample = jnp.stack(cs_list)
    new_gmlp_v_sample = jnp.stack(vs_list)
    return (y_prompt, y_sample, new_delta_prompt, new_conv_prompt, new_delta_sample, new_conv_sample, new_gmlp_v_sample)
```

```python
import functools

import jax
import jax.numpy as jnp
from jax import lax
from jax.experimental import pallas as pl
from jax.experimental.pallas import tpu as pltpu

F32 = jnp.float32
BF16 = jnp.bfloat16

H_A = 4
DK = 128
DV = 128
W_A = H_A * DV
QKV = 2 * H_A * DK + H_A * DV
CONV_W = 4
DN_CHUNK = 64
H_B = 4
DH_B = 128
W_B = H_B * DH_B
GM_CHUNK = 128
EPS = 1e-6

LANES = 128
SUBLANES = 8
AB_W = LANES
OFF_GATE = QKV
OFF_U = OFF_GATE + W_A
OFF_V = OFF_U + W_B
OFF_AB = OFF_V + W_B
PROJ_PAD = OFF_AB + AB_W

VMEM_LIMIT = 56 * 1024 * 1024
FF_BLOCK = 1024


def _rmsnorm(x, g):
    return x * lax.rsqrt(jnp.mean(x * x, axis=-1, keepdims=True) + EPS) * g


def _silu(x):
    return x * jax.nn.sigmoid(x)


def _softplus(x):
    return jnp.maximum(x, 0.0) + jnp.log1p(jnp.exp(-jnp.abs(x)))


def _const_spec(shape):
    nd = len(shape)
    return pl.BlockSpec(shape, lambda *_: (0,) * nd, pipeline_mode=pl.Buffered(1))


def _project(x, g, win_ref):
    h = _rmsnorm(x, g).astype(BF16)
    return jnp.dot(h, win_ref[...], preferred_element_type=F32)


def _qkv_activation(y):
    a = _silu(y)
    blocks = []
    for j in range(QKV // LANES):
        blk = a[:, j * LANES:(j + 1) * LANES]
        if j < 2 * H_A:
            blk = blk * lax.rsqrt(jnp.sum(blk * blk, axis=-1, keepdims=True) + EPS)
            if j < H_A:
                blk = blk * (DK ** -0.5)
        blocks.append(blk)
    return blocks


def _decay_beta(ab, alog, dtb):
    lane = lax.broadcasted_iota(jnp.int32, ab.shape, 1)
    gdec = -jnp.exp(alog) * _softplus(ab + dtb)
    beta = jax.nn.sigmoid(ab)
    return jnp.where(lane < H_A, gdec, beta)


def _prompt_in_kernel(x_ref, g_ref, win_ref, cw_ref, alog_ref, dtb_ref, vg_ref, ws_ref, bsb_ref,
                      qkv_ref, gate_ref, gb_ref, ob_ref, ctail_ref, xp_ref, *, tm, nt):
    i = pl.program_id(1)
    p = _project(x_ref[...], g_ref[...], win_ref)
    pq = p[:, :QKV]

    @pl.when(i == 0)
    def _():
        xp_ref[0:SUBLANES, :] = jnp.zeros((SUBLANES, QKV), F32)

    xp_ref[SUBLANES:SUBLANES + tm, :] = pq
    cw = cw_ref[...]
    y = pq * cw[CONV_W - 1:CONV_W, :]
    for j in range(CONV_W - 1):
        off = SUBLANES - (CONV_W - 1) + j
        y = y + xp_ref[off:off + tm, :] * cw[j:j + 1, :]
    xp_ref[0:SUBLANES, :] = xp_ref[tm:tm + SUBLANES, :]

    @pl.when(i == nt - 1)
    def _():
        ctail_ref[0] = pq[tm - (CONV_W - 1):tm, :]

    for j, blk in enumerate(_qkv_activation(y)):
        qkv_ref[:, j * LANES:(j + 1) * LANES] = blk
    gate_ref[...] = p[:, OFF_GATE:OFF_U]
    gb_ref[...] = _decay_beta(p[:, OFF_AB:PROJ_PAD], alog_ref[...], dtb_ref[...])

    u = p[:, OFF_U:OFF_V]
    vb = _rmsnorm(p[:, OFF_V:OFF_AB], vg_ref[...]).astype(BF16)
    nc = tm // GM_CHUNK
    ri = lax.broadcasted_iota(jnp.int32, (GM_CHUNK, GM_CHUNK), 0)
    ci = lax.broadcasted_iota(jnp.int32, (GM_CHUNK, GM_CHUNK), 1)
    for hh in range(H_B):
        wm = jnp.where(ri >= ci, ws_ref[hh], 0.0).astype(BF16)
        cols = slice(hh * DH_B, (hh + 1) * DH_B)
        rhs = jnp.concatenate([vb[c * GM_CHUNK:(c + 1) * GM_CHUNK, cols] for c in range(nc)], axis=1)
        s = jnp.dot(wm, rhs, preferred_element_type=F32)
        bias = bsb_ref[hh]
        for c in range(nc):
            rows = slice(c * GM_CHUNK, (c + 1) * GM_CHUNK)
            ob_ref[rows, cols] = u[rows, cols] * (s[:, c * DH_B:(c + 1) * DH_B] + bias)


def _prompt_in(x2d, g, win, cw, alog, dtb, vg, ws, bsb, *, batch, seq, tm):
    nt = seq // tm
    rows = batch * seq
    d = x2d.shape[1]
    row_spec = lambda w: pl.BlockSpec((tm, w), lambda b, i: (b * nt + i, 0))
    return pl.pallas_call(
        functools.partial(_prompt_in_kernel, tm=tm, nt=nt),
        grid=(batch, nt),
        in_specs=[row_spec(d), _const_spec((1, d)), _const_spec(win.shape), _const_spec(cw.shape),
                  _const_spec((1, AB_W)), _const_spec((1, AB_W)), _const_spec((1, W_B)),
                  _const_spec(ws.shape), _const_spec(bsb.shape)],
        out_specs=[row_spec(QKV), row_spec(W_A), row_spec(AB_W), row_spec(W_B),
                   pl.BlockSpec((1, CONV_W - 1, QKV), lambda b, i: (b, 0, 0))],
        out_shape=[jax.ShapeDtypeStruct((rows, QKV), F32), jax.ShapeDtypeStruct((rows, W_A), F32),
                   jax.ShapeDtypeStruct((rows, AB_W), F32), jax.ShapeDtypeStruct((rows, W_B), F32),
                   jax.ShapeDtypeStruct((batch, CONV_W - 1, QKV), F32)],
        scratch_shapes=[pltpu.VMEM((tm + SUBLANES, QKV), F32)],
        compiler_params=pltpu.CompilerParams(dimension_semantics=("arbitrary", "arbitrary"),
                                             vmem_limit_bytes=VMEM_LIMIT),
        name="prompt_in",
    )(x2d, g, win, cw, alog, dtb, vg, ws, bsb)


def _unit_lower_inverse(lmat, ri, ci):
    c = lmat.shape[0]
    eye = (ri == ci).astype(F32)
    x = None
    s = 1
    while s < c:
        mask = ((ri // (2 * s)) == (ci // (2 * s))) & (((ri // s) % 2) == 1) & (((ci // s) % 2) == 0)
        cs = jnp.where(mask, lmat, 0.0)
        if x is None:
            x = eye - cs
        else:
            xb = x.astype(BF16)
            y = jnp.dot(xb, cs.astype(BF16), preferred_element_type=F32)
            x = x - jnp.dot(y.astype(BF16), xb, preferred_element_type=F32)
        s *= 2
    return x


def _nt_dot(a, b):
    return lax.dot_general(a.astype(BF16), b.astype(BF16), (((1,), (1,)), ((), ())),
                           preferred_element_type=F32)


def _tn_dot(a, b):
    return lax.dot_general(a.astype(BF16), b.astype(BF16), (((0,), (0,)), ((), ())),
                           preferred_element_type=F32)


def _dot(a, b):
    return jnp.dot(a.astype(BF16), b.astype(BF16), preferred_element_type=F32)


def _delta_kernel(qkv_ref, gb_ref, gate_ref, og_ref, oa_ref, sout_ref, s_ref, *, tm, nt):
    i = pl.program_id(1)
    c = DN_CHUNK

    @pl.when(i == 0)
    def _():
        s_ref[...] = jnp.zeros(s_ref.shape, F32)

    ri = lax.broadcasted_iota(jnp.int32, (c, c), 0)
    ci = lax.broadcasted_iota(jnp.int32, (c, c), 1)
    lower = ri >= ci
    strict = ri > ci
    tri = lower.astype(F32)
    og = og_ref[...]

    def chunk_body(n, carry):
        r0 = pl.multiple_of(n * c, c)
        rows = pl.ds(r0, c)
        gbc = gb_ref[rows, :]
        gcs = jnp.dot(tri, gbc, preferred_element_type=F32, precision=lax.Precision.HIGHEST)
        gcs_t = lax.dot_general(gbc, tri, (((0,), (1,)), ((), ())), preferred_element_type=F32,
                                precision=lax.Precision.HIGHEST)
        for hh in range(H_A):
            q = qkv_ref[rows, hh * DK:(hh + 1) * DK]
            k = qkv_ref[rows, H_A * DK + hh * DK:H_A * DK + (hh + 1) * DK]
            v = qkv_ref[rows, 2 * H_A * DK + hh * DV:2 * H_A * DK + (hh + 1) * DV]
            g_col = jnp.broadcast_to(gcs[:, hh:hh + 1], (c, LANES))
            beta = jnp.broadcast_to(gbc[:, H_A + hh:H_A + hh + 1], (c, LANES))
            g_row = jnp.broadcast_to(gcs_t[hh:hh + 1, :], (c, c))
            decay = jnp.where(lower, jnp.exp(jnp.where(lower, g_col[:, :c] - g_row, 0.0)), 0.0)
            kb = k * beta
            vbeta = v * beta
            eg = jnp.exp(g_col)
            lmat = jnp.where(strict, _nt_dot(kb, k) * decay, 0.0)
            t = _unit_lower_inverse(lmat, ri, ci)
            u = _dot(t, vbeta)
            w = _dot(t, kb * eg)
            qk = _nt_dot(q, k) * decay
            s_old = s_ref[hh]
            v_new = u - _dot(w, s_old)
            o = _dot(q, s_old) * eg + _dot(qk, v_new)
            g_last = g_col[c - 1:c, :]
            k_dec = k * jnp.exp(g_last - g_col)
            s_ref[hh] = s_old * jnp.exp(g_last) + _tn_dot(k_dec, v_new)
            gate = gate_ref[rows, hh * DV:(hh + 1) * DV]
            oa_ref[rows, hh * DV:(hh + 1) * DV] = _rmsnorm(o, og) * _silu(gate)
        return carry

    lax.fori_loop(0, tm // c, chunk_body, 0)

    @pl.when(i == nt - 1)
    def _():
        sout_ref[0] = s_ref[...]


def _delta_prompt(qkv, gb, gate, og, *, batch, seq, tm):
    nt = seq // tm
    rows = batch * seq
    row_spec = lambda w: pl.BlockSpec((tm, w), lambda b, i: (b * nt + i, 0))
    return pl.pallas_call(
        functools.partial(_delta_kernel, tm=tm, nt=nt),
        grid=(batch, nt),
        in_specs=[row_spec(QKV), row_spec(AB_W), row_spec(W_A), _const_spec((1, DV))],
        out_specs=[row_spec(W_A), pl.BlockSpec((1, H_A, DK, DV), lambda b, i: (b, 0, 0, 0))],
        out_shape=[jax.ShapeDtypeStruct((rows, W_A), F32),
                   jax.ShapeDtypeStruct((batch, H_A, DK, DV), F32)],
        scratch_shapes=[pltpu.VMEM((H_A, DK, DV), F32)],
        compiler_params=pltpu.CompilerParams(dimension_semantics=("arbitrary", "arbitrary"),
                                             vmem_limit_bytes=VMEM_LIMIT),
        name="delta_prompt",
    )(qkv, gb, gate, og)


def _out_ffn_kernel(x_ref, oa_ref, ob_ref, wo_ref, gf_ref, wup_ref, wdn_ref, gl_ref, out_ref, *, final_norm):
    y = jnp.dot(oa_ref[...].astype(BF16), wo_ref[0:W_A, :], preferred_element_type=F32)
    y = y + jnp.dot(ob_ref[...].astype(BF16), wo_ref[W_A:W_A + W_B, :], preferred_element_type=F32)
    x1 = x_ref[...] + y
    h = _rmsnorm(x1, gf_ref[...]).astype(BF16)
    d_ff = wup_ref.shape[1]
    ffn = None
    for j in range(d_ff // FF_BLOCK):
        cols = slice(j * FF_BLOCK, (j + 1) * FF_BLOCK)
        a = jnp.dot(h, wup_ref[:, cols], preferred_element_type=F32)
        a = jnp.square(jnp.maximum(a, 0.0)).astype(BF16)
        part = jnp.dot(a, wdn_ref[cols, :], preferred_element_type=F32)
        ffn = part if ffn is None else ffn + part
    x2 = x1 + ffn
    if final_norm:
        x2 = _rmsnorm(x2, gl_ref[...])
    out_ref[...] = x2


def _out_ffn(x2d, oa, ob, wo, gf, wup, wdn, gl, *, tm, final_norm):
    rows, d = x2d.shape
    row_spec = lambda w: pl.BlockSpec((tm, w), lambda i: (i, 0))
    return pl.pallas_call(
        functools.partial(_out_ffn_kernel, final_norm=final_norm),
        grid=(rows // tm,),
        in_specs=[row_spec(d), row_spec(W_A), row_spec(W_B), _const_spec(wo.shape), _const_spec((1, d)),
                  _const_spec(wup.shape), _const_spec(wdn.shape), _const_spec((1, d))],
        out_specs=row_spec(d),
        out_shape=jax.ShapeDtypeStruct((rows, d), F32),
        compiler_params=pltpu.CompilerParams(dimension_semantics=("arbitrary",),
                                             vmem_limit_bytes=VMEM_LIMIT),
        name="out_ffn",
    )(x2d, oa, ob, wo, gf, wup, wdn, gl)


def _sample_in_kernel(x_ref, g_ref, win_ref, cw_ref, c0_ref, c1_ref, c2_ref, alog_ref, dtb_ref, vg_ref,
                      ws0_ref, bs0_ref, qkv_ref, gate_ref, gb_ref, ob_ref, vb_ref, cnew_ref):
    p = _project(x_ref[...], g_ref[...], win_ref)
    pq = p[:, :QKV]
    cw = cw_ref[...]
    c1 = c1_ref[...]
    c2 = c2_ref[...]
    y = c0_ref[...] * cw[0:1, :] + c1 * cw[1:2, :] + c2 * cw[2:3, :] + pq * cw[3:4, :]
    cnew_ref[:, 0:QKV] = c1
    cnew_ref[:, QKV:2 * QKV] = c2
    cnew_ref[:, 2 * QKV:3 * QKV] = pq
    for j, blk in enumerate(_qkv_activation(y)):
        qkv_ref[:, j * LANES:(j + 1) * LANES] = blk
    gate_ref[...] = p[:, OFF_GATE:OFF_U]
    gb_ref[...] = _decay_beta(p[:, OFF_AB:PROJ_PAD], alog_ref[...], dtb_ref[...])
    vb = _rmsnorm(p[:, OFF_V:OFF_AB], vg_ref[...])
    vb_ref[...] = vb
    ob_ref[...] = p[:, OFF_U:OFF_V] * (vb * ws0_ref[...] + bs0_ref[...])


def _sample_in(x2d, g, win, cw, conv_flat, alog, dtb, vg, ws0, bs0):
    n, d = x2d.shape
    full = lambda w: pl.BlockSpec((n, w), lambda i: (0, 0))
    conv_col = lambda j: pl.BlockSpec((n, QKV), lambda i: (0, j))
    return pl.pallas_call(
        _sample_in_kernel,
        grid=(1,),
        in_specs=[full(d), pl.BlockSpec((1, d), lambda i: (0, 0)),
                  pl.BlockSpec(win.shape, lambda i: (0, 0)), pl.BlockSpec(cw.shape, lambda i: (0, 0)),
                  conv_col(0), conv_col(1), conv_col(2),
                  pl.BlockSpec((1, AB_W), lambda i: (0, 0)), pl.BlockSpec((1, AB_W), lambda i: (0, 0)),
                  pl.BlockSpec((1, W_B), lambda i: (0, 0)), pl.BlockSpec((1, W_B), lambda i: (0, 0)),
                  pl.BlockSpec((1, W_B), lambda i: (0, 0))],
        out_specs=[full(QKV), full(W_A), full(AB_W), full(W_B), full(W_B), full((CONV_W - 1) * QKV)],
        out_shape=[jax.ShapeDtypeStruct((n, QKV), F32), jax.ShapeDtypeStruct((n, W_A), F32),
                   jax.ShapeDtypeStruct((n, AB_W), F32), jax.ShapeDtypeStruct((n, W_B), F32),
                   jax.ShapeDtypeStruct((n, W_B), F32),
                   jax.ShapeDtypeStruct((n, (CONV_W - 1) * QKV), F32)],
        compiler_params=pltpu.CompilerParams(dimension_semantics=("arbitrary",),
                                             vmem_limit_bytes=VMEM_LIMIT),
        name="sample_in",
    )(x2d, g, win, cw, conv_flat, conv_flat, conv_flat, alog, dtb, vg, ws0, bs0)


def _delta_step_kernel(qkv_ref, gb_ref, gate_ref, og_ref, s_ref, oa_ref, snew_ref, *, tb):
    og = og_ref[...]
    gb = gb_ref[...]
    for hh in range(H_A):
        q = qkv_ref[:, hh * DK:(hh + 1) * DK]
        k = qkv_ref[:, H_A * DK + hh * DK:H_A * DK + (hh + 1) * DK]
        v = qkv_ref[:, 2 * H_A * DK + hh * DV:2 * H_A * DK + (hh + 1) * DV]
        q_t = q.T
        k_t = k.T
        decay = jnp.exp(gb[:, hh:hh + 1])
        beta = gb[:, H_A + hh:H_A + hh + 1]
        o_rows = []
        for t in range(tb):
            s = s_ref[t, hh] * decay[t:t + 1, :]
            k_col = jnp.broadcast_to(k_t[:, t:t + 1], (DK, DV))
            q_col = jnp.broadcast_to(q_t[:, t:t + 1], (DK, DV))
            kv = jnp.sum(s * k_col, axis=0, keepdims=True)
            delta = (v[t:t + 1, :] - kv) * beta[t:t + 1, :]
            s = s + k_col * delta
            snew_ref[t, hh] = s
            o_rows.append(jnp.sum(s * q_col, axis=0, keepdims=True))
        o = jnp.concatenate(o_rows, axis=0)
        gate = gate_ref[:, hh * DV:(hh + 1) * DV]
        oa_ref[:, hh * DV:(hh + 1) * DV] = _rmsnorm(o, og) * _silu(gate)


def _delta_step(qkv, gb, gate, og, state, *, tb):
    n = qkv.shape[0]
    row_spec = lambda w: pl.BlockSpec((tb, w), lambda i: (i, 0))
    st_spec = pl.BlockSpec((tb, H_A, DK, DV), lambda i: (i, 0, 0, 0))
    return pl.pallas_call(
        functools.partial(_delta_step_kernel, tb=tb),
        grid=(n // tb,),
        in_specs=[row_spec(QKV), row_spec(AB_W), row_spec(W_A), pl.BlockSpec((1, DV), lambda i: (0, 0)),
                  st_spec],
        out_specs=[row_spec(W_A), st_spec],
        out_shape=[jax.ShapeDtypeStruct((n, W_A), F32), jax.ShapeDtypeStruct(state.shape, F32)],
        compiler_params=pltpu.CompilerParams(dimension_semantics=("arbitrary",),
                                             vmem_limit_bytes=VMEM_LIMIT),
        name="delta_step",
    )(qkv, gb, gate, og, state)


def _pad_lanes(v, width):
    v = v.reshape(1, -1).astype(F32)
    return jnp.pad(v, ((0, 0), (0, width - v.shape[1])))


def _pick_tile(seq):
    for tm in (512, 256, 128):
        if seq % tm == 0:
            return tm
    raise ValueError("prompt length must be a multiple of 128")


def kernel(x_prompt, x_sample, state_delta, state_conv, norm_mix_g, w_in, conv_w, A_log, dt_bias, o_norm_g,
           v_norm_g, w_s, b_s, w_o, norm_ffn_g, w_up, w_down, norm_f_g):
    batch, seq, d = x_prompt.shape
    n_dec, dec_seq, _ = x_sample.shape
    depth = w_in.shape[0]
    assert dec_seq == 1 and seq % GM_CHUNK == 0 and n_dec % SUBLANES == 0
    tm = _pick_tile(seq)
    tb = SUBLANES

    off_a = QKV + W_A
    off_u = off_a + 2 * H_A
    off_v = off_u + W_B
    win = jnp.concatenate(
        [w_in[:, :, :off_a], w_in[:, :, off_u:off_v], w_in[:, :, off_v:], w_in[:, :, off_a:off_u],
         jnp.zeros((depth, d, AB_W - 2 * H_A), w_in.dtype)], axis=-1).astype(BF16)
    wo = w_o.astype(BF16)
    wup = w_up.astype(BF16)
    wdn = w_down.astype(BF16)
    bsb = jnp.broadcast_to(b_s[:, :, :, None], b_s.shape + (DH_B,)).astype(F32)
    ws0 = jnp.repeat(w_s[:, :, 0, 0], DH_B, axis=-1)
    bs0 = jnp.repeat(b_s[:, :, 0], DH_B, axis=-1)
    gl = norm_f_g.reshape(1, d)

    xp = x_prompt.reshape(batch * seq, d)
    xs = x_sample.reshape(n_dec, d)
    dp, cp, ds, cs, vs = [], [], [], [], []
    for l in range(depth):
        last = l == depth - 1
        gmix = norm_mix_g[l].reshape(1, d)
        gffn = norm_ffn_g[l].reshape(1, d)
        alog = _pad_lanes(A_log[l], AB_W)
        dtb = _pad_lanes(jnp.concatenate([dt_bias[l], jnp.zeros((H_A,), F32)]), AB_W)
        vg = v_norm_g[l].reshape(1, W_B)
        og = o_norm_g[l].reshape(1, DV)

        qkv, gate, gb, ob, ctail = _prompt_in(xp, gmix, win[l], conv_w[l], alog, dtb, vg, w_s[l], bsb[l],
                                              batch=batch, seq=seq, tm=tm)
        oa, s_fin = _delta_prompt(qkv, gb, gate, og, batch=batch, seq=seq, tm=tm)
        xp = _out_ffn(xp, oa, ob, wo[l], gffn, wup[l], wdn[l], gl, tm=tm, final_norm=last)
        dp.append(s_fin)
        cp.append(ctail)

        conv_flat = state_conv[l].reshape(n_dec, (CONV_W - 1) * QKV)
        qkv_s, gate_s, gb_s, ob_s, vb_s, cnew = _sample_in(
            xs, gmix, win[l], conv_w[l], conv_flat, alog, dtb, vg, ws0[l].reshape(1, W_B),
            bs0[l].reshape(1, W_B))
        oa_s, s_new = _delta_step(qkv_s, gb_s, gate_s, og, state_delta[l], tb=tb)
        xs = _out_ffn(xs, oa_s, ob_s, wo[l], gffn, wup[l], wdn[l], gl, tm=n_dec, final_norm=last)
        ds.append(s_new)
        cs.append(cnew.reshape(n_dec, CONV_W - 1, QKV))
        vs.append(vb_s.reshape(n_dec, 1, W_B))

    return (xp.reshape(batch, seq, d), xs.reshape(n_dec, 1, d), jnp.stack(dp), jnp.stack(cp),
            jnp.stack(ds), jnp.stack(cs), jnp.stack(vs))
```

```python
import functools

import jax
import jax.numpy as jnp
from jax import lax
from jax.experimental import pallas as pl
from jax.experimental.pallas import tpu as pltpu

F32 = jnp.float32
BF16 = jnp.bfloat16

H_A = 4
DK = 128
DV = 128
W_A = H_A * DV
QKV = 2 * H_A * DK + H_A * DV
CONV_W = 4
DN_CHUNK = 64
H_B = 4
DH_B = 128
W_B = H_B * DH_B
GM_CHUNK = 128
EPS = 1e-6

LANES = 128
SUBLANES = 8
AB_W = LANES
OFF_GATE = QKV
OFF_U = OFF_GATE + W_A
OFF_V = OFF_U + W_B
OFF_AB = OFF_V + W_B
PROJ_PAD = OFF_AB + AB_W

VMEM_LIMIT = 56 * 1024 * 1024
FF_BLOCK = 1024


def _rmsnorm(x, g):
    return x * lax.rsqrt(jnp.mean(x * x, axis=-1, keepdims=True) + EPS) * g


def _silu(x):
    return x * jax.nn.sigmoid(x)


def _softplus(x):
    return jnp.maximum(x, 0.0) + jnp.log1p(jnp.exp(-jnp.abs(x)))


def _const_spec(shape):
    nd = len(shape)
    return pl.BlockSpec(shape, lambda *_: (0,) * nd, pipeline_mode=pl.Buffered(1))


def _project(x, g, win_ref):
    h = _rmsnorm(x, g).astype(BF16)
    return jnp.dot(h, win_ref[...], preferred_element_type=F32)


def _qkv_activation(y):
    a = _silu(y)
    blocks = []
    for j in range(QKV // LANES):
        blk = a[:, j * LANES:(j + 1) * LANES]
        if j < 2 * H_A:
            blk = blk * lax.rsqrt(jnp.sum(blk * blk, axis=-1, keepdims=True) + EPS)
            if j < H_A:
                blk = blk * (DK ** -0.5)
        blocks.append(blk)
    return blocks


def _decay_beta(ab, alog, dtb):
    lane = lax.broadcasted_iota(jnp.int32, ab.shape, 1)
    gdec = -jnp.exp(alog) * _softplus(ab + dtb)
    beta = jax.nn.sigmoid(ab)
    return jnp.where(lane < H_A, gdec, beta)


def _prompt_in_kernel(x_ref, g_ref, win_ref, cw_ref, alog_ref, dtb_ref, vg_ref, ws_ref, bsb_ref,
                      qkv_ref, gate_ref, gb_ref, ob_ref, ctail_ref, xp_ref, *, tm, nt):
    i = pl.program_id(1)
    p = _project(x_ref[...], g_ref[...], win_ref)
    pq = p[:, :QKV]

    @pl.when(i == 0)
    def _():
        xp_ref[0:SUBLANES, :] = jnp.zeros((SUBLANES, QKV), F32)

    xp_ref[SUBLANES:SUBLANES + tm, :] = pq
    cw = cw_ref[...]
    y = pq * cw[CONV_W - 1:CONV_W, :]
    for j in range(CONV_W - 1):
        off = SUBLANES - (CONV_W - 1) + j
        y = y + xp_ref[off:off + tm, :] * cw[j:j + 1, :]
    xp_ref[0:SUBLANES, :] = xp_ref[tm:tm + SUBLANES, :]

    @pl.when(i == nt - 1)
    def _():
        ctail_ref[0] = pq[tm - (CONV_W - 1):tm, :]

    for j, blk in enumerate(_qkv_activation(y)):
        qkv_ref[:, j * LANES:(j + 1) * LANES] = blk
    gate_ref[...] = p[:, OFF_GATE:OFF_U]
    gb_ref[...] = _decay_beta(p[:, OFF_AB:PROJ_PAD], alog_ref[...], dtb_ref[...])

    u = p[:, OFF_U:OFF_V]
    vb = _rmsnorm(p[:, OFF_V:OFF_AB], vg_ref[...]).astype(BF16)
    nc = tm // GM_CHUNK
    ri = lax.broadcasted_iota(jnp.int32, (GM_CHUNK, GM_CHUNK), 0)
    ci = lax.broadcasted_iota(jnp.int32, (GM_CHUNK, GM_CHUNK), 1)
    for hh in range(H_B):
        wm = jnp.where(ri >= ci, ws_ref[hh], 0.0).astype(BF16)
        cols = slice(hh * DH_B, (hh + 1) * DH_B)
        rhs = jnp.concatenate([vb[c * GM_CHUNK:(c + 1) * GM_CHUNK, cols] for c in range(nc)], axis=1)
        s = jnp.dot(wm, rhs, preferred_element_type=F32)
        bias = bsb_ref[hh]
        for c in range(nc):
            rows = slice(c * GM_CHUNK, (c + 1) * GM_CHUNK)
            ob_ref[rows, cols] = u[rows, cols] * (s[:, c * DH_B:(c + 1) * DH_B] + bias)


def _prompt_in(x2d, g, win, cw, alog, dtb, vg, ws, bsb, *, batch, seq, tm):
    nt = seq // tm
    rows = batch * seq
    d = x2d.shape[1]
    row_spec = lambda w: pl.BlockSpec((tm, w), lambda b, i: (b * nt + i, 0))
    return pl.pallas_call(
        functools.partial(_prompt_in_kernel, tm=tm, nt=nt),
        grid=(batch, nt),
        in_specs=[row_spec(d), _const_spec((1, d)), _const_spec(win.shape), _const_spec(cw.shape),
                  _const_spec((1, AB_W)), _const_spec((1, AB_W)), _const_spec((1, W_B)),
                  _const_spec(ws.shape), _const_spec(bsb.shape)],
        out_specs=[row_spec(QKV), row_spec(W_A), row_spec(AB_W), row_spec(W_B),
                   pl.BlockSpec((1, CONV_W - 1, QKV), lambda b, i: (b, 0, 0))],
        out_shape=[jax.ShapeDtypeStruct((rows, QKV), F32), jax.ShapeDtypeStruct((rows, W_A), F32),
                   jax.ShapeDtypeStruct((rows, AB_W), F32), jax.ShapeDtypeStruct((rows, W_B), F32),
                   jax.ShapeDtypeStruct((batch, CONV_W - 1, QKV), F32)],
        scratch_shapes=[pltpu.VMEM((tm + SUBLANES, QKV), F32)],
        compiler_params=pltpu.CompilerParams(dimension_semantics=("arbitrary", "arbitrary"),
                                             vmem_limit_bytes=VMEM_LIMIT),
        name="prompt_in",
    )(x2d, g, win, cw, alog, dtb, vg, ws, bsb)


def _unit_lower_inverses(lmats, ri, ci):
    c = lmats[0].shape[0]
    eye = (ri == ci).astype(F32)
    xs = None
    s = 1
    while s < c:
        mask = ((ri // (2 * s)) == (ci // (2 * s))) & (((ri // s) % 2) == 1) & (((ci // s) % 2) == 0)
        css = [jnp.where(mask, lm, 0.0) for lm in lmats]
        if xs is None:
            xs = [eye - cs for cs in css]
        else:
            xbs = [x.astype(BF16) for x in xs]
            ys = [jnp.dot(xb, cs.astype(BF16), preferred_element_type=F32) for xb, cs in zip(xbs, css)]
            zs = [jnp.dot(y.astype(BF16), xb, preferred_element_type=F32) for y, xb in zip(ys, xbs)]
            xs = [x - z for x, z in zip(xs, zs)]
        s *= 2
    return xs


def _nt_dot(a, b):
    return lax.dot_general(a.astype(BF16), b.astype(BF16), (((1,), (1,)), ((), ())),
                           preferred_element_type=F32)


def _tn_dot(a, b):
    return lax.dot_general(a.astype(BF16), b.astype(BF16), (((0,), (0,)), ((), ())),
                           preferred_element_type=F32)


def _dot(a, b):
    return jnp.dot(a.astype(BF16), b.astype(BF16), preferred_element_type=F32)


def _delta_kernel(qkv_ref, gb_ref, gate_ref, og_ref, oa_ref, sout_ref,
                  s_ref, gcs_ref, u_ref, w_ref, qk_ref, kd_ref, *, nb, tm, nt):
    i = pl.program_id(1)
    c = DN_CHUNK

    @pl.when(i == 0)
    def _():
        s_ref[...] = jnp.zeros(s_ref.shape, F32)

    ri = lax.broadcasted_iota(jnp.int32, (c, c), 0)
    ci = lax.broadcasted_iota(jnp.int32, (c, c), 1)
    lower = ri >= ci
    strict = ri > ci
    tri = lower.astype(F32)
    og = og_ref[...]
    qcols = lambda hh: slice(hh * DK, (hh + 1) * DK)
    kcols = lambda hh: slice(H_A * DK + hh * DK, H_A * DK + (hh + 1) * DK)
    vcols = lambda hh: slice(2 * H_A * DK + hh * DV, 2 * H_A * DK + (hh + 1) * DV)

    def prepare(n, carry):
        rows = pl.ds(pl.multiple_of(n * c, c), c)
        chains = [(b, hh) for b in range(nb) for hh in range(H_A)]
        gbcs, gcss, gcsts = [], [], []
        for b in range(nb):
            gbc = gb_ref[b, rows, :]
            gcs = jnp.dot(tri, gbc, preferred_element_type=F32, precision=lax.Precision.HIGHEST)
            gcs_t = lax.dot_general(gbc, tri, (((0,), (1,)), ((), ())), preferred_element_type=F32,
                                    precision=lax.Precision.HIGHEST)
            gcs_ref[b, rows, :] = gcs
            gbcs.append(gbc)
            gcss.append(gcs)
            gcsts.append(gcs_t)
        ks, kbs, betas, g_cols, decays = [], [], [], [], []
        for b, hh in chains:
            k = qkv_ref[b, rows, kcols(hh)]
            g_col = jnp.broadcast_to(gcss[b][:, hh:hh + 1], (c, LANES))
            beta = jnp.broadcast_to(gbcs[b][:, H_A + hh:H_A + hh + 1], (c, LANES))
            g_row = jnp.broadcast_to(gcsts[b][hh:hh + 1, :], (c, c))
            decays.append(jnp.where(lower, jnp.exp(jnp.where(lower, g_col[:, :c] - g_row, 0.0)), 0.0))
            ks.append(k)
            kbs.append(k * beta)
            betas.append(beta)
            g_cols.append(g_col)
        kks = [_nt_dot(kb, k) for kb, k in zip(kbs, ks)]
        lmats = [jnp.where(strict, kk * decay, 0.0) for kk, decay in zip(kks, decays)]
        ts = _unit_lower_inverses(lmats, ri, ci)
        for j, (b, hh) in enumerate(chains):
            u_ref[b, rows, qcols(hh)] = _dot(ts[j], qkv_ref[b, rows, vcols(hh)] * betas[j])
        for j, (b, hh) in enumerate(chains):
            w_ref[b, rows, qcols(hh)] = _dot(ts[j], kbs[j] * jnp.exp(g_cols[j])).astype(BF16)
        for j, (b, hh) in enumerate(chains):
            qk = _nt_dot(qkv_ref[b, rows, qcols(hh)], ks[j])
            qk_ref[b, rows, hh * LANES:hh * LANES + c] = (qk * decays[j]).astype(BF16)
            kd_ref[b, rows, qcols(hh)] = (ks[j] * jnp.exp(g_cols[j][c - 1:c, :] - g_cols[j])).astype(BF16)
        return carry

    def recur(n, carry):
        rows = pl.ds(pl.multiple_of(n * c, c), c)
        chains = [(b, hh) for b in range(nb) for hh in range(H_A)]
        gcss = [gcs_ref[b, rows, :] for b in range(nb)]
        s_olds = [s_ref[b, hh] for b, hh in chains]
        s_bfs = [s.astype(BF16) for s in s_olds]
        wss = [jnp.dot(w_ref[b, rows, qcols(hh)], s_bf, preferred_element_type=F32)
               for (b, hh), s_bf in zip(chains, s_bfs)]
        v_bfs = [(u_ref[b, rows, qcols(hh)] - ws).astype(BF16) for (b, hh), ws in zip(chains, wss)]
        upds = [_tn_dot(kd_ref[b, rows, qcols(hh)], v_bf) for (b, hh), v_bf in zip(chains, v_bfs)]
        g_cols = [jnp.broadcast_to(gcss[b][:, hh:hh + 1], (c, LANES)) for b, hh in chains]
        for j, (b, hh) in enumerate(chains):
            s_ref[b, hh] = s_olds[j] * jnp.exp(g_cols[j][c - 1:c, :]) + upds[j]
        qss = [jnp.dot(qkv_ref[b, rows, qcols(hh)].astype(BF16), s_bf, preferred_element_type=F32)
               for (b, hh), s_bf in zip(chains, s_bfs)]
        qkvs = [jnp.dot(qk_ref[b, rows, hh * LANES:hh * LANES + c], v_bf, preferred_element_type=F32)
                for (b, hh), v_bf in zip(chains, v_bfs)]
        for j, (b, hh) in enumerate(chains):
            o = qss[j] * jnp.exp(g_cols[j]) + qkvs[j]
            oa_ref[b, rows, qcols(hh)] = _rmsnorm(o, og) * _silu(gate_ref[b, rows, qcols(hh)])
        return carry

    lax.fori_loop(0, tm // c, prepare, 0)
    lax.fori_loop(0, tm // c, recur, 0)

    @pl.when(i == nt - 1)
    def _():
        sout_ref[...] = s_ref[...]


def _delta_prompt(qkv, gb, gate, og, *, nb, tm):
    batch, seq, _ = qkv.shape
    nt = seq // tm
    blk = lambda w: pl.BlockSpec((nb, tm, w), lambda b, i: (b, i, 0))
    return pl.pallas_call(
        functools.partial(_delta_kernel, nb=nb, tm=tm, nt=nt),
        grid=(batch // nb, nt),
        in_specs=[blk(QKV), blk(AB_W), blk(W_A), _const_spec((1, DV))],
        out_specs=[blk(W_A), pl.BlockSpec((nb, H_A, DK, DV), lambda b, i: (b, 0, 0, 0))],
        out_shape=[jax.ShapeDtypeStruct((batch, seq, W_A), F32),
                   jax.ShapeDtypeStruct((batch, H_A, DK, DV), F32)],
        scratch_shapes=[pltpu.VMEM((nb, H_A, DK, DV), F32),
                        pltpu.VMEM((nb, tm, AB_W), F32),
                        pltpu.VMEM((nb, tm, W_A), F32),
                        pltpu.VMEM((nb, tm, W_A), BF16),
                        pltpu.VMEM((nb, tm, H_A * LANES), BF16),
                        pltpu.VMEM((nb, tm, W_A), BF16)],
        compiler_params=pltpu.CompilerParams(dimension_semantics=("arbitrary", "arbitrary"),
                                             vmem_limit_bytes=VMEM_LIMIT),
        name="delta_prompt",
    )(qkv, gb, gate, og)


def _out_ffn_kernel(x_ref, oa_ref, ob_ref, wo_ref, gf_ref, wup_ref, wdn_ref, gl_ref, out_ref, *, final_norm):
    y = jnp.dot(oa_ref[...].astype(BF16), wo_ref[0:W_A, :], preferred_element_type=F32)
    y = y + jnp.dot(ob_ref[...].astype(BF16), wo_ref[W_A:W_A + W_B, :], preferred_element_type=F32)
    x1 = x_ref[...] + y
    h = _rmsnorm(x1, gf_ref[...]).astype(BF16)
    d_ff = wup_ref.shape[1]
    ffn = None
    for j in range(d_ff // FF_BLOCK):
        cols = slice(j * FF_BLOCK, (j + 1) * FF_BLOCK)
        a = jnp.dot(h, wup_ref[:, cols], preferred_element_type=F32)
        a = jnp.square(jnp.maximum(a, 0.0)).astype(BF16)
        part = jnp.dot(a, wdn_ref[cols, :], preferred_element_type=F32)
        ffn = part if ffn is None else ffn + part
    x2 = x1 + ffn
    if final_norm:
        x2 = _rmsnorm(x2, gl_ref[...])
    out_ref[...] = x2


def _out_ffn(x2d, oa, ob, wo, gf, wup, wdn, gl, *, tm, final_norm):
    rows, d = x2d.shape
    row_spec = lambda w: pl.BlockSpec((tm, w), lambda i: (i, 0))
    return pl.pallas_call(
        functools.partial(_out_ffn_kernel, final_norm=final_norm),
        grid=(rows // tm,),
        in_specs=[row_spec(d), row_spec(W_A), row_spec(W_B), _const_spec(wo.shape), _const_spec((1, d)),
                  _const_spec(wup.shape), _const_spec(wdn.shape), _const_spec((1, d))],
        out_specs=row_spec(d),
        out_shape=jax.ShapeDtypeStruct((rows, d), F32),
        compiler_params=pltpu.CompilerParams(dimension_semantics=("arbitrary",),
                                             vmem_limit_bytes=VMEM_LIMIT),
        name="out_ffn",
    )(x2d, oa, ob, wo, gf, wup, wdn, gl)


def _sample_in_kernel(x_ref, g_ref, win_ref, cw_ref, c0_ref, c1_ref, c2_ref, alog_ref, dtb_ref, vg_ref,
                      ws0_ref, bs0_ref, qkv_ref, gate_ref, gb_ref, ob_ref, vb_ref, cnew_ref):
    p = _project(x_ref[...], g_ref[...], win_ref)
    pq = p[:, :QKV]
    cw = cw_ref[...]
    c1 = c1_ref[...]
    c2 = c2_ref[...]
    y = c0_ref[...] * cw[0:1, :] + c1 * cw[1:2, :] + c2 * cw[2:3, :] + pq * cw[3:4, :]
    cnew_ref[:, 0:QKV] = c1
    cnew_ref[:, QKV:2 * QKV] = c2
    cnew_ref[:, 2 * QKV:3 * QKV] = pq
    for j, blk in enumerate(_qkv_activation(y)):
        qkv_ref[:, j * LANES:(j + 1) * LANES] = blk
    gate_ref[...] = p[:, OFF_GATE:OFF_U]
    gb_ref[...] = _decay_beta(p[:, OFF_AB:PROJ_PAD], alog_ref[...], dtb_ref[...])
    vb = _rmsnorm(p[:, OFF_V:OFF_AB], vg_ref[...])
    vb_ref[...] = vb
    ob_ref[...] = p[:, OFF_U:OFF_V] * (vb * ws0_ref[...] + bs0_ref[...])


def _sample_in(x2d, g, win, cw, conv_flat, alog, dtb, vg, ws0, bs0):
    n, d = x2d.shape
    full = lambda w: pl.BlockSpec((n, w), lambda i: (0, 0))
    conv_col = lambda j: pl.BlockSpec((n, QKV), lambda i: (0, j))
    return pl.pallas_call(
        _sample_in_kernel,
        grid=(1,),
        in_specs=[full(d), pl.BlockSpec((1, d), lambda i: (0, 0)),
                  pl.BlockSpec(win.shape, lambda i: (0, 0)), pl.BlockSpec(cw.shape, lambda i: (0, 0)),
                  conv_col(0), conv_col(1), conv_col(2),
                  pl.BlockSpec((1, AB_W), lambda i: (0, 0)), pl.BlockSpec((1, AB_W), lambda i: (0, 0)),
                  pl.BlockSpec((1, W_B), lambda i: (0, 0)), pl.BlockSpec((1, W_B), lambda i: (0, 0)),
                  pl.BlockSpec((1, W_B), lambda i: (0, 0))],
        out_specs=[full(QKV), full(W_A), full(AB_W), full(W_B), full(W_B), full((CONV_W - 1) * QKV)],
        out_shape=[jax.ShapeDtypeStruct((n, QKV), F32), jax.ShapeDtypeStruct((n, W_A), F32),
                   jax.ShapeDtypeStruct((n, AB_W), F32), jax.ShapeDtypeStruct((n, W_B), F32),
                   jax.ShapeDtypeStruct((n, W_B), F32),
                   jax.ShapeDtypeStruct((n, (CONV_W - 1) * QKV), F32)],
        compiler_params=pltpu.CompilerParams(dimension_semantics=("arbitrary",),
                                             vmem_limit_bytes=VMEM_LIMIT),
        name="sample_in",
    )(x2d, g, win, cw, conv_flat, conv_flat, conv_flat, alog, dtb, vg, ws0, bs0)


def _delta_step_kernel(qkv_ref, gb_ref, gate_ref, og_ref, s_ref, oa_ref, snew_ref, *, tb):
    og = og_ref[...]
    gb = gb_ref[...]
    for hh in range(H_A):
        q = qkv_ref[:, hh * DK:(hh + 1) * DK]
        k = qkv_ref[:, H_A * DK + hh * DK:H_A * DK + (hh + 1) * DK]
        v = qkv_ref[:, 2 * H_A * DK + hh * DV:2 * H_A * DK + (hh + 1) * DV]
        q_t = q.T
        k_t = k.T
        decay = jnp.exp(gb[:, hh:hh + 1])
        beta = gb[:, H_A + hh:H_A + hh + 1]
        o_rows = []
        for t in range(tb):
            s = s_ref[t, hh] * decay[t:t + 1, :]
            k_col = jnp.broadcast_to(k_t[:, t:t + 1], (DK, DV))
            q_col = jnp.broadcast_to(q_t[:, t:t + 1], (DK, DV))
            kv = jnp.sum(s * k_col, axis=0, keepdims=True)
            delta = (v[t:t + 1, :] - kv) * beta[t:t + 1, :]
            s = s + k_col * delta
            snew_ref[t, hh] = s
            o_rows.append(jnp.sum(s * q_col, axis=0, keepdims=True))
        o = jnp.concatenate(o_rows, axis=0)
        gate = gate_ref[:, hh * DV:(hh + 1) * DV]
        oa_ref[:, hh * DV:(hh + 1) * DV] = _rmsnorm(o, og) * _silu(gate)


def _delta_step(qkv, gb, gate, og, state, *, tb):
    n = qkv.shape[0]
    row_spec = lambda w: pl.BlockSpec((tb, w), lambda i: (i, 0))
    st_spec = pl.BlockSpec((tb, H_A, DK, DV), lambda i: (i, 0, 0, 0))
    return pl.pallas_call(
        functools.partial(_delta_step_kernel, tb=tb),
        grid=(n // tb,),
        in_specs=[row_spec(QKV), row_spec(AB_W), row_spec(W_A), pl.BlockSpec((1, DV), lambda i: (0, 0)),
                  st_spec],
        out_specs=[row_spec(W_A), st_spec],
        out_shape=[jax.ShapeDtypeStruct((n, W_A), F32), jax.ShapeDtypeStruct(state.shape, F32)],
        compiler_params=pltpu.CompilerParams(dimension_semantics=("arbitrary",),
                                             vmem_limit_bytes=VMEM_LIMIT),
        name="delta_step",
    )(qkv, gb, gate, og, state)


def _pad_lanes(v, width):
    v = v.reshape(1, -1).astype(F32)
    return jnp.pad(v, ((0, 0), (0, width - v.shape[1])))


def _pick_tile(seq):
    for tm in (512, 256, 128):
        if seq % tm == 0:
            return tm
    raise ValueError("prompt length must be a multiple of 128")


def kernel(x_prompt, x_sample, state_delta, state_conv, norm_mix_g, w_in, conv_w, A_log, dt_bias, o_norm_g,
           v_norm_g, w_s, b_s, w_o, norm_ffn_g, w_up, w_down, norm_f_g):
    batch, seq, d = x_prompt.shape
    n_dec, dec_seq, _ = x_sample.shape
    depth = w_in.shape[0]
    assert dec_seq == 1 and seq % GM_CHUNK == 0 and n_dec % SUBLANES == 0
    tm = _pick_tile(seq)
    tb = SUBLANES
    nb = next(n for n in (4, 2, 1) if batch % n == 0)
    dtm = 256 if seq % 256 == 0 else GM_CHUNK

    off_a = QKV + W_A
    off_u = off_a + 2 * H_A
    off_v = off_u + W_B
    win = jnp.concatenate(
        [w_in[:, :, :off_a], w_in[:, :, off_u:off_v], w_in[:, :, off_v:], w_in[:, :, off_a:off_u],
         jnp.zeros((depth, d, AB_W - 2 * H_A), w_in.dtype)], axis=-1).astype(BF16)
    wo = w_o.astype(BF16)
    wup = w_up.astype(BF16)
    wdn = w_down.astype(BF16)
    bsb = jnp.broadcast_to(b_s[:, :, :, None], b_s.shape + (DH_B,)).astype(F32)
    ws0 = jnp.repeat(w_s[:, :, 0, 0], DH_B, axis=-1)
    bs0 = jnp.repeat(b_s[:, :, 0], DH_B, axis=-1)
    gl = norm_f_g.reshape(1, d)

    xp = x_prompt.reshape(batch * seq, d)
    xs = x_sample.reshape(n_dec, d)
    dp, cp, ds, cs, vs = [], [], [], [], []
    for l in range(depth):
        last = l == depth - 1
        gmix = norm_mix_g[l].reshape(1, d)
        gffn = norm_ffn_g[l].reshape(1, d)
        alog = _pad_lanes(A_log[l], AB_W)
        dtb = _pad_lanes(jnp.concatenate([dt_bias[l], jnp.zeros((H_A,), F32)]), AB_W)
        vg = v_norm_g[l].reshape(1, W_B)
        og = o_norm_g[l].reshape(1, DV)

        qkv, gate, gb, ob, ctail = _prompt_in(xp, gmix, win[l], conv_w[l], alog, dtb, vg, w_s[l], bsb[l],
                                              batch=batch, seq=seq, tm=tm)
        oa, s_fin = _delta_prompt(qkv.reshape(batch, seq, QKV), gb.reshape(batch, seq, AB_W),
                                  gate.reshape(batch, seq, W_A), og, nb=nb, tm=dtm)
        xp = _out_ffn(xp, oa.reshape(batch * seq, W_A), ob, wo[l], gffn, wup[l], wdn[l], gl, tm=tm,
                      final_norm=last)
        dp.append(s_fin)
        cp.append(ctail)

        conv_flat = state_conv[l].reshape(n_dec, (CONV_W - 1) * QKV)
        qkv_s, gate_s, gb_s, ob_s, vb_s, cnew = _sample_in(
            xs, gmix, win[l], conv_w[l], conv_flat, alog, dtb, vg, ws0[l].reshape(1, W_B),
            bs0[l].reshape(1, W_B))
        oa_s, s_new = _delta_step(qkv_s, gb_s, gate_s, og, state_delta[l], tb=tb)
        xs = _out_ffn(xs, oa_s, ob_s, wo[l], gffn, wup[l], wdn[l], gl, tm=n_dec, final_norm=last)
        ds.append(s_new)
        cs.append(cnew.reshape(n_dec, CONV_W - 1, QKV))
        vs.append(vb_s.reshape(n_dec, 1, W_B))

    return (xp.reshape(batch, seq, d), xs.reshape(n_dec, 1, d), jnp.stack(dp), jnp.stack(cp),
            jnp.stack(ds), jnp.stack(cs), jnp.stack(vs))
```

```python
import functools

import jax
import jax.numpy as jnp
from jax import lax
from jax.experimental import pallas as pl
from jax.experimental.pallas import tpu as pltpu

F32 = jnp.float32
BF16 = jnp.bfloat16

H_A = 4
DK = 128
DV = 128
W_A = H_A * DV
QKV = 2 * H_A * DK + H_A * DV
CONV_W = 4
DN_CHUNK = 64
H_B = 4
DH_B = 128
W_B = H_B * DH_B
GM_CHUNK = 128
EPS = 1e-6

LANES = 128
SUBLANES = 8
AB_W = LANES
OFF_GATE = QKV
OFF_U = OFF_GATE + W_A
OFF_V = OFF_U + W_B
OFF_AB = OFF_V + W_B
PROJ_PAD = OFF_AB + AB_W
CONV_BLK = H_A * DK

VMEM_LIMIT = 56 * 1024 * 1024
FF_BLOCK = 1024


def _rmsnorm(x, g):
    return x * lax.rsqrt(jnp.mean(x * x, axis=-1, keepdims=True) + EPS) * g


def _silu(x):
    return x * jax.nn.sigmoid(x)


def _softplus(x):
    return jnp.maximum(x, 0.0) + jnp.log1p(jnp.exp(-jnp.abs(x)))


def _layer_spec(arr, layer):
    nd = arr.ndim - 1
    return pl.BlockSpec((None,) + arr.shape[1:], lambda *_: (layer,) + (0,) * nd,
                        pipeline_mode=pl.Buffered(1))


def _qkv_activation(y, j):
    a = _silu(y)
    if j >= 2:
        return a
    blocks = []
    for hh in range(H_A):
        blk = a[:, hh * DK:(hh + 1) * DK]
        blk = blk * lax.rsqrt(jnp.sum(blk * blk, axis=-1, keepdims=True) + EPS)
        if j == 0:
            blk = blk * (DK ** -0.5)
        blocks.append(blk)
    return jnp.concatenate(blocks, axis=1)


def _decay_beta(ab, alog, dtb):
    lane = lax.broadcasted_iota(jnp.int32, ab.shape, 1)
    gdec = -jnp.exp(alog) * _softplus(ab + dtb)
    beta = jax.nn.sigmoid(ab)
    return jnp.where(lane < H_A, gdec, beta)


def _prompt_in_kernel(x_ref, g_ref, win_ref, cw_ref, alog_ref, dtb_ref, vg_ref, ws_ref, bsb_ref,
                      qkv_ref, gate_ref, gb_ref, ob_ref, ctail_ref, carry_ref, *, tm, nt):
    i = pl.program_id(1)
    h = _rmsnorm(x_ref[...], g_ref[...]).astype(BF16)
    proj = lambda lo, hi: jnp.dot(h, win_ref[:, lo:hi], preferred_element_type=F32)

    @pl.when(i == 0)
    def _():
        carry_ref[...] = jnp.zeros(carry_ref.shape, F32)

    sub = lax.broadcasted_iota(jnp.int32, (SUBLANES, CONV_BLK), 0)

    def conv_act(pj, j):
        cols = slice(j * CONV_BLK, (j + 1) * CONV_BLK)
        cw = cw_ref[:, cols]
        prev = carry_ref[:, cols]
        y = pj * cw[CONV_W - 1:CONV_W, :]
        for s in range(1, CONV_W):
            sh = pltpu.roll(pj, s, axis=0)
            head = jnp.where(sub < s, pltpu.roll(prev, s, axis=0), sh[0:SUBLANES])
            sh = jnp.concatenate([head, sh[SUBLANES:]], axis=0)
            y = y + sh * cw[CONV_W - 1 - s:CONV_W - s, :]
        carry_ref[:, cols] = pj[tm - SUBLANES:tm, :]

        @pl.when(i == nt - 1)
        def _():
            ctail_ref[0, :, cols] = pj[tm - (CONV_W - 1):tm, :]

        qkv_ref[:, cols] = _qkv_activation(y, j)

    def gmlp_mix(pv):
        vb = _rmsnorm(pv, vg_ref[...]).astype(BF16)
        nc = tm // GM_CHUNK
        ri = lax.broadcasted_iota(jnp.int32, (GM_CHUNK, GM_CHUNK), 0)
        ci = lax.broadcasted_iota(jnp.int32, (GM_CHUNK, GM_CHUNK), 1)
        mixed = []
        for hh in range(H_B):
            wm = jnp.where(ri >= ci, ws_ref[hh], 0.0).astype(BF16)
            cols = slice(hh * DH_B, (hh + 1) * DH_B)
            rhs = jnp.concatenate([vb[c * GM_CHUNK:(c + 1) * GM_CHUNK, cols] for c in range(nc)], axis=1)
            mixed.append(jnp.dot(wm, rhs, preferred_element_type=F32))
        return mixed

    def gmlp_gate(pu, mixed):
        for hh in range(H_B):
            cols = slice(hh * DH_B, (hh + 1) * DH_B)
            bias = bsb_ref[hh]
            for c in range(tm // GM_CHUNK):
                rows = slice(c * GM_CHUNK, (c + 1) * GM_CHUNK)
                ob_ref[rows, cols] = pu[rows, cols] * (mixed[hh][:, c * DH_B:(c + 1) * DH_B] + bias)

    for j in range(QKV // CONV_BLK):
        conv_act(proj(j * CONV_BLK, (j + 1) * CONV_BLK), j)
    mixed = gmlp_mix(proj(OFF_V, OFF_AB))
    gmlp_gate(proj(OFF_U, OFF_V), mixed)
    gb_ref[...] = _decay_beta(proj(OFF_AB, PROJ_PAD), alog_ref[...], dtb_ref[...])
    gate_ref[...] = proj(OFF_GATE, OFF_U)


def _prompt_in(x2d, layer, prm, *, batch, seq, tm):
    nt = seq // tm
    rows = batch * seq
    d = x2d.shape[1]
    row_spec = lambda w: pl.BlockSpec((tm, w), lambda b, i: (b * nt + i, 0))
    names = ("gmix", "win", "cw", "alog", "dtb", "vg", "ws", "bsb")
    return pl.pallas_call(
        functools.partial(_prompt_in_kernel, tm=tm, nt=nt),
        grid=(batch, nt),
        in_specs=[row_spec(d)] + [_layer_spec(prm[n], layer) for n in names],
        out_specs=[row_spec(QKV), row_spec(W_A), row_spec(AB_W), row_spec(W_B),
                   pl.BlockSpec((1, CONV_W - 1, QKV), lambda b, i: (b, 0, 0))],
        out_shape=[jax.ShapeDtypeStruct((rows, QKV), F32), jax.ShapeDtypeStruct((rows, W_A), F32),
                   jax.ShapeDtypeStruct((rows, AB_W), F32), jax.ShapeDtypeStruct((rows, W_B), F32),
                   jax.ShapeDtypeStruct((batch, CONV_W - 1, QKV), F32)],
        scratch_shapes=[pltpu.VMEM((SUBLANES, QKV), F32)],
        compiler_params=pltpu.CompilerParams(dimension_semantics=("arbitrary", "arbitrary"),
                                             vmem_limit_bytes=VMEM_LIMIT),
        name="prompt_in",
    )(x2d, *[prm[n] for n in names])


def _unit_lower_inverses(lmats, ri, ci):
    c = lmats[0].shape[0]
    eye = (ri == ci).astype(F32)
    xs = None
    s = 1
    while s < c:
        mask = ((ri // (2 * s)) == (ci // (2 * s))) & (((ri // s) % 2) == 1) & (((ci // s) % 2) == 0)
        css = [jnp.where(mask, lm, 0.0) for lm in lmats]
        if xs is None:
            xs = [eye - cs for cs in css]
        else:
            xbs = [x.astype(BF16) for x in xs]
            ys = [jnp.dot(xb, cs.astype(BF16), preferred_element_type=F32) for xb, cs in zip(xbs, css)]
            zs = [jnp.dot(y.astype(BF16), xb, preferred_element_type=F32) for y, xb in zip(ys, xbs)]
            xs = [x - z for x, z in zip(xs, zs)]
        s *= 2
    return xs


def _nt_dot(a, b):
    return lax.dot_general(a.astype(BF16), b.astype(BF16), (((1,), (1,)), ((), ())),
                           preferred_element_type=F32)


def _tn_dot(a, b):
    return lax.dot_general(a.astype(BF16), b.astype(BF16), (((0,), (0,)), ((), ())),
                           preferred_element_type=F32)


def _dot(a, b):
    return jnp.dot(a.astype(BF16), b.astype(BF16), preferred_element_type=F32)


def _delta_kernel(qkv_ref, gb_ref, gate_ref, og_ref, oa_ref, sout_ref,
                  s_ref, gcs_ref, u_ref, w_ref, qk_ref, kd_ref, *, nb, tm, nt):
    i = pl.program_id(1)
    c = DN_CHUNK

    @pl.when(i == 0)
    def _():
        s_ref[...] = jnp.zeros(s_ref.shape, F32)

    ri = lax.broadcasted_iota(jnp.int32, (c, c), 0)
    ci = lax.broadcasted_iota(jnp.int32, (c, c), 1)
    lower = ri >= ci
    strict = ri > ci
    tri = lower.astype(F32)
    og = og_ref[...]
    chains = [(b, hh) for b in range(nb) for hh in range(H_A)]
    qcols = lambda hh: slice(hh * DK, (hh + 1) * DK)
    kcols = lambda hh: slice(H_A * DK + hh * DK, H_A * DK + (hh + 1) * DK)
    vcols = lambda hh: slice(2 * H_A * DK + hh * DV, 2 * H_A * DK + (hh + 1) * DV)

    def prepare(n, carry):
        rows = pl.ds(pl.multiple_of(n * c, c), c)
        gbcs, gcss, gcsts = [], [], []
        for b in range(nb):
            gbc = gb_ref[b, rows, :]
            gcs = jnp.dot(tri, gbc, preferred_element_type=F32, precision=lax.Precision.HIGHEST)
            gcs_t = lax.dot_general(gbc, tri, (((0,), (1,)), ((), ())), preferred_element_type=F32,
                                    precision=lax.Precision.HIGHEST)
            gcs_ref[b, rows, :] = gcs
            gbcs.append(gbc)
            gcss.append(gcs)
            gcsts.append(gcs_t)
        ks, kbs, betas, g_cols, decays = [], [], [], [], []
        for b, hh in chains:
            k = qkv_ref[b, rows, kcols(hh)]
            g_col = jnp.broadcast_to(gcss[b][:, hh:hh + 1], (c, LANES))
            beta = jnp.broadcast_to(gbcs[b][:, H_A + hh:H_A + hh + 1], (c, LANES))
            g_row = jnp.broadcast_to(gcsts[b][hh:hh + 1, :], (c, c))
            decays.append(jnp.where(lower, jnp.exp(jnp.where(lower, g_col[:, :c] - g_row, 0.0)), 0.0))
            ks.append(k)
            kbs.append(k * beta)
            betas.append(beta)
            g_cols.append(g_col)
        kks = [_nt_dot(kb, k) for kb, k in zip(kbs, ks)]
        lmats = [jnp.where(strict, kk * decay, 0.0) for kk, decay in zip(kks, decays)]
        ts = _unit_lower_inverses(lmats, ri, ci)
        for j, (b, hh) in enumerate(chains):
            u_ref[b, rows, qcols(hh)] = _dot(ts[j], qkv_ref[b, rows, vcols(hh)] * betas[j])
        for j, (b, hh) in enumerate(chains):
            w_ref[b, rows, qcols(hh)] = _dot(ts[j], kbs[j] * jnp.exp(g_cols[j])).astype(BF16)
        for j, (b, hh) in enumerate(chains):
            qk = _nt_dot(qkv_ref[b, rows, qcols(hh)], ks[j])
            qk_ref[b, rows, hh * LANES:hh * LANES + c] = (qk * decays[j]).astype(BF16)
            kd_ref[b, rows, qcols(hh)] = (ks[j] * jnp.exp(g_cols[j][c - 1:c, :] - g_cols[j])).astype(BF16)
        return carry

    def recur(n, carry):
        rows = pl.ds(pl.multiple_of(n * c, c), c)
        gcss = [gcs_ref[b, rows, :] for b in range(nb)]
        s_olds = [s_ref[b, hh] for b, hh in chains]
        s_bfs = [s.astype(BF16) for s in s_olds]
        wss = [jnp.dot(w_ref[b, rows, qcols(hh)], s_bf, preferred_element_type=F32)
               for (b, hh), s_bf in zip(chains, s_bfs)]
        v_bfs = [(u_ref[b, rows, qcols(hh)] - ws).astype(BF16) for (b, hh), ws in zip(chains, wss)]
        upds = [_tn_dot(kd_ref[b, rows, qcols(hh)], v_bf) for (b, hh), v_bf in zip(chains, v_bfs)]
        g_cols = [jnp.broadcast_to(gcss[b][:, hh:hh + 1], (c, LANES)) for b, hh in chains]
        for j, (b, hh) in enumerate(chains):
            s_ref[b, hh] = s_olds[j] * jnp.exp(g_cols[j][c - 1:c, :]) + upds[j]
        qss = [jnp.dot(qkv_ref[b, rows, qcols(hh)].astype(BF16), s_bf, preferred_element_type=F32)
               for (b, hh), s_bf in zip(chains, s_bfs)]
        qkvs = [jnp.dot(qk_ref[b, rows, hh * LANES:hh * LANES + c], v_bf, preferred_element_type=F32)
                for (b, hh), v_bf in zip(chains, v_bfs)]
        for j, (b, hh) in enumerate(chains):
            o = qss[j] * jnp.exp(g_cols[j]) + qkvs[j]
            oa_ref[b, rows, qcols(hh)] = _rmsnorm(o, og) * _silu(gate_ref[b, rows, qcols(hh)])
        return carry

    lax.fori_loop(0, tm // c, prepare, 0)
    lax.fori_loop(0, tm // c, recur, 0)

    @pl.when(i == nt - 1)
    def _():
        sout_ref[...] = s_ref[...]


def _delta_prompt(qkv, gb, gate, layer, prm, *, nb, tm):
    batch, seq, _ = qkv.shape
    nt = seq // tm
    blk = lambda w: pl.BlockSpec((nb, tm, w), lambda b, i: (b, i, 0))
    return pl.pallas_call(
        functools.partial(_delta_kernel, nb=nb, tm=tm, nt=nt),
        grid=(batch // nb, nt),
        in_specs=[blk(QKV), blk(AB_W), blk(W_A), _layer_spec(prm["og"], layer)],
        out_specs=[blk(W_A), pl.BlockSpec((nb, H_A, DK, DV), lambda b, i: (b, 0, 0, 0))],
        out_shape=[jax.ShapeDtypeStruct((batch, seq, W_A), F32),
                   jax.ShapeDtypeStruct((batch, H_A, DK, DV), F32)],
        scratch_shapes=[pltpu.VMEM((nb, H_A, DK, DV), F32),
                        pltpu.VMEM((nb, tm, AB_W), F32),
                        pltpu.VMEM((nb, tm, W_A), F32),
                        pltpu.VMEM((nb, tm, W_A), BF16),
                        pltpu.VMEM((nb, tm, H_A * LANES), BF16),
                        pltpu.VMEM((nb, tm, W_A), BF16)],
        compiler_params=pltpu.CompilerParams(dimension_semantics=("arbitrary", "arbitrary"),
                                             vmem_limit_bytes=VMEM_LIMIT),
        name="delta_prompt",
    )(qkv, gb, gate, prm["og"])


def _out_ffn_kernel(x_ref, oa_ref, ob_ref, wo_ref, gf_ref, wup_ref, wdn_ref, gl_ref, out_ref, *, final_norm):
    y = jnp.dot(oa_ref[...].astype(BF16), wo_ref[0:W_A, :], preferred_element_type=F32)
    y = y + jnp.dot(ob_ref[...].astype(BF16), wo_ref[W_A:W_A + W_B, :], preferred_element_type=F32)
    x1 = x_ref[...] + y
    h = _rmsnorm(x1, gf_ref[...]).astype(BF16)
    d_ff = wup_ref.shape[1]
    ffn = None
    for j in range(d_ff // FF_BLOCK):
        cols = slice(j * FF_BLOCK, (j + 1) * FF_BLOCK)
        a = jnp.dot(h, wup_ref[:, cols], preferred_element_type=F32)
        a = jnp.square(jnp.maximum(a, 0.0)).astype(BF16)
        part = jnp.dot(a, wdn_ref[cols, :], preferred_element_type=F32)
        ffn = part if ffn is None else ffn + part
    x2 = x1 + ffn
    if final_norm:
        x2 = _rmsnorm(x2, gl_ref[...])
    out_ref[...] = x2


def _out_ffn(x2d, oa, ob, layer, prm, *, tm, final_norm):
    rows, d = x2d.shape
    row_spec = lambda w: pl.BlockSpec((tm, w), lambda i: (i, 0))
    names = ("wo", "gffn", "wup", "wdn")
    return pl.pallas_call(
        functools.partial(_out_ffn_kernel, final_norm=final_norm),
        grid=(rows // tm,),
        in_specs=[row_spec(d), row_spec(W_A), row_spec(W_B)] + [_layer_spec(prm[n], layer) for n in names]
                 + [pl.BlockSpec((1, d), lambda i: (0, 0))],
        out_specs=row_spec(d),
        out_shape=jax.ShapeDtypeStruct((rows, d), F32),
        compiler_params=pltpu.CompilerParams(dimension_semantics=("arbitrary",),
                                             vmem_limit_bytes=VMEM_LIMIT),
        name="out_ffn",
    )(x2d, oa, ob, *[prm[n] for n in names], prm["gl"])


def _sample_in_kernel(x_ref, g_ref, win_ref, cw_ref, alog_ref, dtb_ref, vg_ref, ws0_ref, bs0_ref,
                      c0_ref, c1_ref, c2_ref, qkv_ref, gate_ref, gb_ref, ob_ref, vb_ref, cnew_ref):
    h = _rmsnorm(x_ref[...], g_ref[...]).astype(BF16)
    p = jnp.dot(h, win_ref[...], preferred_element_type=F32)
    pq = p[:, :QKV]
    cw = cw_ref[...]
    c1 = c1_ref[...]
    c2 = c2_ref[...]
    y = c0_ref[...] * cw[0:1, :] + c1 * cw[1:2, :] + c2 * cw[2:3, :] + pq * cw[3:4, :]
    cnew_ref[:, 0:QKV] = c1
    cnew_ref[:, QKV:2 * QKV] = c2
    cnew_ref[:, 2 * QKV:3 * QKV] = pq
    for j in range(QKV // CONV_BLK):
        cols = slice(j * CONV_BLK, (j + 1) * CONV_BLK)
        qkv_ref[:, cols] = _qkv_activation(y[:, cols], j)
    gate_ref[...] = p[:, OFF_GATE:OFF_U]
    gb_ref[...] = _decay_beta(p[:, OFF_AB:PROJ_PAD], alog_ref[...], dtb_ref[...])
    vb = _rmsnorm(p[:, OFF_V:OFF_AB], vg_ref[...])
    vb_ref[...] = vb
    ob_ref[...] = p[:, OFF_U:OFF_V] * (vb * ws0_ref[...] + bs0_ref[...])


def _sample_in(x2d, conv_flat, layer, prm):
    n, d = x2d.shape
    full = lambda w: pl.BlockSpec((n, w), lambda i: (0, 0))
    conv_col = lambda j: pl.BlockSpec((None, n, QKV), lambda i: (layer, 0, j))
    names = ("gmix", "win", "cw", "alog", "dtb", "vg", "ws0", "bs0")
    return pl.pallas_call(
        _sample_in_kernel,
        grid=(1,),
        in_specs=[full(d)] + [_layer_spec(prm[k], layer) for k in names] + [conv_col(j) for j in range(CONV_W - 1)],
        out_specs=[full(QKV), full(W_A), full(AB_W), full(W_B), full(W_B), full((CONV_W - 1) * QKV)],
        out_shape=[jax.ShapeDtypeStruct((n, QKV), F32), jax.ShapeDtypeStruct((n, W_A), F32),
                   jax.ShapeDtypeStruct((n, AB_W), F32), jax.ShapeDtypeStruct((n, W_B), F32),
                   jax.ShapeDtypeStruct((n, W_B), F32),
                   jax.ShapeDtypeStruct((n, (CONV_W - 1) * QKV), F32)],
        compiler_params=pltpu.CompilerParams(dimension_semantics=("arbitrary",),
                                             vmem_limit_bytes=VMEM_LIMIT),
        name="sample_in",
    )(x2d, *[prm[k] for k in names], conv_flat, conv_flat, conv_flat)


def _delta_step_kernel(*refs, tb, chained):
    if chained:
        qkv_ref, gb_ref, gate_ref, og_ref, s_ref, _, oa_ref, snew_ref = refs
    else:
        qkv_ref, gb_ref, gate_ref, og_ref, s_ref, oa_ref, snew_ref = refs
    og = og_ref[...]
    gb = gb_ref[...]
    for hh in range(H_A):
        q = qkv_ref[:, hh * DK:(hh + 1) * DK]
        k = qkv_ref[:, H_A * DK + hh * DK:H_A * DK + (hh + 1) * DK]
        v = qkv_ref[:, 2 * H_A * DK + hh * DV:2 * H_A * DK + (hh + 1) * DV]
        q_t = q.T
        k_t = k.T
        decay = jnp.exp(gb[:, hh:hh + 1])
        beta = gb[:, H_A + hh:H_A + hh + 1]
        o_rows = []
        for t in range(tb):
            s = s_ref[t, hh] * decay[t:t + 1, :]
            k_col = jnp.broadcast_to(k_t[:, t:t + 1], (DK, DV))
            q_col = jnp.broadcast_to(q_t[:, t:t + 1], (DK, DV))
            kv = jnp.sum(s * k_col, axis=0, keepdims=True)
            delta = (v[t:t + 1, :] - kv) * beta[t:t + 1, :]
            s = s + k_col * delta
            snew_ref[t, hh] = s
            o_rows.append(jnp.sum(s * q_col, axis=0, keepdims=True))
        o = jnp.concatenate(o_rows, axis=0)
        gate = gate_ref[:, hh * DV:(hh + 1) * DV]
        oa_ref[:, hh * DV:(hh + 1) * DV] = _rmsnorm(o, og) * _silu(gate)


def _delta_step(qkv, gb, gate, state_all, new_all, layer, prm, *, tb):
    n = qkv.shape[0]
    row_spec = lambda w: pl.BlockSpec((tb, w), lambda i: (i, 0))
    st_spec = pl.BlockSpec((None, tb, H_A, DK, DV), lambda i: (layer, i, 0, 0, 0))
    chained = new_all is not None
    in_specs = [row_spec(QKV), row_spec(AB_W), row_spec(W_A), _layer_spec(prm["og"], layer), st_spec]
    args = [qkv, gb, gate, prm["og"], state_all]
    if chained:
        in_specs.append(pl.BlockSpec(memory_space=pl.ANY))
        args.append(new_all)
    return pl.pallas_call(
        functools.partial(_delta_step_kernel, tb=tb, chained=chained),
        grid=(n // tb,),
        in_specs=in_specs,
        out_specs=[row_spec(W_A), st_spec],
        out_shape=[jax.ShapeDtypeStruct((n, W_A), F32), jax.ShapeDtypeStruct(state_all.shape, F32)],
        input_output_aliases={len(args) - 1: 1} if chained else {},
        compiler_params=pltpu.CompilerParams(dimension_semantics=("arbitrary",),
                                             vmem_limit_bytes=VMEM_LIMIT),
        name="delta_step",
    )(*args)


def _pick_tile(seq):
    for tm in (512, 256, 128):
        if seq % tm == 0:
            return tm
    raise ValueError("prompt length must be a multiple of 128")


def kernel(x_prompt, x_sample, state_delta, state_conv, norm_mix_g, w_in, conv_w, A_log, dt_bias, o_norm_g,
           v_norm_g, w_s, b_s, w_o, norm_ffn_g, w_up, w_down, norm_f_g):
    batch, seq, d = x_prompt.shape
    n_dec, dec_seq, _ = x_sample.shape
    depth = w_in.shape[0]
    assert dec_seq == 1 and seq % GM_CHUNK == 0 and n_dec % SUBLANES == 0
    tm = _pick_tile(seq)
    tb = 2 * SUBLANES if n_dec % (2 * SUBLANES) == 0 else SUBLANES
    nb = next(n for n in (4, 2, 1) if batch % n == 0)
    dtm = 256 if seq % 256 == 0 else GM_CHUNK

    off_a = QKV + W_A
    off_u = off_a + 2 * H_A
    off_v = off_u + W_B
    lane_pad = lambda v: jnp.pad(v.astype(F32), ((0, 0), (0, AB_W - v.shape[1]))).reshape(depth, 1, AB_W)
    prm = {
        "win": jnp.concatenate(
            [w_in[:, :, :off_a], w_in[:, :, off_u:off_v], w_in[:, :, off_v:], w_in[:, :, off_a:off_u],
             jnp.zeros((depth, d, AB_W - 2 * H_A), w_in.dtype)], axis=-1).astype(BF16),
        "wo": w_o.astype(BF16),
        "wup": w_up.astype(BF16),
        "wdn": w_down.astype(BF16),
        "gmix": norm_mix_g.reshape(depth, 1, d),
        "gffn": norm_ffn_g.reshape(depth, 1, d),
        "gl": norm_f_g.reshape(1, d),
        "cw": conv_w,
        "alog": lane_pad(A_log),
        "dtb": lane_pad(dt_bias),
        "vg": v_norm_g.reshape(depth, 1, W_B),
        "og": o_norm_g.reshape(depth, 1, DV),
        "ws": w_s,
        "bsb": jnp.broadcast_to(b_s[:, :, :, None], b_s.shape + (DH_B,)).astype(F32),
        "ws0": jnp.repeat(w_s[:, :, 0, 0], DH_B, axis=-1).reshape(depth, 1, W_B),
        "bs0": jnp.repeat(b_s[:, :, 0], DH_B, axis=-1).reshape(depth, 1, W_B),
    }
    conv_flat = state_conv.reshape(depth, n_dec, (CONV_W - 1) * QKV)

    xp = x_prompt.reshape(batch * seq, d)
    xs = x_sample.reshape(n_dec, d)
    dp, cp, cs, vs = [], [], [], []
    new_state = None
    for l in range(depth):
        last = l == depth - 1
        qkv, gate, gb, ob, ctail = _prompt_in(xp, l, prm, batch=batch, seq=seq, tm=tm)
        oa, s_fin = _delta_prompt(qkv.reshape(batch, seq, QKV), gb.reshape(batch, seq, AB_W),
                                  gate.reshape(batch, seq, W_A), l, prm, nb=nb, tm=dtm)
        xp = _out_ffn(xp, oa.reshape(batch * seq, W_A), ob, l, prm, tm=tm, final_norm=last)
        dp.append(s_fin)
        cp.append(ctail)

        qkv_s, gate_s, gb_s, ob_s, vb_s, cnew = _sample_in(xs, conv_flat, l, prm)
        oa_s, new_state = _delta_step(qkv_s, gb_s, gate_s, state_delta, new_state, l, prm, tb=tb)
        xs = _out_ffn(xs, oa_s, ob_s, l, prm, tm=n_dec, final_norm=last)
        cs.append(cnew.reshape(n_dec, CONV_W - 1, QKV))
        vs.append(vb_s.reshape(n_dec, 1, W_B))

    return (xp.reshape(batch, seq, d), xs.reshape(n_dec, 1, d), jnp.stack(dp), jnp.stack(cp),
            new_state, jnp.stack(cs), jnp.stack(vs))
```

```python
import functools

import jax
import jax.numpy as jnp
from jax import lax
from jax.experimental import pallas as pl
from jax.experimental.pallas import tpu as pltpu

F32 = jnp.float32
BF16 = jnp.bfloat16

H_A = 4
DK = 128
DV = 128
W_A = H_A * DV
QKV = 2 * H_A * DK + H_A * DV
CONV_W = 4
DN_CHUNK = 64
H_B = 4
DH_B = 128
W_B = H_B * DH_B
GM_CHUNK = 128
EPS = 1e-6

LANES = 128
SUBLANES = 8
AB_W = LANES
OFF_GATE = QKV
OFF_U = OFF_GATE + W_A
OFF_V = OFF_U + W_B
OFF_AB = OFF_V + W_B
PROJ_PAD = OFF_AB + AB_W
CONV_BLK = H_A * DK

VMEM_LIMIT = 56 * 1024 * 1024
FF_BLOCK = 1024


def _rmsnorm(x, g):
    return x * lax.rsqrt(jnp.mean(x * x, axis=-1, keepdims=True) + EPS) * g


def _silu(x):
    return x * jax.nn.sigmoid(x)


def _softplus(x):
    return jnp.maximum(x, 0.0) + jnp.log1p(jnp.exp(-jnp.abs(x)))


def _layer_spec(arr, layer):
    nd = arr.ndim - 1
    return pl.BlockSpec((None,) + arr.shape[1:], lambda *_: (layer,) + (0,) * nd,
                        pipeline_mode=pl.Buffered(1))


def _qkv_activation(y, j):
    a = _silu(y)
    if j >= 2:
        return a
    blocks = []
    for hh in range(H_A):
        blk = a[:, hh * DK:(hh + 1) * DK]
        blk = blk * lax.rsqrt(jnp.sum(blk * blk, axis=-1, keepdims=True) + EPS)
        if j == 0:
            blk = blk * (DK ** -0.5)
        blocks.append(blk)
    return jnp.concatenate(blocks, axis=1)


def _decay_beta(ab, alog, dtb):
    lane = lax.broadcasted_iota(jnp.int32, ab.shape, 1)
    gdec = -jnp.exp(alog) * _softplus(ab + dtb)
    beta = jax.nn.sigmoid(ab)
    return jnp.where(lane < H_A, gdec, beta)


def _prompt_in_kernel(x_ref, g_ref, win_ref, cw_ref, alog_ref, dtb_ref, vg_ref, ws_ref, bsb_ref,
                      qkv_ref, gate_ref, gb_ref, ob_ref, ctail_ref, carry_ref, *, tm, nt):
    i = pl.program_id(1)
    h = _rmsnorm(x_ref[...], g_ref[...]).astype(BF16)
    proj = lambda lo, hi: jnp.dot(h, win_ref[:, lo:hi], preferred_element_type=F32)

    @pl.when(i == 0)
    def _():
        carry_ref[...] = jnp.zeros(carry_ref.shape, F32)

    sub = lax.broadcasted_iota(jnp.int32, (SUBLANES, CONV_BLK), 0)

    def conv_act(pj, j):
        cols = slice(j * CONV_BLK, (j + 1) * CONV_BLK)
        cw = cw_ref[:, cols]
        prev = carry_ref[:, cols]
        y = pj * cw[CONV_W - 1:CONV_W, :]
        for s in range(1, CONV_W):
            sh = pltpu.roll(pj, s, axis=0)
            head = jnp.where(sub < s, pltpu.roll(prev, s, axis=0), sh[0:SUBLANES])
            sh = jnp.concatenate([head, sh[SUBLANES:]], axis=0)
            y = y + sh * cw[CONV_W - 1 - s:CONV_W - s, :]
        carry_ref[:, cols] = pj[tm - SUBLANES:tm, :]
        ctail_ref[0, :, cols] = pj[tm - (CONV_W - 1):tm, :]
        qkv_ref[:, cols] = _qkv_activation(y, j)

    def gmlp_mix(pv):
        vb = _rmsnorm(pv, vg_ref[...]).astype(BF16)
        nc = tm // GM_CHUNK
        ri = lax.broadcasted_iota(jnp.int32, (GM_CHUNK, GM_CHUNK), 0)
        ci = lax.broadcasted_iota(jnp.int32, (GM_CHUNK, GM_CHUNK), 1)
        mixed = []
        for hh in range(H_B):
            wm = jnp.where(ri >= ci, ws_ref[hh], 0.0).astype(BF16)
            cols = slice(hh * DH_B, (hh + 1) * DH_B)
            rhs = jnp.concatenate([vb[c * GM_CHUNK:(c + 1) * GM_CHUNK, cols] for c in range(nc)], axis=1)
            mixed.append(jnp.dot(wm, rhs, preferred_element_type=F32))
        return mixed

    def gmlp_gate(pu, mixed):
        for hh in range(H_B):
            cols = slice(hh * DH_B, (hh + 1) * DH_B)
            bias = bsb_ref[hh]
            for c in range(tm // GM_CHUNK):
                rows = slice(c * GM_CHUNK, (c + 1) * GM_CHUNK)
                ob_ref[rows, cols] = pu[rows, cols] * (mixed[hh][:, c * DH_B:(c + 1) * DH_B] + bias)

    p_q = proj(0, CONV_BLK)
    p_k = proj(CONV_BLK, 2 * CONV_BLK)
    conv_act(p_q, 0)
    p_v = proj(2 * CONV_BLK, 3 * CONV_BLK)
    conv_act(p_k, 1)
    p_vv = proj(OFF_V, OFF_AB)
    conv_act(p_v, 2)
    p_u = proj(OFF_U, OFF_V)
    mixed = gmlp_mix(p_vv)
    p_ab = proj(OFF_AB, PROJ_PAD)
    gmlp_gate(p_u, mixed)
    p_gate = proj(OFF_GATE, OFF_U)
    gb_ref[...] = _decay_beta(p_ab, alog_ref[...], dtb_ref[...])
    gate_ref[...] = p_gate


def _prompt_in(x2d, layer, prm, *, batch, seq, tm):
    nt = seq // tm
    rows = batch * seq
    d = x2d.shape[1]
    row_spec = lambda w: pl.BlockSpec((tm, w), lambda b, i: (b * nt + i, 0))
    names = ("gmix", "win", "cw", "alog", "dtb", "vg", "ws", "bsb")
    return pl.pallas_call(
        functools.partial(_prompt_in_kernel, tm=tm, nt=nt),
        grid=(batch, nt),
        in_specs=[row_spec(d)] + [_layer_spec(prm[n], layer) for n in names],
        out_specs=[row_spec(QKV), row_spec(W_A), row_spec(AB_W), row_spec(W_B),
                   pl.BlockSpec((1, CONV_W - 1, QKV), lambda b, i: (b, 0, 0))],
        out_shape=[jax.ShapeDtypeStruct((rows, QKV), F32), jax.ShapeDtypeStruct((rows, W_A), F32),
                   jax.ShapeDtypeStruct((rows, AB_W), F32), jax.ShapeDtypeStruct((rows, W_B), F32),
                   jax.ShapeDtypeStruct((batch, CONV_W - 1, QKV), F32)],
        scratch_shapes=[pltpu.VMEM((SUBLANES, QKV), F32)],
        compiler_params=pltpu.CompilerParams(dimension_semantics=("arbitrary", "arbitrary"),
                                             vmem_limit_bytes=VMEM_LIMIT),
        name="prompt_in",
    )(x2d, *[prm[n] for n in names])


def _unit_lower_inverses(lmats, ri, ci):
    c = lmats[0].shape[0]
    eye = (ri == ci).astype(F32)
    xs = None
    s = 1
    while s < c:
        mask = ((ri // (2 * s)) == (ci // (2 * s))) & (((ri // s) % 2) == 1) & (((ci // s) % 2) == 0)
        css = [jnp.where(mask, lm, 0.0) for lm in lmats]
        if xs is None:
            xs = [eye - cs for cs in css]
        else:
            xbs = [x.astype(BF16) for x in xs]
            ys = [jnp.dot(xb, cs.astype(BF16), preferred_element_type=F32) for xb, cs in zip(xbs, css)]
            zs = [jnp.dot(y.astype(BF16), xb, preferred_element_type=F32) for y, xb in zip(ys, xbs)]
            xs = [x - z for x, z in zip(xs, zs)]
        s *= 2
    return xs


def _nt_dot(a, b):
    return lax.dot_general(a.astype(BF16), b.astype(BF16), (((1,), (1,)), ((), ())),
                           preferred_element_type=F32)


def _tn_dot(a, b):
    return lax.dot_general(a.astype(BF16), b.astype(BF16), (((0,), (0,)), ((), ())),
                           preferred_element_type=F32)


def _dot(a, b):
    return jnp.dot(a.astype(BF16), b.astype(BF16), preferred_element_type=F32)


def _delta_kernel(qkv_ref, gb_ref, gate_ref, og_ref, oa_ref, sout_ref,
                  s_ref, gcs_ref, u_ref, w_ref, qk_ref, kd_ref, *, nb, tm, nt):
    i = pl.program_id(1)
    c = DN_CHUNK

    @pl.when(i == 0)
    def _():
        s_ref[...] = jnp.zeros(s_ref.shape, F32)

    ri = lax.broadcasted_iota(jnp.int32, (c, c), 0)
    ci = lax.broadcasted_iota(jnp.int32, (c, c), 1)
    lower = ri >= ci
    strict = ri > ci
    rowid = lax.broadcasted_iota(jnp.int32, (c, LANES), 0)
    og = og_ref[...]
    chains = [(b, hh) for b in range(nb) for hh in range(H_A)]
    qcols = lambda hh: slice(hh * DK, (hh + 1) * DK)
    kcols = lambda hh: slice(H_A * DK + hh * DK, H_A * DK + (hh + 1) * DK)
    vcols = lambda hh: slice(2 * H_A * DK + hh * DV, 2 * H_A * DK + (hh + 1) * DV)

    def prepare(n, carry):
        rows = pl.ds(pl.multiple_of(n * c, c), c)
        gbcs, gcss, gcsts = [], [], []
        for b in range(nb):
            gbc = gb_ref[b, rows, :]
            gcs = gbc
            sh = 1
            while sh < c:
                gcs = gcs + jnp.where(rowid >= sh, pltpu.roll(gcs, sh, axis=0), 0.0)
                sh *= 2
            gcs_ref[b, rows, :] = gcs
            gbcs.append(gbc)
            gcss.append(gcs)
            gcsts.append(gcs.T)
        ks, kbs, betas, g_cols, decays = [], [], [], [], []
        for b, hh in chains:
            k = qkv_ref[b, rows, kcols(hh)]
            g_col = jnp.broadcast_to(gcss[b][:, hh:hh + 1], (c, LANES))
            beta = jnp.broadcast_to(gbcs[b][:, H_A + hh:H_A + hh + 1], (c, LANES))
            g_row = jnp.broadcast_to(gcsts[b][hh:hh + 1, :], (c, c))
            decays.append(jnp.where(lower, jnp.exp(jnp.where(lower, g_col[:, :c] - g_row, 0.0)), 0.0))
            ks.append(k)
            kbs.append(k * beta)
            betas.append(beta)
            g_cols.append(g_col)
        kqs = [_nt_dot(jnp.concatenate([kb, qkv_ref[b, rows, qcols(hh)]], axis=0), k)
               for (b, hh), kb, k in zip(chains, kbs, ks)]
        lmats = [jnp.where(strict, kq[:c] * decay, 0.0) for kq, decay in zip(kqs, decays)]
        ts = _unit_lower_inverses(lmats, ri, ci)
        for j, (b, hh) in enumerate(chains):
            rhs = jnp.concatenate([qkv_ref[b, rows, vcols(hh)] * betas[j], kbs[j] * jnp.exp(g_cols[j])], axis=1)
            uw = _dot(ts[j], rhs)
            u_ref[b, rows, qcols(hh)] = uw[:, :DV]
            w_ref[b, rows, qcols(hh)] = uw[:, DV:].astype(BF16)
        for j, (b, hh) in enumerate(chains):
            qk_ref[b, rows, hh * LANES:hh * LANES + c] = (kqs[j][c:] * decays[j]).astype(BF16)
            kd_ref[b, rows, qcols(hh)] = (ks[j] * jnp.exp(g_cols[j][c - 1:c, :] - g_cols[j])).astype(BF16)
        return carry

    def recur(n, carry):
        rows = pl.ds(pl.multiple_of(n * c, c), c)
        gcss = [gcs_ref[b, rows, :] for b in range(nb)]
        s_olds = [s_ref[b, hh] for b, hh in chains]
        s_bfs = [s.astype(BF16) for s in s_olds]
        wss = [jnp.dot(w_ref[b, rows, qcols(hh)], s_bf, preferred_element_type=F32)
               for (b, hh), s_bf in zip(chains, s_bfs)]
        v_bfs = [(u_ref[b, rows, qcols(hh)] - ws).astype(BF16) for (b, hh), ws in zip(chains, wss)]
        upds = [_tn_dot(kd_ref[b, rows, qcols(hh)], v_bf) for (b, hh), v_bf in zip(chains, v_bfs)]
        g_cols = [jnp.broadcast_to(gcss[b][:, hh:hh + 1], (c, LANES)) for b, hh in chains]
        for j, (b, hh) in enumerate(chains):
            s_ref[b, hh] = s_olds[j] * jnp.exp(g_cols[j][c - 1:c, :]) + upds[j]
        qss = [jnp.dot(qkv_ref[b, rows, qcols(hh)].astype(BF16), s_bf, preferred_element_type=F32)
               for (b, hh), s_bf in zip(chains, s_bfs)]
        qkvs = [jnp.dot(qk_ref[b, rows, hh * LANES:hh * LANES + c], v_bf, preferred_element_type=F32)
                for (b, hh), v_bf in zip(chains, v_bfs)]
        for j, (b, hh) in enumerate(chains):
            o = qss[j] * jnp.exp(g_cols[j]) + qkvs[j]
            oa_ref[b, rows, qcols(hh)] = _rmsnorm(o, og) * _silu(gate_ref[b, rows, qcols(hh)])
        return carry

    lax.fori_loop(0, tm // c, prepare, 0)
    lax.fori_loop(0, tm // c, recur, 0)

    @pl.when(i == nt - 1)
    def _():
        sout_ref[...] = s_ref[...]


def _delta_prompt(qkv, gb, gate, layer, prm, *, nb, tm):
    batch, seq, _ = qkv.shape
    nt = seq // tm
    blk = lambda w: pl.BlockSpec((nb, tm, w), lambda b, i: (b, i, 0))
    return pl.pallas_call(
        functools.partial(_delta_kernel, nb=nb, tm=tm, nt=nt),
        grid=(batch // nb, nt),
        in_specs=[blk(QKV), blk(AB_W), blk(W_A), _layer_spec(prm["og"], layer)],
        out_specs=[blk(W_A), pl.BlockSpec((nb, H_A, DK, DV), lambda b, i: (b, 0, 0, 0))],
        out_shape=[jax.ShapeDtypeStruct((batch, seq, W_A), F32),
                   jax.ShapeDtypeStruct((batch, H_A, DK, DV), F32)],
        scratch_shapes=[pltpu.VMEM((nb, H_A, DK, DV), F32),
                        pltpu.VMEM((nb, tm, AB_W), F32),
                        pltpu.VMEM((nb, tm, W_A), F32),
                        pltpu.VMEM((nb, tm, W_A), BF16),
                        pltpu.VMEM((nb, tm, H_A * LANES), BF16),
                        pltpu.VMEM((nb, tm, W_A), BF16)],
        compiler_params=pltpu.CompilerParams(dimension_semantics=("arbitrary", "arbitrary"),
                                             vmem_limit_bytes=VMEM_LIMIT),
        name="delta_prompt",
    )(qkv, gb, gate, prm["og"])


def _out_ffn_kernel(x_ref, oa_ref, ob_ref, wo_ref, gf_ref, wup_ref, wdn_ref, gl_ref, out_ref, *, final_norm):
    y = jnp.dot(oa_ref[...].astype(BF16), wo_ref[0:W_A, :], preferred_element_type=F32)
    y = y + jnp.dot(ob_ref[...].astype(BF16), wo_ref[W_A:W_A + W_B, :], preferred_element_type=F32)
    x1 = x_ref[...] + y
    h = _rmsnorm(x1, gf_ref[...]).astype(BF16)
    d_ff = wup_ref.shape[1]
    ffn = None
    for j in range(d_ff // FF_BLOCK):
        cols = slice(j * FF_BLOCK, (j + 1) * FF_BLOCK)
        a = jnp.dot(h, wup_ref[:, cols], preferred_element_type=F32)
        a = jnp.square(jnp.maximum(a, 0.0)).astype(BF16)
        part = jnp.dot(a, wdn_ref[cols, :], preferred_element_type=F32)
        ffn = part if ffn is None else ffn + part
    x2 = x1 + ffn
    if final_norm:
        x2 = _rmsnorm(x2, gl_ref[...])
    out_ref[...] = x2


def _out_ffn(x2d, oa, ob, layer, prm, *, tm, final_norm):
    rows, d = x2d.shape
    row_spec = lambda w: pl.BlockSpec((tm, w), lambda i: (i, 0))
    names = ("wo", "gffn", "wup", "wdn")
    return pl.pallas_call(
        functools.partial(_out_ffn_kernel, final_norm=final_norm),
        grid=(rows // tm,),
        in_specs=[row_spec(d), row_spec(W_A), row_spec(W_B)] + [_layer_spec(prm[n], layer) for n in names]
                 + [pl.BlockSpec((1, d), lambda i: (0, 0))],
        out_specs=row_spec(d),
        out_shape=jax.ShapeDtypeStruct((rows, d), F32),
        compiler_params=pltpu.CompilerParams(dimension_semantics=("arbitrary",),
                                             vmem_limit_bytes=VMEM_LIMIT),
        name="out_ffn",
    )(x2d, oa, ob, *[prm[n] for n in names], prm["gl"])


def _sample_in_kernel(x_ref, g_ref, win_ref, cw_ref, alog_ref, dtb_ref, vg_ref, ws0_ref, bs0_ref,
                      c0_ref, c1_ref, c2_ref, qkv_ref, gate_ref, gb_ref, ob_ref, vb_ref, cnew_ref):
    h = _rmsnorm(x_ref[...], g_ref[...]).astype(BF16)
    p = jnp.dot(h, win_ref[...], preferred_element_type=F32)
    pq = p[:, :QKV]
    cw = cw_ref[...]
    c1 = c1_ref[...]
    c2 = c2_ref[...]
    y = c0_ref[...] * cw[0:1, :] + c1 * cw[1:2, :] + c2 * cw[2:3, :] + pq * cw[3:4, :]
    cnew_ref[:, 0:QKV] = c1
    cnew_ref[:, QKV:2 * QKV] = c2
    cnew_ref[:, 2 * QKV:3 * QKV] = pq
    for j in range(QKV // CONV_BLK):
        cols = slice(j * CONV_BLK, (j + 1) * CONV_BLK)
        qkv_ref[:, cols] = _qkv_activation(y[:, cols], j)
    gate_ref[...] = p[:, OFF_GATE:OFF_U]
    gb_ref[...] = _decay_beta(p[:, OFF_AB:PROJ_PAD], alog_ref[...], dtb_ref[...])
    vb = _rmsnorm(p[:, OFF_V:OFF_AB], vg_ref[...])
    vb_ref[...] = vb
    ob_ref[...] = p[:, OFF_U:OFF_V] * (vb * ws0_ref[...] + bs0_ref[...])


def _sample_in(x2d, conv_flat, layer, prm):
    n, d = x2d.shape
    full = lambda w: pl.BlockSpec((n, w), lambda i: (0, 0))
    conv_col = lambda j: pl.BlockSpec((None, n, QKV), lambda i: (layer, 0, j))
    names = ("gmix", "win", "cw", "alog", "dtb", "vg", "ws0", "bs0")
    return pl.pallas_call(
        _sample_in_kernel,
        grid=(1,),
        in_specs=[full(d)] + [_layer_spec(prm[k], layer) for k in names] + [conv_col(j) for j in range(CONV_W - 1)],
        out_specs=[full(QKV), full(W_A), full(AB_W), full(W_B), full(W_B), full((CONV_W - 1) * QKV)],
        out_shape=[jax.ShapeDtypeStruct((n, QKV), F32), jax.ShapeDtypeStruct((n, W_A), F32),
                   jax.ShapeDtypeStruct((n, AB_W), F32), jax.ShapeDtypeStruct((n, W_B), F32),
                   jax.ShapeDtypeStruct((n, W_B), F32),
                   jax.ShapeDtypeStruct((n, (CONV_W - 1) * QKV), F32)],
        compiler_params=pltpu.CompilerParams(dimension_semantics=("arbitrary",),
                                             vmem_limit_bytes=VMEM_LIMIT),
        name="sample_in",
    )(x2d, *[prm[k] for k in names], conv_flat, conv_flat, conv_flat)


def _delta_step_kernel(*refs, tb, chained):
    if chained:
        qkv_ref, gb_ref, gate_ref, og_ref, s_ref, _, oa_ref, snew_ref = refs
    else:
        qkv_ref, gb_ref, gate_ref, og_ref, s_ref, oa_ref, snew_ref = refs
    og = og_ref[...]
    gb = gb_ref[...]
    for hh in range(H_A):
        q = qkv_ref[:, hh * DK:(hh + 1) * DK]
        k = qkv_ref[:, H_A * DK + hh * DK:H_A * DK + (hh + 1) * DK]
        v = qkv_ref[:, 2 * H_A * DK + hh * DV:2 * H_A * DK + (hh + 1) * DV]
        q_t = q.T
        k_t = k.T
        decay = jnp.exp(gb[:, hh:hh + 1])
        beta = gb[:, H_A + hh:H_A + hh + 1]
        o_rows = []
        for t in range(tb):
            s = s_ref[t, hh] * decay[t:t + 1, :]
            k_col = jnp.broadcast_to(k_t[:, t:t + 1], (DK, DV))
            q_col = jnp.broadcast_to(q_t[:, t:t + 1], (DK, DV))
            kv = jnp.sum(s * k_col, axis=0, keepdims=True)
            delta = (v[t:t + 1, :] - kv) * beta[t:t + 1, :]
            s = s + k_col * delta
            snew_ref[t, hh] = s
            o_rows.append(jnp.sum(s * q_col, axis=0, keepdims=True))
        o = jnp.concatenate(o_rows, axis=0)
        gate = gate_ref[:, hh * DV:(hh + 1) * DV]
        oa_ref[:, hh * DV:(hh + 1) * DV] = _rmsnorm(o, og) * _silu(gate)


def _delta_step(qkv, gb, gate, state_all, new_all, layer, prm, *, tb):
    n = qkv.shape[0]
    row_spec = lambda w: pl.BlockSpec((tb, w), lambda i: (i, 0))
    st_spec = pl.BlockSpec((None, tb, H_A, DK, DV), lambda i: (layer, i, 0, 0, 0))
    chained = new_all is not None
    in_specs = [row_spec(QKV), row_spec(AB_W), row_spec(W_A), _layer_spec(prm["og"], layer), st_spec]
    args = [qkv, gb, gate, prm["og"], state_all]
    if chained:
        in_specs.append(pl.BlockSpec(memory_space=pl.ANY))
        args.append(new_all)
    return pl.pallas_call(
        functools.partial(_delta_step_kernel, tb=tb, chained=chained),
        grid=(n // tb,),
        in_specs=in_specs,
        out_specs=[row_spec(W_A), st_spec],
        out_shape=[jax.ShapeDtypeStruct((n, W_A), F32), jax.ShapeDtypeStruct(state_all.shape, F32)],
        input_output_aliases={len(args) - 1: 1} if chained else {},
        compiler_params=pltpu.CompilerParams(dimension_semantics=("arbitrary",),
                                             vmem_limit_bytes=VMEM_LIMIT),
        name="delta_step",
    )(*args)


def _pick_tile(seq):
    for tm in (512, 256, 128):
        if seq % tm == 0:
            return tm
    raise ValueError("prompt length must be a multiple of 128")


def kernel(x_prompt, x_sample, state_delta, state_conv, norm_mix_g, w_in, conv_w, A_log, dt_bias, o_norm_g,
           v_norm_g, w_s, b_s, w_o, norm_ffn_g, w_up, w_down, norm_f_g):
    batch, seq, d = x_prompt.shape
    n_dec, dec_seq, _ = x_sample.shape
    depth = w_in.shape[0]
    assert dec_seq == 1 and seq % GM_CHUNK == 0 and n_dec % SUBLANES == 0
    tm = _pick_tile(seq)
    tb = 2 * SUBLANES if n_dec % (2 * SUBLANES) == 0 else SUBLANES
    nb = next(n for n in (8, 4, 2, 1) if batch % n == 0)
    dtm = GM_CHUNK

    off_a = QKV + W_A
    off_u = off_a + 2 * H_A
    off_v = off_u + W_B
    lane_pad = lambda v: jnp.pad(v.astype(F32), ((0, 0), (0, AB_W - v.shape[1]))).reshape(depth, 1, AB_W)
    prm = {
        "win": jnp.concatenate(
            [w_in[:, :, :off_a], w_in[:, :, off_u:off_v], w_in[:, :, off_v:], w_in[:, :, off_a:off_u],
             jnp.zeros((depth, d, AB_W - 2 * H_A), w_in.dtype)], axis=-1).astype(BF16),
        "wo": w_o.astype(BF16),
        "wup": w_up.astype(BF16),
        "wdn": w_down.astype(BF16),
        "gmix": norm_mix_g.reshape(depth, 1, d),
        "gffn": norm_ffn_g.reshape(depth, 1, d),
        "gl": norm_f_g.reshape(1, d),
        "cw": conv_w,
        "alog": lane_pad(A_log),
        "dtb": lane_pad(dt_bias),
        "vg": v_norm_g.reshape(depth, 1, W_B),
        "og": o_norm_g.reshape(depth, 1, DV),
        "ws": w_s,
        "bsb": jnp.broadcast_to(b_s[:, :, :, None], b_s.shape + (DH_B,)).astype(F32),
        "ws0": jnp.repeat(w_s[:, :, 0, 0], DH_B, axis=-1).reshape(depth, 1, W_B),
        "bs0": jnp.repeat(b_s[:, :, 0], DH_B, axis=-1).reshape(depth, 1, W_B),
    }
    conv_flat = state_conv.reshape(depth, n_dec, (CONV_W - 1) * QKV)

    xp = x_prompt.reshape(batch * seq, d)
    xs = x_sample.reshape(n_dec, d)
    dp, cp, cs, vs = [], [], [], []
    new_state = None
    for l in range(depth):
        last = l == depth - 1
        qkv, gate, gb, ob, ctail = _prompt_in(xp, l, prm, batch=batch, seq=seq, tm=tm)
        oa, s_fin = _delta_prompt(qkv.reshape(batch, seq, QKV), gb.reshape(batch, seq, AB_W),
                                  gate.reshape(batch, seq, W_A), l, prm, nb=nb, tm=dtm)
        xp = _out_ffn(xp, oa.reshape(batch * seq, W_A), ob, l, prm, tm=tm, final_norm=last)
        dp.append(s_fin)
        cp.append(ctail)

        qkv_s, gate_s, gb_s, ob_s, vb_s, cnew = _sample_in(xs, conv_flat, l, prm)
        oa_s, new_state = _delta_step(qkv_s, gb_s, gate_s, state_delta, new_state, l, prm, tb=tb)
        xs = _out_ffn(xs, oa_s, ob_s, l, prm, tm=n_dec, final_norm=last)
        cs.append(cnew.reshape(n_dec, CONV_W - 1, QKV))
        vs.append(vb_s.reshape(n_dec, 1, W_B))

    return (xp.reshape(batch, seq, d), xs.reshape(n_dec, 1, d), jnp.stack(dp), jnp.stack(cp),
            new_state, jnp.stack(cs), jnp.stack(vs))
```

```python
import functools

import jax
import jax.numpy as jnp
from jax import lax
from jax.experimental import pallas as pl
from jax.experimental.pallas import tpu as pltpu

F32 = jnp.float32
BF16 = jnp.bfloat16

H_A = 4
DK = 128
DV = 128
W_A = H_A * DV
QKV = 2 * H_A * DK + H_A * DV
CONV_W = 4
DN_CHUNK = 64
H_B = 4
DH_B = 128
W_B = H_B * DH_B
GM_CHUNK = 128
EPS = 1e-6

LANES = 128
SUBLANES = 8
AB_W = LANES
OFF_GATE = QKV
OFF_U = OFF_GATE + W_A
OFF_V = OFF_U + W_B
OFF_AB = OFF_V + W_B
PROJ_PAD = OFF_AB + AB_W
CONV_BLK = H_A * DK

VMEM_LIMIT = 56 * 1024 * 1024
FF_BLOCK = 1024
PROJ_ROWS = 512
PREP_CHUNKS = 2


def _rmsnorm(x, g):
    return x * lax.rsqrt(jnp.mean(x * x, axis=-1, keepdims=True) + EPS) * g


def _silu(x):
    return x * jax.nn.sigmoid(x)


def _softplus(x):
    return jnp.maximum(x, 0.0) + jnp.log1p(jnp.exp(-jnp.abs(x)))


def _layer_spec(arr, layer):
    nd = arr.ndim - 1
    return pl.BlockSpec((None,) + arr.shape[1:], lambda *_: (layer,) + (0,) * nd,
                        pipeline_mode=pl.Buffered(1))


def _qkv_activation(y, j):
    a = _silu(y)
    if j >= 2:
        return a
    blocks = []
    for hh in range(H_A):
        blk = a[:, hh * DK:(hh + 1) * DK]
        blk = blk * lax.rsqrt(jnp.sum(blk * blk, axis=-1, keepdims=True) + EPS)
        if j == 0:
            blk = blk * (DK ** -0.5)
        blocks.append(blk)
    return jnp.concatenate(blocks, axis=1)


def _decay_beta(ab, alog, dtb):
    lane = lax.broadcasted_iota(jnp.int32, ab.shape, 1)
    gdec = -jnp.exp(alog) * _softplus(ab + dtb)
    beta = jax.nn.sigmoid(ab)
    return jnp.where(lane < H_A, gdec, beta)


def _prompt_in_kernel(x_ref, g_ref, win_ref, cw_ref, alog_ref, dtb_ref, vg_ref, ws_ref, bsb_ref,
                      qkv_ref, gate_ref, gb_ref, ob_ref, ctail_ref, carry_ref, *, tm, nt):
    i = pl.program_id(1)
    sm = PROJ_ROWS
    n_sub = tm // sm

    @pl.when(i == 0)
    def _():
        carry_ref[...] = jnp.zeros(carry_ref.shape, F32)

    sub = lax.broadcasted_iota(jnp.int32, (SUBLANES, CONV_BLK), 0)
    ri = lax.broadcasted_iota(jnp.int32, (GM_CHUNK, GM_CHUNK), 0)
    ci = lax.broadcasted_iota(jnp.int32, (GM_CHUNK, GM_CHUNK), 1)
    nc = sm // GM_CHUNK
    hs, tails, mixes = {}, {}, {}

    def proj(s, lo, hi):
        if s not in hs:
            hs[s] = _rmsnorm(x_ref[s * sm:(s + 1) * sm, :], g_ref[...]).astype(BF16)
        return jnp.dot(hs[s], win_ref[:, lo:hi], preferred_element_type=F32)

    def conv_act(s, j, pj):
        rows = slice(s * sm, (s + 1) * sm)
        cols = slice(j * CONV_BLK, (j + 1) * CONV_BLK)
        cw = cw_ref[:, cols]
        prev = carry_ref[:, cols] if s == 0 else tails[s - 1, j]
        y = pj * cw[CONV_W - 1:CONV_W, :]
        for k in range(1, CONV_W):
            sh = pltpu.roll(pj, k, axis=0)
            head = jnp.where(sub < k, pltpu.roll(prev, k, axis=0), sh[0:SUBLANES])
            sh = jnp.concatenate([head, sh[SUBLANES:]], axis=0)
            y = y + sh * cw[CONV_W - 1 - k:CONV_W - k, :]
        tails[s, j] = pj[sm - SUBLANES:sm, :]
        if s == n_sub - 1:
            carry_ref[:, cols] = tails[s, j]
            ctail_ref[0, :, cols] = pj[sm - (CONV_W - 1):sm, :]
        qkv_ref[rows, cols] = _qkv_activation(y, j)

    def gmlp_mix(s, pv):
        vb = _rmsnorm(pv, vg_ref[...]).astype(BF16)
        mixes[s] = []
        for hh in range(H_B):
            wm = jnp.where(ri >= ci, ws_ref[hh], 0.0).astype(BF16)
            cols = slice(hh * DH_B, (hh + 1) * DH_B)
            rhs = jnp.concatenate([vb[c * GM_CHUNK:(c + 1) * GM_CHUNK, cols] for c in range(nc)], axis=1)
            mixes[s].append(jnp.dot(wm, rhs, preferred_element_type=F32))

    def gmlp_gate(s, pu):
        for hh in range(H_B):
            cols = slice(hh * DH_B, (hh + 1) * DH_B)
            bias = bsb_ref[hh]
            for c in range(nc):
                src = slice(c * GM_CHUNK, (c + 1) * GM_CHUNK)
                dst = slice(s * sm + c * GM_CHUNK, s * sm + (c + 1) * GM_CHUNK)
                ob_ref[dst, cols] = pu[src, cols] * (mixes[s][hh][:, c * DH_B:(c + 1) * DH_B] + bias)

    def store_gb(s, pab):
        gb_ref[s * sm:(s + 1) * sm, :] = _decay_beta(pab, alog_ref[...], dtb_ref[...])

    def store_gate(s, pg):
        gate_ref[s * sm:(s + 1) * sm, :] = pg

    stages = []
    for s in range(n_sub):
        for j in range(QKV // CONV_BLK):
            stages.append((s, j * CONV_BLK, (j + 1) * CONV_BLK, functools.partial(conv_act, s, j)))
        stages.append((s, OFF_V, OFF_AB, functools.partial(gmlp_mix, s)))
        stages.append((s, OFF_U, OFF_V, functools.partial(gmlp_gate, s)))
        stages.append((s, OFF_AB, PROJ_PAD, functools.partial(store_gb, s)))
        stages.append((s, OFF_GATE, OFF_U, functools.partial(store_gate, s)))
    pending = proj(*stages[0][:3])
    for n, stage in enumerate(stages):
        ahead = proj(*stages[n + 1][:3]) if n + 1 < len(stages) else None
        stage[3](pending)
        pending = ahead


def _prompt_in(x2d, layer, prm, *, batch, seq, tm):
    nt = seq // tm
    rows = batch * seq
    d = x2d.shape[1]
    row_spec = lambda w: pl.BlockSpec((tm, w), lambda b, i: (b * nt + i, 0))
    names = ("gmix", "win", "cw", "alog", "dtb", "vg", "ws", "bsb")
    return pl.pallas_call(
        functools.partial(_prompt_in_kernel, tm=tm, nt=nt),
        grid=(batch, nt),
        in_specs=[row_spec(d)] + [_layer_spec(prm[n], layer) for n in names],
        out_specs=[row_spec(QKV), row_spec(W_A), row_spec(AB_W), row_spec(W_B),
                   pl.BlockSpec((1, CONV_W - 1, QKV), lambda b, i: (b, 0, 0))],
        out_shape=[jax.ShapeDtypeStruct((rows, QKV), F32), jax.ShapeDtypeStruct((rows, W_A), F32),
                   jax.ShapeDtypeStruct((rows, AB_W), F32), jax.ShapeDtypeStruct((rows, W_B), F32),
                   jax.ShapeDtypeStruct((batch, CONV_W - 1, QKV), F32)],
        scratch_shapes=[pltpu.VMEM((SUBLANES, QKV), F32)],
        compiler_params=pltpu.CompilerParams(dimension_semantics=("arbitrary", "arbitrary"),
                                             vmem_limit_bytes=VMEM_LIMIT),
        name="prompt_in",
    )(x2d, *[prm[n] for n in names])


def _pair_blockdiag(m, left):
    zero = jnp.zeros_like(m)
    return jnp.concatenate([jnp.where(left, m, zero), jnp.where(left, zero, m)], axis=0)


def _unit_lower_inverses(lpairs, ri, cj, left):
    c = lpairs[0].shape[0]
    eye = (ri == cj).astype(F32)
    xs = None
    s = 1
    while s < c:
        mask = ((ri // (2 * s)) == (cj // (2 * s))) & (((ri // s) % 2) == 1) & (((cj // s) % 2) == 0)
        css = [jnp.where(mask, lp, 0.0) for lp in lpairs]
        if xs is None:
            xs = [eye - cs for cs in css]
        else:
            xbs = [x.astype(BF16) for x in xs]
            ys = [jnp.dot(xb, _pair_blockdiag(cs.astype(BF16), left), preferred_element_type=F32)
                  for xb, cs in zip(xbs, css)]
            zs = [jnp.dot(y.astype(BF16), _pair_blockdiag(xb, left), preferred_element_type=F32)
                  for y, xb in zip(ys, xbs)]
            xs = [x - z for x, z in zip(xs, zs)]
        s *= 2
    return xs


def _nt_dot(a, b):
    return lax.dot_general(a.astype(BF16), b.astype(BF16), (((1,), (1,)), ((), ())),
                           preferred_element_type=F32)


def _tn_dot(a, b):
    return lax.dot_general(a.astype(BF16), b.astype(BF16), (((0,), (0,)), ((), ())),
                           preferred_element_type=F32)


def _dot(a, b):
    return jnp.dot(a.astype(BF16), b.astype(BF16), preferred_element_type=F32)


def _delta_kernel(qkv_ref, gb_ref, gate_ref, og_ref, oa_ref, sout_ref,
                  s_ref, gcs_ref, u_ref, w_ref, qk_ref, kd_ref, *, nb, tm, nt):
    i = pl.program_id(1)
    c = DN_CHUNK

    @pl.when(i == 0)
    def _():
        s_ref[...] = jnp.zeros(s_ref.shape, F32)

    assert 2 * c == LANES and H_A % 2 == 0
    ri = lax.broadcasted_iota(jnp.int32, (c, LANES), 0)
    lane = lax.broadcasted_iota(jnp.int32, (c, LANES), 1)
    left = lane < c
    cj = jnp.where(left, lane, lane - c)
    lower = ri >= cj
    strict = ri > cj
    og = og_ref[...]
    chains = [(b, hh) for b in range(nb) for hh in range(H_A)]
    pairs = [(b, p) for b in range(nb) for p in range(H_A // 2)]
    qcols = lambda hh: slice(hh * DK, (hh + 1) * DK)
    kcols = lambda hh: slice(H_A * DK + hh * DK, H_A * DK + (hh + 1) * DK)
    vcols = lambda hh: slice(2 * H_A * DK + hh * DV, 2 * H_A * DK + (hh + 1) * DV)
    pcols = lambda p: slice(p * LANES, (p + 1) * LANES)
    cat = jnp.concatenate

    def prepare(n, carry):
        units = [(b, pl.ds(pl.multiple_of((n * PREP_CHUNKS + e) * c, c), c))
                 for e in range(PREP_CHUNKS) for b in range(nb)]
        chains = [(ui, hh) for ui in range(len(units)) for hh in range(H_A)]
        pairs = [(ui, p) for ui in range(len(units)) for p in range(H_A // 2)]
        gbcs, gcss, gcsts = [], [], []
        for b, rows in units:
            gbc = gb_ref[b, rows, :]
            gcs = gbc
            sh = 1
            while sh < c:
                gcs = gcs + jnp.where(ri >= sh, pltpu.roll(gcs, sh, axis=0), 0.0)
                sh *= 2
            gcs_ref[b, rows, :] = gcs
            gbcs.append(gbc)
            gcss.append(gcs)
            gcsts.append(gcs.T)
        ks, kbs, betas, g_cols = {}, {}, {}, {}
        for ui, hh in chains:
            b, rows = units[ui]
            ks[ui, hh] = qkv_ref[b, rows, kcols(hh)]
            g_cols[ui, hh] = jnp.broadcast_to(gcss[ui][:, hh:hh + 1], (c, LANES))
            betas[ui, hh] = jnp.broadcast_to(gbcs[ui][:, H_A + hh:H_A + hh + 1], (c, LANES))
            kbs[ui, hh] = ks[ui, hh] * betas[ui, hh]
        decays, kqs = [], []
        for ui, p in pairs:
            b, rows = units[ui]
            h0, h1 = 2 * p, 2 * p + 1
            g_col = jnp.where(left, g_cols[ui, h0], g_cols[ui, h1])
            g_row = jnp.broadcast_to(cat([gcsts[ui][h0:h0 + 1, :], gcsts[ui][h1:h1 + 1, :]], axis=1), (c, LANES))
            decays.append(jnp.where(lower, jnp.exp(jnp.where(lower, g_col - g_row, 0.0)), 0.0))
            zero = jnp.zeros((c, DK), F32)
            lhs = cat([cat([kbs[ui, h0], kbs[ui, h1]], axis=1),
                       cat([qkv_ref[b, rows, qcols(h0)], qkv_ref[b, rows, qcols(h1)]], axis=1)], axis=0)
            rhs = cat([cat([ks[ui, h0], zero], axis=1), cat([zero, ks[ui, h1]], axis=1)], axis=0)
            kqs.append(_nt_dot(lhs, rhs))
        lpairs = [jnp.where(strict, kq[:c] * decay, 0.0) for kq, decay in zip(kqs, decays)]
        ts = _unit_lower_inverses(lpairs, ri, cj, left)
        for j, (ui, p) in enumerate(pairs):
            b, rows = units[ui]
            h0, h1 = 2 * p, 2 * p + 1
            zero = jnp.zeros((c, DV), F32)
            vb0 = qkv_ref[b, rows, vcols(h0)] * betas[ui, h0]
            vb1 = qkv_ref[b, rows, vcols(h1)] * betas[ui, h1]
            kg0 = kbs[ui, h0] * jnp.exp(g_cols[ui, h0])
            kg1 = kbs[ui, h1] * jnp.exp(g_cols[ui, h1])
            rhs = cat([cat([vb0, zero, kg0, zero], axis=1), cat([zero, vb1, zero, kg1], axis=1)], axis=0)
            uw = _dot(ts[j], rhs)
            u_ref[b, rows, qcols(h0)] = uw[:, 0:DV]
            u_ref[b, rows, qcols(h1)] = uw[:, DV:2 * DV]
            w_ref[b, rows, qcols(h0)] = uw[:, 2 * DV:3 * DV].astype(BF16)
            w_ref[b, rows, qcols(h1)] = uw[:, 3 * DV:4 * DV].astype(BF16)
            qk_ref[b, rows, pcols(p)] = (kqs[j][c:] * decays[j]).astype(BF16)
        for ui, hh in chains:
            b, rows = units[ui]
            kd_ref[b, rows, qcols(hh)] = (ks[ui, hh] * jnp.exp(g_cols[ui, hh][c - 1:c, :] - g_cols[ui, hh])).astype(BF16)
        return carry

    def recur(n, carry):
        rows = pl.ds(pl.multiple_of(n * c, c), c)
        gcss = [gcs_ref[b, rows, :] for b in range(nb)]
        s_olds = [s_ref[b, hh] for b, hh in chains]
        s_bfs = [s.astype(BF16) for s in s_olds]
        wss = [jnp.dot(w_ref[b, rows, qcols(hh)], s_bf, preferred_element_type=F32)
               for (b, hh), s_bf in zip(chains, s_bfs)]
        v_bfs = [(u_ref[b, rows, qcols(hh)] - ws).astype(BF16) for (b, hh), ws in zip(chains, wss)]
        upds = [_tn_dot(kd_ref[b, rows, qcols(hh)], v_bf) for (b, hh), v_bf in zip(chains, v_bfs)]
        g_cols = [jnp.broadcast_to(gcss[b][:, hh:hh + 1], (c, LANES)) for b, hh in chains]
        for j, (b, hh) in enumerate(chains):
            s_ref[b, hh] = s_olds[j] * jnp.exp(g_cols[j][c - 1:c, :]) + upds[j]
        qss = [jnp.dot(qkv_ref[b, rows, qcols(hh)].astype(BF16), s_bf, preferred_element_type=F32)
               for (b, hh), s_bf in zip(chains, s_bfs)]
        zero = jnp.zeros((c, DV), BF16)
        qkvs = []
        for j, (b, p) in enumerate(pairs):
            v0, v1 = v_bfs[2 * j], v_bfs[2 * j + 1]
            rhs = cat([cat([v0, zero], axis=1), cat([zero, v1], axis=1)], axis=0)
            both = jnp.dot(qk_ref[b, rows, pcols(p)], rhs, preferred_element_type=F32)
            qkvs += [both[:, :DV], both[:, DV:]]
        for j, (b, hh) in enumerate(chains):
            o = qss[j] * jnp.exp(g_cols[j]) + qkvs[j]
            oa_ref[b, rows, qcols(hh)] = _rmsnorm(o, og) * _silu(gate_ref[b, rows, qcols(hh)])
        return carry

    lax.fori_loop(0, tm // (c * PREP_CHUNKS), prepare, 0)
    lax.fori_loop(0, tm // c, recur, 0)

    @pl.when(i == nt - 1)
    def _():
        sout_ref[...] = s_ref[...]


def _delta_prompt(qkv, gb, gate, layer, prm, *, nb, tm):
    batch, seq, _ = qkv.shape
    nt = seq // tm
    blk = lambda w: pl.BlockSpec((nb, tm, w), lambda b, i: (b, i, 0))
    return pl.pallas_call(
        functools.partial(_delta_kernel, nb=nb, tm=tm, nt=nt),
        grid=(batch // nb, nt),
        in_specs=[blk(QKV), blk(AB_W), blk(W_A), _layer_spec(prm["og"], layer)],
        out_specs=[blk(W_A), pl.BlockSpec((nb, H_A, DK, DV), lambda b, i: (b, 0, 0, 0))],
        out_shape=[jax.ShapeDtypeStruct((batch, seq, W_A), F32),
                   jax.ShapeDtypeStruct((batch, H_A, DK, DV), F32)],
        scratch_shapes=[pltpu.VMEM((nb, H_A, DK, DV), F32),
                        pltpu.VMEM((nb, tm, AB_W), F32),
                        pltpu.VMEM((nb, tm, W_A), F32),
                        pltpu.VMEM((nb, tm, W_A), BF16),
                        pltpu.VMEM((nb, tm, H_A // 2 * LANES), BF16),
                        pltpu.VMEM((nb, tm, W_A), BF16)],
        compiler_params=pltpu.CompilerParams(dimension_semantics=("arbitrary", "arbitrary"),
                                             vmem_limit_bytes=VMEM_LIMIT),
        name="delta_prompt",
    )(qkv, gb, gate, prm["og"])


def _out_ffn_kernel(x_ref, oa_ref, ob_ref, wo_ref, gf_ref, wup_ref, wdn_ref, gl_ref, out_ref, *, final_norm):
    y = jnp.dot(oa_ref[...].astype(BF16), wo_ref[0:W_A, :], preferred_element_type=F32)
    y = y + jnp.dot(ob_ref[...].astype(BF16), wo_ref[W_A:W_A + W_B, :], preferred_element_type=F32)
    x1 = x_ref[...] + y
    h = _rmsnorm(x1, gf_ref[...]).astype(BF16)
    d_ff = wup_ref.shape[1]
    ffn = None
    for j in range(d_ff // FF_BLOCK):
        cols = slice(j * FF_BLOCK, (j + 1) * FF_BLOCK)
        a = jnp.dot(h, wup_ref[:, cols], preferred_element_type=F32)
        a = jnp.square(jnp.maximum(a, 0.0)).astype(BF16)
        part = jnp.dot(a, wdn_ref[cols, :], preferred_element_type=F32)
        ffn = part if ffn is None else ffn + part
    x2 = x1 + ffn
    if final_norm:
        x2 = _rmsnorm(x2, gl_ref[...])
    out_ref[...] = x2


def _out_ffn(x2d, oa, ob, layer, prm, *, tm, final_norm):
    rows, d = x2d.shape
    row_spec = lambda w: pl.BlockSpec((tm, w), lambda i: (i, 0))
    names = ("wo", "gffn", "wup", "wdn")
    return pl.pallas_call(
        functools.partial(_out_ffn_kernel, final_norm=final_norm),
        grid=(rows // tm,),
        in_specs=[row_spec(d), row_spec(W_A), row_spec(W_B)] + [_layer_spec(prm[n], layer) for n in names]
                 + [pl.BlockSpec((1, d), lambda i: (0, 0))],
        out_specs=row_spec(d),
        out_shape=jax.ShapeDtypeStruct((rows, d), F32),
        compiler_params=pltpu.CompilerParams(dimension_semantics=("arbitrary",),
                                             vmem_limit_bytes=VMEM_LIMIT),
        name="out_ffn",
    )(x2d, oa, ob, *[prm[n] for n in names], prm["gl"])


def _sample_in_kernel(x_ref, g_ref, win_ref, cw_ref, alog_ref, dtb_ref, vg_ref, ws0_ref, bs0_ref,
                      c0_ref, c1_ref, c2_ref, qkv_ref, gate_ref, gb_ref, ob_ref, vb_ref, cnew_ref):
    h = _rmsnorm(x_ref[...], g_ref[...]).astype(BF16)
    p = jnp.dot(h, win_ref[...], preferred_element_type=F32)
    pq = p[:, :QKV]
    cw = cw_ref[...]
    c1 = c1_ref[...]
    c2 = c2_ref[...]
    y = c0_ref[...] * cw[0:1, :] + c1 * cw[1:2, :] + c2 * cw[2:3, :] + pq * cw[3:4, :]
    cnew_ref[:, 0:QKV] = c1
    cnew_ref[:, QKV:2 * QKV] = c2
    cnew_ref[:, 2 * QKV:3 * QKV] = pq
    for j in range(QKV // CONV_BLK):
        cols = slice(j * CONV_BLK, (j + 1) * CONV_BLK)
        qkv_ref[:, cols] = _qkv_activation(y[:, cols], j)
    gate_ref[...] = p[:, OFF_GATE:OFF_U]
    gb_ref[...] = _decay_beta(p[:, OFF_AB:PROJ_PAD], alog_ref[...], dtb_ref[...])
    vb = _rmsnorm(p[:, OFF_V:OFF_AB], vg_ref[...])
    vb_ref[...] = vb
    ob_ref[...] = p[:, OFF_U:OFF_V] * (vb * ws0_ref[...] + bs0_ref[...])


def _sample_in(x2d, conv_flat, layer, prm):
    n, d = x2d.shape
    full = lambda w: pl.BlockSpec((n, w), lambda i: (0, 0))
    conv_col = lambda j: pl.BlockSpec((None, n, QKV), lambda i: (layer, 0, j))
    names = ("gmix", "win", "cw", "alog", "dtb", "vg", "ws0", "bs0")
    return pl.pallas_call(
        _sample_in_kernel,
        grid=(1,),
        in_specs=[full(d)] + [_layer_spec(prm[k], layer) for k in names] + [conv_col(j) for j in range(CONV_W - 1)],
        out_specs=[full(QKV), full(W_A), full(AB_W), full(W_B), full(W_B), full((CONV_W - 1) * QKV)],
        out_shape=[jax.ShapeDtypeStruct((n, QKV), F32), jax.ShapeDtypeStruct((n, W_A), F32),
                   jax.ShapeDtypeStruct((n, AB_W), F32), jax.ShapeDtypeStruct((n, W_B), F32),
                   jax.ShapeDtypeStruct((n, W_B), F32),
                   jax.ShapeDtypeStruct((n, (CONV_W - 1) * QKV), F32)],
        compiler_params=pltpu.CompilerParams(dimension_semantics=("arbitrary",),
                                             vmem_limit_bytes=VMEM_LIMIT),
        name="sample_in",
    )(x2d, *[prm[k] for k in names], conv_flat, conv_flat, conv_flat)


def _delta_step_kernel(*refs, tb, chained):
    if chained:
        qkv_ref, gb_ref, gate_ref, og_ref, s_ref, _, oa_ref, snew_ref = refs
    else:
        qkv_ref, gb_ref, gate_ref, og_ref, s_ref, oa_ref, snew_all_ref = refs
        snew_ref = snew_all_ref.at[0]
        snew_all_ref[1:] = jnp.zeros((snew_all_ref.shape[0] - 1,) + snew_all_ref.shape[1:], F32)
    og = og_ref[...]
    gb = gb_ref[...]
    for hh in range(H_A):
        q = qkv_ref[:, hh * DK:(hh + 1) * DK]
        k = qkv_ref[:, H_A * DK + hh * DK:H_A * DK + (hh + 1) * DK]
        v = qkv_ref[:, 2 * H_A * DK + hh * DV:2 * H_A * DK + (hh + 1) * DV]
        q_t = q.T
        k_t = k.T
        decay = jnp.exp(gb[:, hh:hh + 1])
        beta = gb[:, H_A + hh:H_A + hh + 1]
        o_rows = []
        for t in range(tb):
            s = s_ref[t, hh] * decay[t:t + 1, :]
            k_col = jnp.broadcast_to(k_t[:, t:t + 1], (DK, DV))
            q_col = jnp.broadcast_to(q_t[:, t:t + 1], (DK, DV))
            kv = jnp.sum(s * k_col, axis=0, keepdims=True)
            delta = (v[t:t + 1, :] - kv) * beta[t:t + 1, :]
            s = s + k_col * delta
            snew_ref[t, hh] = s
            o_rows.append(jnp.sum(s * q_col, axis=0, keepdims=True))
        o = jnp.concatenate(o_rows, axis=0)
        gate = gate_ref[:, hh * DV:(hh + 1) * DV]
        oa_ref[:, hh * DV:(hh + 1) * DV] = _rmsnorm(o, og) * _silu(gate)


def _delta_step(qkv, gb, gate, state_all, new_all, layer, prm, *, tb):
    n = qkv.shape[0]
    depth = state_all.shape[0]
    row_spec = lambda w: pl.BlockSpec((tb, w), lambda i: (i, 0))
    st_spec = pl.BlockSpec((None, tb, H_A, DK, DV), lambda i: (layer, i, 0, 0, 0))
    chained = new_all is not None
    in_specs = [row_spec(QKV), row_spec(AB_W), row_spec(W_A), _layer_spec(prm["og"], layer), st_spec]
    args = [qkv, gb, gate, prm["og"], state_all]
    if chained:
        in_specs.append(pl.BlockSpec(memory_space=pl.ANY))
        args.append(new_all)
        new_spec = st_spec
    else:
        assert layer == 0
        new_spec = pl.BlockSpec((depth, tb, H_A, DK, DV), lambda i: (0, i, 0, 0, 0))
    return pl.pallas_call(
        functools.partial(_delta_step_kernel, tb=tb, chained=chained),
        grid=(n // tb,),
        in_specs=in_specs,
        out_specs=[row_spec(W_A), new_spec],
        out_shape=[jax.ShapeDtypeStruct((n, W_A), F32), jax.ShapeDtypeStruct(state_all.shape, F32)],
        input_output_aliases={len(args) - 1: 1} if chained else {},
        compiler_params=pltpu.CompilerParams(dimension_semantics=("arbitrary",),
                                             vmem_limit_bytes=VMEM_LIMIT),
        name="delta_step",
    )(*args)


def _pick_tile(seq, candidates):
    for tm in candidates:
        if seq % tm == 0:
            return tm
    raise ValueError(f"prompt length must be a multiple of {candidates[-1]}")


def kernel(x_prompt, x_sample, state_delta, state_conv, norm_mix_g, w_in, conv_w, A_log, dt_bias, o_norm_g,
           v_norm_g, w_s, b_s, w_o, norm_ffn_g, w_up, w_down, norm_f_g):
    batch, seq, d = x_prompt.shape
    n_dec, dec_seq, _ = x_sample.shape
    depth = w_in.shape[0]
    assert dec_seq == 1 and seq % GM_CHUNK == 0 and n_dec % SUBLANES == 0
    assert seq % PROJ_ROWS == 0
    tm = _pick_tile(seq, (512, 256, 128))
    ptm = _pick_tile(seq, (2 * PROJ_ROWS, PROJ_ROWS))
    tb = 2 * SUBLANES if n_dec % (2 * SUBLANES) == 0 else SUBLANES
    nb = next(n for n in (8, 4, 2, 1) if batch % n == 0)
    dtm = GM_CHUNK

    off_a = QKV + W_A
    off_u = off_a + 2 * H_A
    off_v = off_u + W_B
    lane_pad = lambda v: jnp.pad(v.astype(F32), ((0, 0), (0, AB_W - v.shape[1]))).reshape(depth, 1, AB_W)
    prm = {
        "win": jnp.concatenate(
            [w_in[:, :, :off_a], w_in[:, :, off_u:off_v], w_in[:, :, off_v:], w_in[:, :, off_a:off_u],
             jnp.zeros((depth, d, AB_W - 2 * H_A), w_in.dtype)], axis=-1).astype(BF16),
        "wo": w_o.astype(BF16),
        "wup": w_up.astype(BF16),
        "wdn": w_down.astype(BF16),
        "gmix": norm_mix_g.reshape(depth, 1, d),
        "gffn": norm_ffn_g.reshape(depth, 1, d),
        "gl": norm_f_g.reshape(1, d),
        "cw": conv_w,
        "alog": lane_pad(A_log),
        "dtb": lane_pad(dt_bias),
        "vg": v_norm_g.reshape(depth, 1, W_B),
        "og": o_norm_g.reshape(depth, 1, DV),
        "ws": w_s,
        "bsb": jnp.broadcast_to(b_s[:, :, :, None], b_s.shape + (DH_B,)).astype(F32),
        "ws0": jnp.repeat(w_s[:, :, 0, 0], DH_B, axis=-1).reshape(depth, 1, W_B),
        "bs0": jnp.repeat(b_s[:, :, 0], DH_B, axis=-1).reshape(depth, 1, W_B),
    }
    conv_flat = state_conv.reshape(depth, n_dec, (CONV_W - 1) * QKV)

    xp = x_prompt.reshape(batch * seq, d)
    xs = x_sample.reshape(n_dec, d)
    dp, cp, cs, vs = [], [], [], []
    new_state = None
    for l in range(depth):
        last = l == depth - 1
        qkv, gate, gb, ob, ctail = _prompt_in(xp, l, prm, batch=batch, seq=seq, tm=ptm)
        oa, s_fin = _delta_prompt(qkv.reshape(batch, seq, QKV), gb.reshape(batch, seq, AB_W),
                                  gate.reshape(batch, seq, W_A), l, prm, nb=nb, tm=dtm)
        xp = _out_ffn(xp, oa.reshape(batch * seq, W_A), ob, l, prm, tm=tm, final_norm=last)
        dp.append(s_fin)
        cp.append(ctail)

        qkv_s, gate_s, gb_s, ob_s, vb_s, cnew = _sample_in(xs, conv_flat, l, prm)
        oa_s, new_state = _delta_step(qkv_s, gb_s, gate_s, state_delta, new_state, l, prm,
                                      tb=SUBLANES if new_state is None else tb)
        xs = _out_ffn(xs, oa_s, ob_s, l, prm, tm=n_dec, final_norm=last)
        cs.append(cnew.reshape(n_dec, CONV_W - 1, QKV))
        vs.append(vb_s.reshape(n_dec, 1, W_B))

    return (xp.reshape(batch, seq, d), xs.reshape(n_dec, 1, d), jnp.stack(dp), jnp.stack(cp),
            new_state, jnp.stack(cs), jnp.stack(vs))
```

```python
import functools

import jax
import jax.numpy as jnp
from jax import lax
from jax.experimental import pallas as pl
from jax.experimental.pallas import tpu as pltpu

F32 = jnp.float32
BF16 = jnp.bfloat16

H_A = 4
DK = 128
DV = 128
W_A = H_A * DV
QKV = 2 * H_A * DK + H_A * DV
CONV_W = 4
DN_CHUNK = 64
H_B = 4
DH_B = 128
W_B = H_B * DH_B
GM_CHUNK = 128
EPS = 1e-6

LANES = 128
SUBLANES = 8
AB_W = LANES
OFF_GATE = QKV
OFF_U = OFF_GATE + W_A
OFF_V = OFF_U + W_B
OFF_AB = OFF_V + W_B
PROJ_PAD = OFF_AB + AB_W
CONV_BLK = H_A * DK

VMEM_LIMIT = 56 * 1024 * 1024
FF_BLOCK = 1024
PROJ_ROWS = 512
PREP_CHUNKS = 2


def _rmsnorm(x, g):
    return x * lax.rsqrt(jnp.mean(x * x, axis=-1, keepdims=True) + EPS) * g


def _silu(x):
    return x * jax.nn.sigmoid(x)


def _softplus(x):
    return jnp.maximum(x, 0.0) + jnp.log1p(jnp.exp(-jnp.abs(x)))


def _layer_spec(arr, layer):
    nd = arr.ndim - 1
    return pl.BlockSpec((None,) + arr.shape[1:], lambda *_: (layer,) + (0,) * nd,
                        pipeline_mode=pl.Buffered(1))


def _qkv_activation(y, j):
    a = _silu(y)
    if j >= 2:
        return a
    blocks = []
    for hh in range(H_A):
        blk = a[:, hh * DK:(hh + 1) * DK]
        inv = lax.rsqrt(jnp.sum(blk * blk, axis=-1, keepdims=True) + EPS)
        if j == 0:
            inv = inv * (DK ** -0.5)
        blocks.append(blk * inv)
    return jnp.concatenate(blocks, axis=1)


def _decay_beta(ab, alog, dtb):
    lane = lax.broadcasted_iota(jnp.int32, ab.shape, 1)
    gdec = -jnp.exp(alog) * _softplus(ab + dtb)
    beta = jax.nn.sigmoid(ab)
    return jnp.where(lane < H_A, gdec, beta)


def _prompt_in_kernel(x_ref, g_ref, win_ref, cw_ref, alog_ref, dtb_ref, vg_ref, ws_ref, bsb_ref,
                      qkv_ref, gate_ref, gb_ref, ob_ref, ctail_ref, carry_ref, *, tm, nt):
    i = pl.program_id(1)
    sm = PROJ_ROWS
    n_sub = tm // sm

    @pl.when(i == 0)
    def _():
        carry_ref[...] = jnp.zeros(carry_ref.shape, F32)

    sub = lax.broadcasted_iota(jnp.int32, (SUBLANES, CONV_BLK), 0)
    ri = lax.broadcasted_iota(jnp.int32, (GM_CHUNK, GM_CHUNK), 0)
    ci = lax.broadcasted_iota(jnp.int32, (GM_CHUNK, GM_CHUNK), 1)
    nc = sm // GM_CHUNK
    hs, tails, mixes = {}, {}, {}

    def proj(s, lo, hi):
        if s not in hs:
            hs[s] = _rmsnorm(x_ref[s * sm:(s + 1) * sm, :], g_ref[...]).astype(BF16)
        return jnp.dot(hs[s], win_ref[:, lo:hi], preferred_element_type=F32)

    def conv_act(s, j, pj):
        rows = slice(s * sm, (s + 1) * sm)
        cols = slice(j * CONV_BLK, (j + 1) * CONV_BLK)
        cw = cw_ref[:, cols]
        prev = carry_ref[:, cols] if s == 0 else tails[s - 1, j]

        def shift(x, x_prev, k):
            sh = pltpu.roll(x, k, axis=0)
            head = jnp.where(sub < k, pltpu.roll(x_prev, k, axis=0), sh[0:SUBLANES])
            return jnp.concatenate([head, sh[SUBLANES:]], axis=0)

        w0, w1, w2, w3 = (cw[k:k + 1, :] for k in range(CONV_W))
        pj1 = shift(pj, prev, 1)
        far = pj * w1 + pj1 * w0
        far_prev = prev * w1 + pltpu.roll(prev, 1, axis=0) * w0
        y = (pj * w3 + pj1 * w2) + shift(far, far_prev, 2)
        tails[s, j] = pj[sm - SUBLANES:sm, :]
        if s == n_sub - 1:
            carry_ref[:, cols] = tails[s, j]
            ctail_ref[0, :, cols] = pj[sm - (CONV_W - 1):sm, :]
        qkv_ref[rows, cols] = _qkv_activation(y, j)

    def gmlp_mix(s, pv):
        vb = _rmsnorm(pv, vg_ref[...]).astype(BF16)
        mixes[s] = []
        for hh in range(H_B):
            wm = jnp.where(ri >= ci, ws_ref[hh], 0.0).astype(BF16)
            cols = slice(hh * DH_B, (hh + 1) * DH_B)
            rhs = jnp.concatenate([vb[c * GM_CHUNK:(c + 1) * GM_CHUNK, cols] for c in range(nc)], axis=1)
            mixes[s].append(jnp.dot(wm, rhs, preferred_element_type=F32))

    def gmlp_gate(s, pu):
        for hh in range(H_B):
            cols = slice(hh * DH_B, (hh + 1) * DH_B)
            bias = bsb_ref[hh]
            for c in range(nc):
                src = slice(c * GM_CHUNK, (c + 1) * GM_CHUNK)
                dst = slice(s * sm + c * GM_CHUNK, s * sm + (c + 1) * GM_CHUNK)
                ob_ref[dst, cols] = pu[src, cols] * (mixes[s][hh][:, c * DH_B:(c + 1) * DH_B] + bias)

    def store_gb(s, pab):
        gb_ref[s * sm:(s + 1) * sm, :] = _decay_beta(pab, alog_ref[...], dtb_ref[...])

    def store_gate(s, pg):
        gate_ref[s * sm:(s + 1) * sm, :] = pg

    stages = []
    for s in range(n_sub):
        for j in range(QKV // CONV_BLK):
            stages.append((s, j * CONV_BLK, (j + 1) * CONV_BLK, functools.partial(conv_act, s, j)))
        stages.append((s, OFF_V, OFF_AB, functools.partial(gmlp_mix, s)))
        stages.append((s, OFF_U, OFF_V, functools.partial(gmlp_gate, s)))
        stages.append((s, OFF_AB, PROJ_PAD, functools.partial(store_gb, s)))
        stages.append((s, OFF_GATE, OFF_U, functools.partial(store_gate, s)))
    pending = proj(*stages[0][:3])
    for n, stage in enumerate(stages):
        ahead = proj(*stages[n + 1][:3]) if n + 1 < len(stages) else None
        stage[3](pending)
        pending = ahead


def _prompt_in(x2d, layer, prm, *, batch, seq, tm):
    nt = seq // tm
    rows = batch * seq
    d = x2d.shape[1]
    row_spec = lambda w: pl.BlockSpec((tm, w), lambda b, i: (b * nt + i, 0))
    names = ("gmix", "win", "cw", "alog", "dtb", "vg", "ws", "bsb")
    return pl.pallas_call(
        functools.partial(_prompt_in_kernel, tm=tm, nt=nt),
        grid=(batch, nt),
        in_specs=[row_spec(d)] + [_layer_spec(prm[n], layer) for n in names],
        out_specs=[row_spec(QKV), row_spec(W_A), row_spec(AB_W), row_spec(W_B),
                   pl.BlockSpec((1, CONV_W - 1, QKV), lambda b, i: (b, 0, 0))],
        out_shape=[jax.ShapeDtypeStruct((rows, QKV), F32), jax.ShapeDtypeStruct((rows, W_A), F32),
                   jax.ShapeDtypeStruct((rows, AB_W), F32), jax.ShapeDtypeStruct((rows, W_B), F32),
                   jax.ShapeDtypeStruct((batch, CONV_W - 1, QKV), F32)],
        scratch_shapes=[pltpu.VMEM((SUBLANES, QKV), F32)],
        compiler_params=pltpu.CompilerParams(dimension_semantics=("arbitrary", "arbitrary"),
                                             vmem_limit_bytes=VMEM_LIMIT),
        name="prompt_in",
    )(x2d, *[prm[n] for n in names])


def _pair_blockdiag(m, left):
    zero = jnp.zeros_like(m)
    return jnp.concatenate([jnp.where(left, m, zero), jnp.where(left, zero, m)], axis=0)


def _unit_lower_inverses(lpairs, ri, cj, left):
    c = lpairs[0].shape[0]
    eye = (ri == cj).astype(F32)
    xs = None
    s = 1
    while s < c:
        mask = ((ri // (2 * s)) == (cj // (2 * s))) & (((ri // s) % 2) == 1) & (((cj // s) % 2) == 0)
        css = [jnp.where(mask, lp, 0.0) for lp in lpairs]
        if xs is None:
            xs = [eye - cs for cs in css]
        else:
            xbs = [x.astype(BF16) for x in xs]
            ys = [jnp.dot(xb, _pair_blockdiag(cs.astype(BF16), left), preferred_element_type=F32)
                  for xb, cs in zip(xbs, css)]
            zs = [jnp.dot(y.astype(BF16), _pair_blockdiag(xb, left), preferred_element_type=F32)
                  for y, xb in zip(ys, xbs)]
            xs = [x - z for x, z in zip(xs, zs)]
        s *= 2
    return xs


def _nt_dot(a, b):
    return lax.dot_general(a.astype(BF16), b.astype(BF16), (((1,), (1,)), ((), ())),
                           preferred_element_type=F32)


def _tn_dot(a, b):
    return lax.dot_general(a.astype(BF16), b.astype(BF16), (((0,), (0,)), ((), ())),
                           preferred_element_type=F32)


def _dot(a, b):
    return jnp.dot(a.astype(BF16), b.astype(BF16), preferred_element_type=F32)


def _delta_kernel(qkv_ref, gb_ref, gate_ref, og_ref, oa_ref, sout_ref,
                  s_ref, gcs_ref, u_ref, w_ref, qe_ref, qk_ref, kdt_ref, *, nb, tm, nt):
    i = pl.program_id(1)
    c = DN_CHUNK

    @pl.when(i == 0)
    def _():
        s_ref[...] = jnp.zeros(s_ref.shape, F32)

    assert 2 * c == LANES and H_A % 2 == 0
    ri = lax.broadcasted_iota(jnp.int32, (c, LANES), 0)
    lane = lax.broadcasted_iota(jnp.int32, (c, LANES), 1)
    left = lane < c
    cj = jnp.where(left, lane, lane - c)
    lower = ri >= cj
    strict = ri > cj
    og = og_ref[...]
    chains = [(b, hh) for b in range(nb) for hh in range(H_A)]
    pairs = [(b, p) for b in range(nb) for p in range(H_A // 2)]
    qcols = lambda hh: slice(hh * DK, (hh + 1) * DK)
    kcols = lambda hh: slice(H_A * DK + hh * DK, H_A * DK + (hh + 1) * DK)
    vcols = lambda hh: slice(2 * H_A * DK + hh * DV, 2 * H_A * DK + (hh + 1) * DV)
    pcols = lambda p: slice(p * LANES, (p + 1) * LANES)
    cat = jnp.concatenate

    def prepare(n, carry):
        chunk_ids = [n * PREP_CHUNKS + e for e in range(PREP_CHUNKS) for _ in range(nb)]
        units = [(b, pl.ds(pl.multiple_of((n * PREP_CHUNKS + e) * c, c), c))
                 for e in range(PREP_CHUNKS) for b in range(nb)]
        chains = [(ui, hh) for ui in range(len(units)) for hh in range(H_A)]
        pairs = [(ui, p) for ui in range(len(units)) for p in range(H_A // 2)]
        gbcs, gcss, gcsts = [], [], []
        for b, rows in units:
            gbc = gb_ref[b, rows, :]
            gcs = gbc
            sh = 1
            while sh < c:
                gcs = gcs + jnp.where(ri >= sh, pltpu.roll(gcs, sh, axis=0), 0.0)
                sh *= 2
            gcs_ref[b, rows, :] = gcs
            gbcs.append(gbc)
            gcss.append(gcs)
            gcsts.append(gcs.T)
        qs, ks, vs, kbs, betas, g_cols = {}, {}, {}, {}, {}, {}
        for ui, hh in chains:
            b, rows = units[ui]
            qs[ui, hh] = qkv_ref[b, rows, qcols(hh)]
            ks[ui, hh] = qkv_ref[b, rows, kcols(hh)]
            vs[ui, hh] = qkv_ref[b, rows, vcols(hh)]
            g_cols[ui, hh] = jnp.broadcast_to(gcss[ui][:, hh:hh + 1], (c, LANES))
            betas[ui, hh] = jnp.broadcast_to(gbcs[ui][:, H_A + hh:H_A + hh + 1], (c, LANES))
            kbs[ui, hh] = ks[ui, hh] * betas[ui, hh]
        decays, kqs = [], []
        for ui, p in pairs:
            b, rows = units[ui]
            h0, h1 = 2 * p, 2 * p + 1
            g_col = jnp.where(left, g_cols[ui, h0], g_cols[ui, h1])
            g_row = jnp.broadcast_to(cat([gcsts[ui][h0:h0 + 1, :], gcsts[ui][h1:h1 + 1, :]], axis=1), (c, LANES))
            decays.append(jnp.where(lower, jnp.exp(jnp.where(lower, g_col - g_row, 0.0)), 0.0))
            zero = jnp.zeros((c, DK), F32)
            lhs = cat([cat([kbs[ui, h0], kbs[ui, h1]], axis=1), cat([qs[ui, h0], qs[ui, h1]], axis=1)], axis=0)
            rhs = cat([cat([ks[ui, h0], zero], axis=1), cat([zero, ks[ui, h1]], axis=1)], axis=0)
            kqs.append(_nt_dot(lhs, rhs))
        lpairs = [jnp.where(strict, kq[:c] * decay, 0.0) for kq, decay in zip(kqs, decays)]
        ts = _unit_lower_inverses(lpairs, ri, cj, left)
        for j, (ui, p) in enumerate(pairs):
            b, rows = units[ui]
            h0, h1 = 2 * p, 2 * p + 1
            zero = jnp.zeros((c, DV), F32)
            vb0 = vs[ui, h0] * betas[ui, h0]
            vb1 = vs[ui, h1] * betas[ui, h1]
            kg0 = kbs[ui, h0] * jnp.exp(g_cols[ui, h0])
            kg1 = kbs[ui, h1] * jnp.exp(g_cols[ui, h1])
            rhs = cat([cat([vb0, zero, kg0, zero], axis=1), cat([zero, vb1, zero, kg1], axis=1)], axis=0)
            uw = _dot(ts[j], rhs)
            u_ref[b, rows, qcols(h0)] = uw[:, 0:DV]
            u_ref[b, rows, qcols(h1)] = uw[:, DV:2 * DV]
            w_ref[b, rows, qcols(h0)] = uw[:, 2 * DV:3 * DV].astype(BF16)
            w_ref[b, rows, qcols(h1)] = uw[:, 3 * DV:4 * DV].astype(BF16)
            qk_ref[b, rows, pcols(p)] = (kqs[j][c:] * decays[j]).astype(BF16)
        for ui, hh in chains:
            b, rows = units[ui]
            g_col = g_cols[ui, hh]
            qe_ref[b, rows, qcols(hh)] = (qs[ui, hh] * jnp.exp(g_col)).astype(BF16)
            k_dec = ks[ui, hh] * jnp.exp(g_col[c - 1:c, :] - g_col)
            kdt_ref[b, hh, pl.ds(pl.multiple_of(chunk_ids[ui] * DK, DK), DK), :] = k_dec.T.astype(BF16)
        return carry

    def recur(n, carry):
        rows = pl.ds(pl.multiple_of(n * c, c), c)
        gcss = [gcs_ref[b, rows, :] for b in range(nb)]
        s_olds = [s_ref[b, hh] for b, hh in chains]
        s_bfs = [s.astype(BF16) for s in s_olds]
        wqs = [jnp.dot(cat([w_ref[b, rows, qcols(hh)], qe_ref[b, rows, qcols(hh)]], axis=0), s_bf,
                       preferred_element_type=F32) for (b, hh), s_bf in zip(chains, s_bfs)]
        v_bfs = [(u_ref[b, rows, qcols(hh)] - wq[:c]).astype(BF16) for (b, hh), wq in zip(chains, wqs)]
        kdt_rows = pl.ds(pl.multiple_of(n * DK, DK), DK)
        upds = [jnp.dot(kdt_ref[b, hh, kdt_rows, :], v_bf, preferred_element_type=F32)
                for (b, hh), v_bf in zip(chains, v_bfs)]
        for j, (b, hh) in enumerate(chains):
            g_last = jnp.broadcast_to(gcss[b][c - 1:c, hh:hh + 1], (1, LANES))
            s_ref[b, hh] = s_olds[j] * jnp.exp(g_last) + upds[j]
        zero = jnp.zeros((c, DV), BF16)
        qkvs = []
        for j, (b, p) in enumerate(pairs):
            v0, v1 = v_bfs[2 * j], v_bfs[2 * j + 1]
            rhs = cat([cat([v0, zero], axis=1), cat([zero, v1], axis=1)], axis=0)
            both = jnp.dot(qk_ref[b, rows, pcols(p)], rhs, preferred_element_type=F32)
            qkvs += [both[:, :DV], both[:, DV:]]
        for j, (b, hh) in enumerate(chains):
            o = wqs[j][c:] + qkvs[j]
            oa_ref[b, rows, qcols(hh)] = _rmsnorm(o, og) * _silu(gate_ref[b, rows, qcols(hh)])
        return carry

    lax.fori_loop(0, tm // (c * PREP_CHUNKS), prepare, 0)
    lax.fori_loop(0, tm // c, recur, 0)

    @pl.when(i == nt - 1)
    def _():
        sout_ref[...] = s_ref[...]


def _delta_prompt(qkv, gb, gate, layer, prm, *, nb, tm):
    batch, seq, _ = qkv.shape
    nt = seq // tm
    blk = lambda w: pl.BlockSpec((nb, tm, w), lambda b, i: (b, i, 0))
    return pl.pallas_call(
        functools.partial(_delta_kernel, nb=nb, tm=tm, nt=nt),
        grid=(batch // nb, nt),
        in_specs=[blk(QKV), blk(AB_W), blk(W_A), _layer_spec(prm["og"], layer)],
        out_specs=[blk(W_A), pl.BlockSpec((nb, H_A, DK, DV), lambda b, i: (b, 0, 0, 0))],
        out_shape=[jax.ShapeDtypeStruct((batch, seq, W_A), F32),
                   jax.ShapeDtypeStruct((batch, H_A, DK, DV), F32)],
        scratch_shapes=[pltpu.VMEM((nb, H_A, DK, DV), F32),
                        pltpu.VMEM((nb, tm, AB_W), F32),
                        pltpu.VMEM((nb, tm, W_A), F32),
                        pltpu.VMEM((nb, tm, W_A), BF16),
                        pltpu.VMEM((nb, tm, W_A), BF16),
                        pltpu.VMEM((nb, tm, H_A // 2 * LANES), BF16),
                        pltpu.VMEM((nb, H_A, tm // DN_CHUNK * DK, DN_CHUNK), BF16)],
        compiler_params=pltpu.CompilerParams(dimension_semantics=("arbitrary", "arbitrary"),
                                             vmem_limit_bytes=VMEM_LIMIT),
        name="delta_prompt",
    )(qkv, gb, gate, prm["og"])


def _out_ffn_kernel(x_ref, oa_ref, ob_ref, wo_ref, gf_ref, wup_ref, wdn_ref, gl_ref, out_ref, *, final_norm):
    y = jnp.dot(oa_ref[...].astype(BF16), wo_ref[0:W_A, :], preferred_element_type=F32)
    y = y + jnp.dot(ob_ref[...].astype(BF16), wo_ref[W_A:W_A + W_B, :], preferred_element_type=F32)
    x1 = x_ref[...] + y
    h = _rmsnorm(x1, gf_ref[...]).astype(BF16)
    d_ff = wup_ref.shape[1]
    ffn = None
    for j in range(d_ff // FF_BLOCK):
        cols = slice(j * FF_BLOCK, (j + 1) * FF_BLOCK)
        a = jnp.dot(h, wup_ref[:, cols], preferred_element_type=F32)
        a = jnp.square(jnp.maximum(a, 0.0)).astype(BF16)
        part = jnp.dot(a, wdn_ref[cols, :], preferred_element_type=F32)
        ffn = part if ffn is None else ffn + part
    x2 = x1 + ffn
    if final_norm:
        x2 = _rmsnorm(x2, gl_ref[...])
    out_ref[...] = x2


def _out_ffn(x2d, oa, ob, layer, prm, *, tm, final_norm):
    rows, d = x2d.shape
    row_spec = lambda w: pl.BlockSpec((tm, w), lambda i: (i, 0))
    names = ("wo", "gffn", "wup", "wdn")
    return pl.pallas_call(
        functools.partial(_out_ffn_kernel, final_norm=final_norm),
        grid=(rows // tm,),
        in_specs=[row_spec(d), row_spec(W_A), row_spec(W_B)] + [_layer_spec(prm[n], layer) for n in names]
                 + [pl.BlockSpec((1, d), lambda i: (0, 0))],
        out_specs=row_spec(d),
        out_shape=jax.ShapeDtypeStruct((rows, d), F32),
        compiler_params=pltpu.CompilerParams(dimension_semantics=("arbitrary",),
                                             vmem_limit_bytes=VMEM_LIMIT),
        name="out_ffn",
    )(x2d, oa, ob, *[prm[n] for n in names], prm["gl"])


def _sample_in_kernel(x_ref, g_ref, win_ref, cw_ref, alog_ref, dtb_ref, vg_ref, ws0_ref, bs0_ref,
                      c0_ref, c1_ref, c2_ref, qkv_ref, gate_ref, gb_ref, ob_ref, vb_ref, cnew_ref):
    h = _rmsnorm(x_ref[...], g_ref[...]).astype(BF16)
    p = jnp.dot(h, win_ref[...], preferred_element_type=F32)
    pq = p[:, :QKV]
    cw = cw_ref[...]
    c1 = c1_ref[...]
    c2 = c2_ref[...]
    y = c0_ref[...] * cw[0:1, :] + c1 * cw[1:2, :] + c2 * cw[2:3, :] + pq * cw[3:4, :]
    cnew_ref[:, 0:QKV] = c1
    cnew_ref[:, QKV:2 * QKV] = c2
    cnew_ref[:, 2 * QKV:3 * QKV] = pq
    for j in range(QKV // CONV_BLK):
        cols = slice(j * CONV_BLK, (j + 1) * CONV_BLK)
        qkv_ref[:, cols] = _qkv_activation(y[:, cols], j)
    gate_ref[...] = p[:, OFF_GATE:OFF_U]
    gb_ref[...] = _decay_beta(p[:, OFF_AB:PROJ_PAD], alog_ref[...], dtb_ref[...])
    vb = _rmsnorm(p[:, OFF_V:OFF_AB], vg_ref[...])
    vb_ref[...] = vb
    ob_ref[...] = p[:, OFF_U:OFF_V] * (vb * ws0_ref[...] + bs0_ref[...])


def _sample_in(x2d, conv_flat, layer, prm):
    n, d = x2d.shape
    full = lambda w: pl.BlockSpec((n, w), lambda i: (0, 0))
    conv_col = lambda j: pl.BlockSpec((None, n, QKV), lambda i: (layer, 0, j))
    names = ("gmix", "win", "cw", "alog", "dtb", "vg", "ws0", "bs0")
    return pl.pallas_call(
        _sample_in_kernel,
        grid=(1,),
        in_specs=[full(d)] + [_layer_spec(prm[k], layer) for k in names] + [conv_col(j) for j in range(CONV_W - 1)],
        out_specs=[full(QKV), full(W_A), full(AB_W), full(W_B), full(W_B), full((CONV_W - 1) * QKV)],
        out_shape=[jax.ShapeDtypeStruct((n, QKV), F32), jax.ShapeDtypeStruct((n, W_A), F32),
                   jax.ShapeDtypeStruct((n, AB_W), F32), jax.ShapeDtypeStruct((n, W_B), F32),
                   jax.ShapeDtypeStruct((n, W_B), F32),
                   jax.ShapeDtypeStruct((n, (CONV_W - 1) * QKV), F32)],
        compiler_params=pltpu.CompilerParams(dimension_semantics=("arbitrary",),
                                             vmem_limit_bytes=VMEM_LIMIT),
        name="sample_in",
    )(x2d, *[prm[k] for k in names], conv_flat, conv_flat, conv_flat)


def _delta_step_kernel(*refs, tb, chained):
    if chained:
        qkv_ref, gb_ref, gate_ref, og_ref, s_ref, _, oa_ref, snew_ref = refs
    else:
        qkv_ref, gb_ref, gate_ref, og_ref, s_ref, oa_ref, snew_all_ref = refs
        snew_ref = snew_all_ref.at[0]
        snew_all_ref[1:] = jnp.zeros((snew_all_ref.shape[0] - 1,) + snew_all_ref.shape[1:], F32)
    og = og_ref[...]
    gb = gb_ref[...]
    for hh in range(H_A):
        q = qkv_ref[:, hh * DK:(hh + 1) * DK]
        k = qkv_ref[:, H_A * DK + hh * DK:H_A * DK + (hh + 1) * DK]
        v = qkv_ref[:, 2 * H_A * DK + hh * DV:2 * H_A * DK + (hh + 1) * DV]
        q_t = q.T
        k_t = k.T
        decay = jnp.exp(gb[:, hh:hh + 1])
        beta = gb[:, H_A + hh:H_A + hh + 1]
        o_rows = []
        for t in range(tb):
            s = s_ref[t, hh] * decay[t:t + 1, :]
            k_col = jnp.broadcast_to(k_t[:, t:t + 1], (DK, DV))
            q_col = jnp.broadcast_to(q_t[:, t:t + 1], (DK, DV))
            kv = jnp.sum(s * k_col, axis=0, keepdims=True)
            delta = (v[t:t + 1, :] - kv) * beta[t:t + 1, :]
            s = s + k_col * delta
            snew_ref[t, hh] = s
            o_rows.append(jnp.sum(s * q_col, axis=0, keepdims=True))
        o = jnp.concatenate(o_rows, axis=0)
        gate = gate_ref[:, hh * DV:(hh + 1) * DV]
        oa_ref[:, hh * DV:(hh + 1) * DV] = _rmsnorm(o, og) * _silu(gate)


def _delta_step(qkv, gb, gate, state_all, new_all, layer, prm, *, tb):
    n = qkv.shape[0]
    depth = state_all.shape[0]
    row_spec = lambda w: pl.BlockSpec((tb, w), lambda i: (i, 0))
    st_spec = pl.BlockSpec((None, tb, H_A, DK, DV), lambda i: (layer, i, 0, 0, 0))
    chained = new_all is not None
    in_specs = [row_spec(QKV), row_spec(AB_W), row_spec(W_A), _layer_spec(prm["og"], layer), st_spec]
    args = [qkv, gb, gate, prm["og"], state_all]
    if chained:
        in_specs.append(pl.BlockSpec(memory_space=pl.ANY))
        args.append(new_all)
        new_spec = st_spec
    else:
        assert layer == 0
        new_spec = pl.BlockSpec((depth, tb, H_A, DK, DV), lambda i: (0, i, 0, 0, 0))
    return pl.pallas_call(
        functools.partial(_delta_step_kernel, tb=tb, chained=chained),
        grid=(n // tb,),
        in_specs=in_specs,
        out_specs=[row_spec(W_A), new_spec],
        out_shape=[jax.ShapeDtypeStruct((n, W_A), F32), jax.ShapeDtypeStruct(state_all.shape, F32)],
        input_output_aliases={len(args) - 1: 1} if chained else {},
        compiler_params=pltpu.CompilerParams(dimension_semantics=("arbitrary",),
                                             vmem_limit_bytes=VMEM_LIMIT),
        name="delta_step",
    )(*args)


def _reorder_win_kernel(w_ref, out_ref):
    w = w_ref[...]
    off_a = QKV + W_A
    off_u = off_a + 2 * H_A
    off_v = off_u + W_B
    out_ref[:, 0:OFF_U] = w[:, 0:off_a].astype(BF16)
    out_ref[:, OFF_U:OFF_V] = w[:, off_u:off_v].astype(BF16)
    out_ref[:, OFF_V:OFF_AB] = w[:, off_v:off_v + W_B].astype(BF16)
    ab = jnp.concatenate([w[:, off_a:off_u], jnp.zeros((w.shape[0], AB_W - 2 * H_A), F32)], axis=1)
    out_ref[:, OFF_AB:PROJ_PAD] = ab.astype(BF16)


def _reorder_win(w_in, *, rows):
    depth, d, width = w_in.shape
    return pl.pallas_call(
        _reorder_win_kernel,
        grid=(depth, d // rows),
        in_specs=[pl.BlockSpec((None, rows, width), lambda l, i: (l, i, 0))],
        out_specs=pl.BlockSpec((None, rows, PROJ_PAD), lambda l, i: (l, i, 0)),
        out_shape=jax.ShapeDtypeStruct((depth, d, PROJ_PAD), BF16),
        compiler_params=pltpu.CompilerParams(dimension_semantics=("arbitrary", "arbitrary"),
                                             vmem_limit_bytes=VMEM_LIMIT),
        name="reorder_win",
    )(w_in)


def _pick_tile(seq, candidates):
    for tm in candidates:
        if seq % tm == 0:
            return tm
    raise ValueError(f"prompt length must be a multiple of {candidates[-1]}")


def kernel(x_prompt, x_sample, state_delta, state_conv, norm_mix_g, w_in, conv_w, A_log, dt_bias, o_norm_g,
           v_norm_g, w_s, b_s, w_o, norm_ffn_g, w_up, w_down, norm_f_g):
    batch, seq, d = x_prompt.shape
    n_dec, dec_seq, _ = x_sample.shape
    depth = w_in.shape[0]
    assert dec_seq == 1 and seq % GM_CHUNK == 0 and n_dec % SUBLANES == 0
    assert seq % PROJ_ROWS == 0
    tm = _pick_tile(seq, (512, 256, 128))
    ptm = _pick_tile(seq, (2 * PROJ_ROWS, PROJ_ROWS))
    tb = 2 * SUBLANES if n_dec % (2 * SUBLANES) == 0 else SUBLANES
    nb = next(n for n in (8, 4, 2, 1) if batch % n == 0)
    dtm = GM_CHUNK

    lane_pad = lambda v: jnp.pad(v.astype(F32), ((0, 0), (0, AB_W - v.shape[1]))).reshape(depth, 1, AB_W)
    prm = {
        "win": _reorder_win(w_in, rows=256 if d % 256 == 0 else SUBLANES),
        "wo": w_o.astype(BF16),
        "wup": w_up.astype(BF16),
        "wdn": w_down.astype(BF16),
        "gmix": norm_mix_g.reshape(depth, 1, d),
        "gffn": norm_ffn_g.reshape(depth, 1, d),
        "gl": norm_f_g.reshape(1, d),
        "cw": conv_w,
        "alog": lane_pad(A_log),
        "dtb": lane_pad(dt_bias),
        "vg": v_norm_g.reshape(depth, 1, W_B),
        "og": o_norm_g.reshape(depth, 1, DV),
        "ws": w_s,
        "bsb": jnp.broadcast_to(b_s[:, :, :, None], b_s.shape + (DH_B,)).astype(F32),
        "ws0": jnp.repeat(w_s[:, :, 0, 0], DH_B, axis=-1).reshape(depth, 1, W_B),
        "bs0": jnp.repeat(b_s[:, :, 0], DH_B, axis=-1).reshape(depth, 1, W_B),
    }
    conv_flat = state_conv.reshape(depth, n_dec, (CONV_W - 1) * QKV)

    xp = x_prompt.reshape(batch * seq, d)
    xs = x_sample.reshape(n_dec, d)
    dp, cp, cs, vs = [], [], [], []
    new_state = None
    for l in range(depth):
        last = l == depth - 1
        qkv, gate, gb, ob, ctail = _prompt_in(xp, l, prm, batch=batch, seq=seq, tm=ptm)
        oa, s_fin = _delta_prompt(qkv.reshape(batch, seq, QKV), gb.reshape(batch, seq, AB_W),
                                  gate.reshape(batch, seq, W_A), l, prm, nb=nb, tm=dtm)
        xp = _out_ffn(xp, oa.reshape(batch * seq, W_A), ob, l, prm, tm=tm, final_norm=last)
        dp.append(s_fin)
        cp.append(ctail)

        qkv_s, gate_s, gb_s, ob_s, vb_s, cnew = _sample_in(xs, conv_flat, l, prm)
        oa_s, new_state = _delta_step(qkv_s, gb_s, gate_s, state_delta, new_state, l, prm,
                                      tb=SUBLANES if new_state is None else tb)
        xs = _out_ffn(xs, oa_s, ob_s, l, prm, tm=n_dec, final_norm=last)
        cs.append(cnew.reshape(n_dec, CONV_W - 1, QKV))
        vs.append(vb_s.reshape(n_dec, 1, W_B))

    return (xp.reshape(batch, seq, d), xs.reshape(n_dec, 1, d), jnp.stack(dp), jnp.stack(cp),
            new_state, jnp.stack(cs), jnp.stack(vs))
```

```python
import functools

import jax
import jax.numpy as jnp
from jax import lax
from jax.experimental import pallas as pl
from jax.experimental.pallas import tpu as pltpu

F32 = jnp.float32
BF16 = jnp.bfloat16

H_A = 4
DK = 128
DV = 128
W_A = H_A * DV
QKV = 2 * H_A * DK + H_A * DV
CONV_W = 4
DN_CHUNK = 64
H_B = 4
DH_B = 128
W_B = H_B * DH_B
GM_CHUNK = 128
EPS = 1e-6

LANES = 128
SUBLANES = 8
AB_W = LANES
OFF_GATE = QKV
OFF_U = OFF_GATE + W_A
OFF_V = OFF_U + W_B
OFF_AB = OFF_V + W_B
PROJ_PAD = OFF_AB + AB_W
CONV_BLK = H_A * DK

VMEM_LIMIT = 56 * 1024 * 1024
FF_BLOCK = 1024
PROJ_ROWS = 512
PREP_CHUNKS = 2


def _rmsnorm(x, g):
    return x * lax.rsqrt(jnp.mean(x * x, axis=-1, keepdims=True) + EPS) * g


def _silu(x):
    return x * jax.nn.sigmoid(x)


def _softplus(x):
    return jnp.maximum(x, 0.0) + jnp.log1p(jnp.exp(-jnp.abs(x)))


def _layer_spec(arr, layer):
    nd = arr.ndim - 1
    return pl.BlockSpec((None,) + arr.shape[1:], lambda *_: (layer,) + (0,) * nd,
                        pipeline_mode=pl.Buffered(1))


def _qkv_activation(y, j):
    a = _silu(y)
    if j >= 2:
        return a
    blocks = []
    for hh in range(H_A):
        blk = a[:, hh * DK:(hh + 1) * DK]
        inv = lax.rsqrt(jnp.sum(blk * blk, axis=-1, keepdims=True) + EPS)
        if j == 0:
            inv = inv * (DK ** -0.5)
        blocks.append(blk * inv)
    return jnp.concatenate(blocks, axis=1)


def _decay_beta(ab, alog, dtb):
    lane = lax.broadcasted_iota(jnp.int32, ab.shape, 1)
    gdec = -jnp.exp(alog) * _softplus(ab + dtb)
    beta = jax.nn.sigmoid(ab)
    return jnp.where(lane < H_A, gdec, beta)


def _prompt_in_kernel(x_ref, g_ref, win_ref, cw_ref, alog_ref, dtb_ref, vg_ref, ws_ref, bsb_ref,
                      qkv_ref, gate_ref, gb_ref, ob_ref, ctail_ref, carry_ref, *, tm, nt):
    i = pl.program_id(1)
    sm = PROJ_ROWS
    n_sub = tm // sm

    @pl.when(i == 0)
    def _():
        carry_ref[...] = jnp.zeros(carry_ref.shape, F32)

    sub = lax.broadcasted_iota(jnp.int32, (SUBLANES, CONV_BLK), 0)
    ri = lax.broadcasted_iota(jnp.int32, (GM_CHUNK, GM_CHUNK), 0)
    ci = lax.broadcasted_iota(jnp.int32, (GM_CHUNK, GM_CHUNK), 1)
    nc = sm // GM_CHUNK
    hs, tails, mixes = {}, {}, {}

    def proj(s, lo, hi):
        if s not in hs:
            hs[s] = _rmsnorm(x_ref[s * sm:(s + 1) * sm, :], g_ref[...]).astype(BF16)
        return jnp.dot(hs[s], win_ref[:, lo:hi], preferred_element_type=F32)

    def conv_act(s, j, pj):
        rows = slice(s * sm, (s + 1) * sm)
        cols = slice(j * CONV_BLK, (j + 1) * CONV_BLK)
        cw = cw_ref[:, cols]
        prev = carry_ref[:, cols] if s == 0 else tails[s - 1, j]

        def shift(x, x_prev, k):
            sh = pltpu.roll(x, k, axis=0)
            head = jnp.where(sub < k, pltpu.roll(x_prev, k, axis=0), sh[0:SUBLANES])
            return jnp.concatenate([head, sh[SUBLANES:]], axis=0)

        w0, w1, w2, w3 = (cw[k:k + 1, :] for k in range(CONV_W))
        pj1 = shift(pj, prev, 1)
        far = pj * w1 + pj1 * w0
        far_prev = prev * w1 + pltpu.roll(prev, 1, axis=0) * w0
        y = (pj * w3 + pj1 * w2) + shift(far, far_prev, 2)
        tails[s, j] = pj[sm - SUBLANES:sm, :]
        if s == n_sub - 1:
            carry_ref[:, cols] = tails[s, j]
            for k in range(CONV_W - 1):
                row = sm - (CONV_W - 1) + k
                ctail_ref[k, pl.ds(pl.program_id(0), 1), cols] = pj[row:row + 1, :]
        qkv_ref[rows, cols] = _qkv_activation(y, j)

    def gmlp_mix(s, pv):
        vb = _rmsnorm(pv, vg_ref[...]).astype(BF16)
        mixes[s] = []
        for hh in range(H_B):
            wm = jnp.where(ri >= ci, ws_ref[hh], 0.0).astype(BF16)
            cols = slice(hh * DH_B, (hh + 1) * DH_B)
            rhs = jnp.concatenate([vb[c * GM_CHUNK:(c + 1) * GM_CHUNK, cols] for c in range(nc)], axis=1)
            mixes[s].append(jnp.dot(wm, rhs, preferred_element_type=F32))

    def gmlp_gate(s, pu):
        for hh in range(H_B):
            cols = slice(hh * DH_B, (hh + 1) * DH_B)
            bias = bsb_ref[hh]
            for c in range(nc):
                src = slice(c * GM_CHUNK, (c + 1) * GM_CHUNK)
                dst = slice(s * sm + c * GM_CHUNK, s * sm + (c + 1) * GM_CHUNK)
                ob_ref[dst, cols] = pu[src, cols] * (mixes[s][hh][:, c * DH_B:(c + 1) * DH_B] + bias)

    def store_gb(s, pab):
        gb_ref[s * sm:(s + 1) * sm, :] = _decay_beta(pab, alog_ref[...], dtb_ref[...])

    def store_gate(s, pg):
        gate_ref[s * sm:(s + 1) * sm, :] = pg

    stages = []
    for s in range(n_sub):
        for j in range(QKV // CONV_BLK):
            stages.append((s, j * CONV_BLK, (j + 1) * CONV_BLK, functools.partial(conv_act, s, j)))
        stages.append((s, OFF_V, OFF_AB, functools.partial(gmlp_mix, s)))
        stages.append((s, OFF_U, OFF_V, functools.partial(gmlp_gate, s)))
        stages.append((s, OFF_AB, PROJ_PAD, functools.partial(store_gb, s)))
        stages.append((s, OFF_GATE, OFF_U, functools.partial(store_gate, s)))
    pending = proj(*stages[0][:3])
    for n, stage in enumerate(stages):
        ahead = proj(*stages[n + 1][:3]) if n + 1 < len(stages) else None
        stage[3](pending)
        pending = ahead


def _prompt_in(x2d, layer, prm, *, batch, seq, tm):
    nt = seq // tm
    rows = batch * seq
    d = x2d.shape[1]
    row_spec = lambda w: pl.BlockSpec((tm, w), lambda b, i: (b * nt + i, 0))
    names = ("gmix", "win", "cw", "alog", "dtb", "vg", "ws", "bsb")
    return pl.pallas_call(
        functools.partial(_prompt_in_kernel, tm=tm, nt=nt),
        grid=(batch, nt),
        in_specs=[row_spec(d)] + [_layer_spec(prm[n], layer) for n in names],
        out_specs=[row_spec(QKV), row_spec(W_A), row_spec(AB_W), row_spec(W_B),
                   pl.BlockSpec((CONV_W - 1, batch, QKV), lambda b, i: (0, 0, 0))],
        out_shape=[jax.ShapeDtypeStruct((rows, QKV), F32), jax.ShapeDtypeStruct((rows, W_A), F32),
                   jax.ShapeDtypeStruct((rows, AB_W), F32), jax.ShapeDtypeStruct((rows, W_B), F32),
                   jax.ShapeDtypeStruct((CONV_W - 1, batch, QKV), F32)],
        scratch_shapes=[pltpu.VMEM((SUBLANES, QKV), F32)],
        compiler_params=pltpu.CompilerParams(dimension_semantics=("arbitrary", "arbitrary"),
                                             vmem_limit_bytes=VMEM_LIMIT),
        name="prompt_in",
    )(x2d, *[prm[n] for n in names])


def _pair_blockdiag(m, left):
    zero = jnp.zeros_like(m)
    return jnp.concatenate([jnp.where(left, m, zero), jnp.where(left, zero, m)], axis=0)


def _unit_lower_inverses(lpairs, ri, cj, left):
    c = lpairs[0].shape[0]
    eye = (ri == cj).astype(F32)
    xs = None
    s = 1
    while s < c:
        mask = ((ri // (2 * s)) == (cj // (2 * s))) & (((ri // s) % 2) == 1) & (((cj // s) % 2) == 0)
        css = [jnp.where(mask, lp, 0.0) for lp in lpairs]
        if xs is None:
            xs = [eye - cs for cs in css]
        else:
            xbs = [x.astype(BF16) for x in xs]
            ys = [jnp.dot(xb, _pair_blockdiag(cs.astype(BF16), left), preferred_element_type=F32)
                  for xb, cs in zip(xbs, css)]
            zs = [jnp.dot(y.astype(BF16), _pair_blockdiag(xb, left), preferred_element_type=F32)
                  for y, xb in zip(ys, xbs)]
            xs = [x - z for x, z in zip(xs, zs)]
        s *= 2
    return xs


def _nt_dot(a, b):
    return lax.dot_general(a.astype(BF16), b.astype(BF16), (((1,), (1,)), ((), ())),
                           preferred_element_type=F32)


def _tn_dot(a, b):
    return lax.dot_general(a.astype(BF16), b.astype(BF16), (((0,), (0,)), ((), ())),
                           preferred_element_type=F32)


def _dot(a, b):
    return jnp.dot(a.astype(BF16), b.astype(BF16), preferred_element_type=F32)


def _delta_kernel(qkv_ref, gb_ref, gate_ref, og_ref, oa_ref, sout_ref,
                  s_ref, gcs_ref, u_ref, w_ref, qe_ref, qk_ref, kdt_ref, *, nb, tm, nt):
    i = pl.program_id(1)
    c = DN_CHUNK

    @pl.when(i == 0)
    def _():
        s_ref[...] = jnp.zeros(s_ref.shape, F32)

    assert 2 * c == LANES and H_A % 2 == 0
    ri = lax.broadcasted_iota(jnp.int32, (c, LANES), 0)
    lane = lax.broadcasted_iota(jnp.int32, (c, LANES), 1)
    left = lane < c
    cj = jnp.where(left, lane, lane - c)
    lower = ri >= cj
    strict = ri > cj
    og = og_ref[...]
    chains = [(b, hh) for b in range(nb) for hh in range(H_A)]
    pairs = [(b, p) for b in range(nb) for p in range(H_A // 2)]
    qcols = lambda hh: slice(hh * DK, (hh + 1) * DK)
    kcols = lambda hh: slice(H_A * DK + hh * DK, H_A * DK + (hh + 1) * DK)
    vcols = lambda hh: slice(2 * H_A * DK + hh * DV, 2 * H_A * DK + (hh + 1) * DV)
    pcols = lambda p: slice(p * LANES, (p + 1) * LANES)
    cat = jnp.concatenate

    def prepare(n, carry):
        chunk_ids = [n * PREP_CHUNKS + e for e in range(PREP_CHUNKS) for _ in range(nb)]
        units = [(b, pl.ds(pl.multiple_of((n * PREP_CHUNKS + e) * c, c), c))
                 for e in range(PREP_CHUNKS) for b in range(nb)]
        chains = [(ui, hh) for ui in range(len(units)) for hh in range(H_A)]
        pairs = [(ui, p) for ui in range(len(units)) for p in range(H_A // 2)]
        gbcs, gcss, gcsts = [], [], []
        for b, rows in units:
            gbc = gb_ref[b, rows, :]
            gcs = gbc
            sh = 1
            while sh < c:
                gcs = gcs + jnp.where(ri >= sh, pltpu.roll(gcs, sh, axis=0), 0.0)
                sh *= 2
            gcs_ref[b, rows, :] = gcs
            gbcs.append(gbc)
            gcss.append(gcs)
            gcsts.append(gcs.T)
        qs, ks, vs, kbs, betas, g_cols = {}, {}, {}, {}, {}, {}
        for ui, hh in chains:
            b, rows = units[ui]
            qs[ui, hh] = qkv_ref[b, rows, qcols(hh)]
            ks[ui, hh] = qkv_ref[b, rows, kcols(hh)]
            vs[ui, hh] = qkv_ref[b, rows, vcols(hh)]
            g_cols[ui, hh] = jnp.broadcast_to(gcss[ui][:, hh:hh + 1], (c, LANES))
            betas[ui, hh] = jnp.broadcast_to(gbcs[ui][:, H_A + hh:H_A + hh + 1], (c, LANES))
            kbs[ui, hh] = ks[ui, hh] * betas[ui, hh]
        decays, kqs = [], []
        for ui, p in pairs:
            b, rows = units[ui]
            h0, h1 = 2 * p, 2 * p + 1
            g_col = jnp.where(left, g_cols[ui, h0], g_cols[ui, h1])
            g_row = jnp.broadcast_to(cat([gcsts[ui][h0:h0 + 1, :], gcsts[ui][h1:h1 + 1, :]], axis=1), (c, LANES))
            decays.append(jnp.where(lower, jnp.exp(jnp.where(lower, g_col - g_row, 0.0)), 0.0))
            zero = jnp.zeros((c, DK), F32)
            lhs = cat([cat([kbs[ui, h0], kbs[ui, h1]], axis=1), cat([qs[ui, h0], qs[ui, h1]], axis=1)], axis=0)
            rhs = cat([cat([ks[ui, h0], zero], axis=1), cat([zero, ks[ui, h1]], axis=1)], axis=0)
            kqs.append(_nt_dot(lhs, rhs))
        lpairs = [jnp.where(strict, kq[:c] * decay, 0.0) for kq, decay in zip(kqs, decays)]
        ts = _unit_lower_inverses(lpairs, ri, cj, left)
        for j, (ui, p) in enumerate(pairs):
            b, rows = units[ui]
            h0, h1 = 2 * p, 2 * p + 1
            zero = jnp.zeros((c, DV), F32)
            vb0 = vs[ui, h0] * betas[ui, h0]
            vb1 = vs[ui, h1] * betas[ui, h1]
            kg0 = kbs[ui, h0] * jnp.exp(g_cols[ui, h0])
            kg1 = kbs[ui, h1] * jnp.exp(g_cols[ui, h1])
            rhs = cat([cat([vb0, zero, kg0, zero], axis=1), cat([zero, vb1, zero, kg1], axis=1)], axis=0)
            uw = _dot(ts[j], rhs)
            u_ref[b, rows, qcols(h0)] = uw[:, 0:DV]
            u_ref[b, rows, qcols(h1)] = uw[:, DV:2 * DV]
            w_ref[b, rows, qcols(h0)] = uw[:, 2 * DV:3 * DV].astype(BF16)
            w_ref[b, rows, qcols(h1)] = uw[:, 3 * DV:4 * DV].astype(BF16)
            qk_ref[b, rows, pcols(p)] = (kqs[j][c:] * decays[j]).astype(BF16)
        for ui, hh in chains:
            b, rows = units[ui]
            g_col = g_cols[ui, hh]
            qe_ref[b, rows, qcols(hh)] = (qs[ui, hh] * jnp.exp(g_col)).astype(BF16)
            k_dec = ks[ui, hh] * jnp.exp(g_col[c - 1:c, :] - g_col)
            kdt_ref[b, hh, pl.ds(pl.multiple_of(chunk_ids[ui] * DK, DK), DK), :] = k_dec.T.astype(BF16)
        return carry

    def recur(n, carry):
        rows = pl.ds(pl.multiple_of(n * c, c), c)
        gcss = [gcs_ref[b, rows, :] for b in range(nb)]
        s_olds = [s_ref[b, hh] for b, hh in chains]
        s_bfs = [s.astype(BF16) for s in s_olds]
        wqs = [jnp.dot(cat([w_ref[b, rows, qcols(hh)], qe_ref[b, rows, qcols(hh)]], axis=0), s_bf,
                       preferred_element_type=F32) for (b, hh), s_bf in zip(chains, s_bfs)]
        v_bfs = [(u_ref[b, rows, qcols(hh)] - wq[:c]).astype(BF16) for (b, hh), wq in zip(chains, wqs)]
        kdt_rows = pl.ds(pl.multiple_of(n * DK, DK), DK)
        upds = [jnp.dot(kdt_ref[b, hh, kdt_rows, :], v_bf, preferred_element_type=F32)
                for (b, hh), v_bf in zip(chains, v_bfs)]
        for j, (b, hh) in enumerate(chains):
            g_last = jnp.broadcast_to(gcss[b][c - 1:c, hh:hh + 1], (1, LANES))
            s_ref[b, hh] = s_olds[j] * jnp.exp(g_last) + upds[j]
        zero = jnp.zeros((c, DV), BF16)
        qkvs = []
        for j, (b, p) in enumerate(pairs):
            v0, v1 = v_bfs[2 * j], v_bfs[2 * j + 1]
            rhs = cat([cat([v0, zero], axis=1), cat([zero, v1], axis=1)], axis=0)
            both = jnp.dot(qk_ref[b, rows, pcols(p)], rhs, preferred_element_type=F32)
            qkvs += [both[:, :DV], both[:, DV:]]
        for j, (b, hh) in enumerate(chains):
            o = wqs[j][c:] + qkvs[j]
            oa_ref[b, rows, qcols(hh)] = _rmsnorm(o, og) * _silu(gate_ref[b, rows, qcols(hh)])
        return carry

    lax.fori_loop(0, tm // (c * PREP_CHUNKS), prepare, 0)
    lax.fori_loop(0, tm // c, recur, 0)

    @pl.when(i == nt - 1)
    def _():
        sout_ref[...] = s_ref[...]


def _delta_prompt(qkv, gb, gate, layer, prm, *, nb, tm):
    batch, seq, _ = qkv.shape
    nt = seq // tm
    blk = lambda w: pl.BlockSpec((nb, tm, w), lambda b, i: (b, i, 0))
    return pl.pallas_call(
        functools.partial(_delta_kernel, nb=nb, tm=tm, nt=nt),
        grid=(batch // nb, nt),
        in_specs=[blk(QKV), blk(AB_W), blk(W_A), _layer_spec(prm["og"], layer)],
        out_specs=[blk(W_A), pl.BlockSpec((nb, H_A, DK, DV), lambda b, i: (b, 0, 0, 0))],
        out_shape=[jax.ShapeDtypeStruct((batch, seq, W_A), F32),
                   jax.ShapeDtypeStruct((batch, H_A, DK, DV), F32)],
        scratch_shapes=[pltpu.VMEM((nb, H_A, DK, DV), F32),
                        pltpu.VMEM((nb, tm, AB_W), F32),
                        pltpu.VMEM((nb, tm, W_A), F32),
                        pltpu.VMEM((nb, tm, W_A), BF16),
                        pltpu.VMEM((nb, tm, W_A), BF16),
                        pltpu.VMEM((nb, tm, H_A // 2 * LANES), BF16),
                        pltpu.VMEM((nb, H_A, tm // DN_CHUNK * DK, DN_CHUNK), BF16)],
        compiler_params=pltpu.CompilerParams(dimension_semantics=("arbitrary", "arbitrary"),
                                             vmem_limit_bytes=VMEM_LIMIT),
        name="delta_prompt",
    )(qkv, gb, gate, prm["og"])


def _out_ffn_kernel(x_ref, oa_ref, ob_ref, wo_ref, gf_ref, wup_ref, wdn_ref, gl_ref, out_ref, *, final_norm):
    y = jnp.dot(oa_ref[...].astype(BF16), wo_ref[0:W_A, :], preferred_element_type=F32)
    y = y + jnp.dot(ob_ref[...].astype(BF16), wo_ref[W_A:W_A + W_B, :], preferred_element_type=F32)
    x1 = x_ref[...] + y
    h = _rmsnorm(x1, gf_ref[...]).astype(BF16)
    d_ff = wup_ref.shape[1]
    ffn = None
    for j in range(d_ff // FF_BLOCK):
        cols = slice(j * FF_BLOCK, (j + 1) * FF_BLOCK)
        a = jnp.dot(h, wup_ref[:, cols], preferred_element_type=F32)
        a = jnp.square(jnp.maximum(a, 0.0)).astype(BF16)
        part = jnp.dot(a, wdn_ref[cols, :], preferred_element_type=F32)
        ffn = part if ffn is None else ffn + part
    x2 = x1 + ffn
    if final_norm:
        x2 = _rmsnorm(x2, gl_ref[...])
    out_ref[...] = x2


def _out_ffn(x2d, oa, ob, layer, prm, *, tm, final_norm):
    rows, d = x2d.shape
    row_spec = lambda w: pl.BlockSpec((tm, w), lambda i: (i, 0))
    names = ("wo", "gffn", "wup", "wdn")
    return pl.pallas_call(
        functools.partial(_out_ffn_kernel, final_norm=final_norm),
        grid=(rows // tm,),
        in_specs=[row_spec(d), row_spec(W_A), row_spec(W_B)] + [_layer_spec(prm[n], layer) for n in names]
                 + [pl.BlockSpec((1, d), lambda i: (0, 0))],
        out_specs=row_spec(d),
        out_shape=jax.ShapeDtypeStruct((rows, d), F32),
        compiler_params=pltpu.CompilerParams(dimension_semantics=("arbitrary",),
                                             vmem_limit_bytes=VMEM_LIMIT),
        name="out_ffn",
    )(x2d, oa, ob, *[prm[n] for n in names], prm["gl"])


def _sample_in_kernel(x_ref, g_ref, win_ref, cw_ref, alog_ref, dtb_ref, vg_ref, ws0_ref, bs0_ref,
                      c0_ref, c1_ref, c2_ref, qkv_ref, gate_ref, gb_ref, ob_ref, vb_ref, cnew_ref):
    h = _rmsnorm(x_ref[...], g_ref[...]).astype(BF16)
    p = jnp.dot(h, win_ref[...], preferred_element_type=F32)
    pq = p[:, :QKV]
    cw = cw_ref[...]
    c1 = c1_ref[...]
    c2 = c2_ref[...]
    y = c0_ref[...] * cw[0:1, :] + c1 * cw[1:2, :] + c2 * cw[2:3, :] + pq * cw[3:4, :]
    cnew_ref[0] = c1
    cnew_ref[1] = c2
    cnew_ref[2] = pq
    for j in range(QKV // CONV_BLK):
        cols = slice(j * CONV_BLK, (j + 1) * CONV_BLK)
        qkv_ref[:, cols] = _qkv_activation(y[:, cols], j)
    gate_ref[...] = p[:, OFF_GATE:OFF_U]
    gb_ref[...] = _decay_beta(p[:, OFF_AB:PROJ_PAD], alog_ref[...], dtb_ref[...])
    vb = _rmsnorm(p[:, OFF_V:OFF_AB], vg_ref[...])
    vb_ref[...] = vb
    ob_ref[...] = p[:, OFF_U:OFF_V] * (vb * ws0_ref[...] + bs0_ref[...])


def _sample_in(x2d, conv_rows, layer, prm):
    n, d = x2d.shape
    full = lambda w: pl.BlockSpec((n, w), lambda i: (0, 0))
    conv_row = lambda j: pl.BlockSpec((None, None, n, QKV), lambda i: (layer, j, 0, 0))
    names = ("gmix", "win", "cw", "alog", "dtb", "vg", "ws0", "bs0")
    return pl.pallas_call(
        _sample_in_kernel,
        grid=(1,),
        in_specs=[full(d)] + [_layer_spec(prm[k], layer) for k in names] + [conv_row(j) for j in range(CONV_W - 1)],
        out_specs=[full(QKV), full(W_A), full(AB_W), full(W_B), full(W_B),
                   pl.BlockSpec((CONV_W - 1, n, QKV), lambda i: (0, 0, 0))],
        out_shape=[jax.ShapeDtypeStruct((n, QKV), F32), jax.ShapeDtypeStruct((n, W_A), F32),
                   jax.ShapeDtypeStruct((n, AB_W), F32), jax.ShapeDtypeStruct((n, W_B), F32),
                   jax.ShapeDtypeStruct((n, W_B), F32),
                   jax.ShapeDtypeStruct((CONV_W - 1, n, QKV), F32)],
        compiler_params=pltpu.CompilerParams(dimension_semantics=("arbitrary",),
                                             vmem_limit_bytes=VMEM_LIMIT),
        name="sample_in",
    )(x2d, *[prm[k] for k in names], conv_rows, conv_rows, conv_rows)


def _delta_step_kernel(*refs, tb, chained):
    if chained:
        qkv_ref, gb_ref, gate_ref, og_ref, s_ref, _, oa_ref, snew_ref = refs
    else:
        qkv_ref, gb_ref, gate_ref, og_ref, s_ref, oa_ref, snew_all_ref = refs
        snew_ref = snew_all_ref.at[0]
        snew_all_ref[1:] = jnp.zeros((snew_all_ref.shape[0] - 1,) + snew_all_ref.shape[1:], F32)
    og = og_ref[...]
    gb = gb_ref[...]
    for hh in range(H_A):
        q = qkv_ref[:, hh * DK:(hh + 1) * DK]
        k = qkv_ref[:, H_A * DK + hh * DK:H_A * DK + (hh + 1) * DK]
        v = qkv_ref[:, 2 * H_A * DK + hh * DV:2 * H_A * DK + (hh + 1) * DV]
        q_t = q.T
        k_t = k.T
        decay = jnp.exp(gb[:, hh:hh + 1])
        beta = gb[:, H_A + hh:H_A + hh + 1]
        o_rows = []
        for t in range(tb):
            s = s_ref[t, hh] * decay[t:t + 1, :]
            k_col = jnp.broadcast_to(k_t[:, t:t + 1], (DK, DV))
            q_col = jnp.broadcast_to(q_t[:, t:t + 1], (DK, DV))
            kv = jnp.sum(s * k_col, axis=0, keepdims=True)
            delta = (v[t:t + 1, :] - kv) * beta[t:t + 1, :]
            s = s + k_col * delta
            snew_ref[t, hh] = s
            o_rows.append(jnp.sum(s * q_col, axis=0, keepdims=True))
        o = jnp.concatenate(o_rows, axis=0)
        gate = gate_ref[:, hh * DV:(hh + 1) * DV]
        oa_ref[:, hh * DV:(hh + 1) * DV] = _rmsnorm(o, og) * _silu(gate)


def _delta_step(qkv, gb, gate, state_all, new_all, layer, prm, *, tb):
    n = qkv.shape[0]
    depth = state_all.shape[0]
    row_spec = lambda w: pl.BlockSpec((tb, w), lambda i: (i, 0))
    st_spec = pl.BlockSpec((None, tb, H_A, DK, DV), lambda i: (layer, i, 0, 0, 0))
    chained = new_all is not None
    in_specs = [row_spec(QKV), row_spec(AB_W), row_spec(W_A), _layer_spec(prm["og"], layer), st_spec]
    args = [qkv, gb, gate, prm["og"], state_all]
    if chained:
        in_specs.append(pl.BlockSpec(memory_space=pl.ANY))
        args.append(new_all)
        new_spec = st_spec
    else:
        assert layer == 0
        new_spec = pl.BlockSpec((depth, tb, H_A, DK, DV), lambda i: (0, i, 0, 0, 0))
    return pl.pallas_call(
        functools.partial(_delta_step_kernel, tb=tb, chained=chained),
        grid=(n // tb,),
        in_specs=in_specs,
        out_specs=[row_spec(W_A), new_spec],
        out_shape=[jax.ShapeDtypeStruct((n, W_A), F32), jax.ShapeDtypeStruct(state_all.shape, F32)],
        input_output_aliases={len(args) - 1: 1} if chained else {},
        compiler_params=pltpu.CompilerParams(dimension_semantics=("arbitrary",),
                                             vmem_limit_bytes=VMEM_LIMIT),
        name="delta_step",
    )(*args)


def _reorder_win_kernel(wt_ref, out_ref):
    off_a = QKV + W_A
    off_u = off_a + 2 * H_A
    src_of = lambda dst: dst if dst < off_a else dst + 2 * H_A
    for dst in range(0, OFF_AB, LANES):
        src = src_of(dst)
        out_ref[:, dst:dst + LANES] = wt_ref[src:src + LANES, :].T.astype(BF16)
    ab = jnp.concatenate([wt_ref[off_a:off_u, :], jnp.zeros((AB_W - 2 * H_A, wt_ref.shape[1]), F32)], axis=0)
    out_ref[:, OFF_AB:PROJ_PAD] = ab.T.astype(BF16)


def _reorder_win(wt, *, cols):
    depth, width, d = wt.shape
    return pl.pallas_call(
        _reorder_win_kernel,
        grid=(depth, d // cols),
        in_specs=[pl.BlockSpec((None, width, cols), lambda l, i: (l, 0, i))],
        out_specs=pl.BlockSpec((None, cols, PROJ_PAD), lambda l, i: (l, i, 0)),
        out_shape=jax.ShapeDtypeStruct((depth, d, PROJ_PAD), BF16),
        compiler_params=pltpu.CompilerParams(dimension_semantics=("arbitrary", "arbitrary"),
                                             vmem_limit_bytes=VMEM_LIMIT),
        name="reorder_win",
    )(wt)


def _pick_tile(seq, candidates):
    for tm in candidates:
        if seq % tm == 0:
            return tm
    raise ValueError(f"prompt length must be a multiple of {candidates[-1]}")


def kernel(x_prompt, x_sample, state_delta, state_conv, norm_mix_g, w_in, conv_w, A_log, dt_bias, o_norm_g,
           v_norm_g, w_s, b_s, w_o, norm_ffn_g, w_up, w_down, norm_f_g):
    batch, seq, d = x_prompt.shape
    n_dec, dec_seq, _ = x_sample.shape
    depth = w_in.shape[0]
    assert dec_seq == 1 and seq % GM_CHUNK == 0 and n_dec % SUBLANES == 0
    assert seq % PROJ_ROWS == 0
    tm = _pick_tile(seq, (512, 256, 128))
    ptm = _pick_tile(seq, (2 * PROJ_ROWS, PROJ_ROWS))
    tb = 2 * SUBLANES if n_dec % (2 * SUBLANES) == 0 else SUBLANES
    nb = next(n for n in (8, 4, 2, 1) if batch % n == 0)
    dtm = GM_CHUNK

    lane_pad = lambda v: jnp.pad(v.astype(F32), ((0, 0), (0, AB_W - v.shape[1]))).reshape(depth, 1, AB_W)
    prm = {
        "win": _reorder_win(jnp.swapaxes(w_in, 1, 2), cols=256 if d % 256 == 0 else LANES),
        "wo": w_o.astype(BF16),
        "wup": w_up.astype(BF16),
        "wdn": w_down.astype(BF16),
        "gmix": norm_mix_g.reshape(depth, 1, d),
        "gffn": norm_ffn_g.reshape(depth, 1, d),
        "gl": norm_f_g.reshape(1, d),
        "cw": conv_w,
        "alog": lane_pad(A_log),
        "dtb": lane_pad(dt_bias),
        "vg": v_norm_g.reshape(depth, 1, W_B),
        "og": o_norm_g.reshape(depth, 1, DV),
        "ws": w_s,
        "bsb": jnp.broadcast_to(b_s[:, :, :, None], b_s.shape + (DH_B,)).astype(F32),
        "ws0": jnp.repeat(w_s[:, :, 0, 0], DH_B, axis=-1).reshape(depth, 1, W_B),
        "bs0": jnp.repeat(b_s[:, :, 0], DH_B, axis=-1).reshape(depth, 1, W_B),
    }
    conv_rows = jnp.swapaxes(state_conv, 1, 2)

    xp = x_prompt.reshape(batch * seq, d)
    xs = x_sample.reshape(n_dec, d)
    dp, cp, cs, vs = [], [], [], []
    new_state = None
    for l in range(depth):
        last = l == depth - 1
        qkv, gate, gb, ob, ctail = _prompt_in(xp, l, prm, batch=batch, seq=seq, tm=ptm)
        oa, s_fin = _delta_prompt(qkv.reshape(batch, seq, QKV), gb.reshape(batch, seq, AB_W),
                                  gate.reshape(batch, seq, W_A), l, prm, nb=nb, tm=dtm)
        xp = _out_ffn(xp, oa.reshape(batch * seq, W_A), ob, l, prm, tm=tm, final_norm=last)
        dp.append(s_fin)
        cp.append(ctail)

        qkv_s, gate_s, gb_s, ob_s, vb_s, cnew = _sample_in(xs, conv_rows, l, prm)
        oa_s, new_state = _delta_step(qkv_s, gb_s, gate_s, state_delta, new_state, l, prm,
                                      tb=SUBLANES if new_state is None else tb)
        xs = _out_ffn(xs, oa_s, ob_s, l, prm, tm=n_dec, final_norm=last)
        cs.append(cnew)
        vs.append(vb_s.reshape(n_dec, 1, W_B))

    return (xp.reshape(batch, seq, d), xs.reshape(n_dec, 1, d), jnp.stack(dp), jnp.swapaxes(jnp.stack(cp), 1, 2),
            new_state, jnp.swapaxes(jnp.stack(cs), 1, 2), jnp.stack(vs))
```

```python
import functools

import jax
import jax.numpy as jnp
from jax import lax
from jax.experimental import pallas as pl
from jax.experimental.pallas import tpu as pltpu

F32 = jnp.float32
BF16 = jnp.bfloat16

H_A = 4
DK = 128
DV = 128
W_A = H_A * DV
QKV = 2 * H_A * DK + H_A * DV
CONV_W = 4
DN_CHUNK = 64
H_B = 4
DH_B = 128
W_B = H_B * DH_B
GM_CHUNK = 128
EPS = 1e-6

LANES = 128
SUBLANES = 8
AB_W = LANES
OFF_GATE = QKV
OFF_U = OFF_GATE + W_A
OFF_V = OFF_U + W_B
OFF_AB = OFF_V + W_B
PROJ_PAD = OFF_AB + AB_W
CONV_BLK = H_A * DK

VMEM_LIMIT = 56 * 1024 * 1024
FF_BLOCK = 1024
PROJ_ROWS = 512
PREP_CHUNKS = 2


def _rmsnorm(x, g):
    return x * lax.rsqrt(jnp.mean(x * x, axis=-1, keepdims=True) + EPS) * g


def _silu(x):
    return x * jax.nn.sigmoid(x)


def _softplus(x):
    return jnp.maximum(x, 0.0) + jnp.log1p(jnp.exp(-jnp.abs(x)))


def _layer_spec(arr, layer):
    nd = arr.ndim - 1
    return pl.BlockSpec((None,) + arr.shape[1:], lambda *_: (layer,) + (0,) * nd,
                        pipeline_mode=pl.Buffered(1))


def _qkv_activation(y, j):
    a = _silu(y)
    if j >= 2:
        return a
    blocks = []
    for hh in range(H_A):
        blk = a[:, hh * DK:(hh + 1) * DK]
        inv = lax.rsqrt(jnp.sum(blk * blk, axis=-1, keepdims=True) + EPS)
        if j == 0:
            inv = inv * (DK ** -0.5)
        blocks.append(blk * inv)
    return jnp.concatenate(blocks, axis=1)


def _decay_beta(ab, alog, dtb):
    lane = lax.broadcasted_iota(jnp.int32, ab.shape, 1)
    gdec = -jnp.exp(alog) * _softplus(ab + dtb)
    beta = jax.nn.sigmoid(ab)
    return jnp.where(lane < H_A, gdec, beta)


def _prompt_in_kernel(x_ref, g_ref, win_ref, cw_ref, alog_ref, dtb_ref, vg_ref, ws_ref, bsb_ref, *rest,
                      tm, nt, n_cast):
    cast_in, rest = rest[:n_cast], rest[n_cast:]
    (qkv_ref, gate_ref, gb_ref, ob_ref, ctail_ref), rest = rest[:5], rest[5:]
    cast_out, (carry_ref,) = rest[:n_cast], rest[n_cast:]
    for src, dst in zip(cast_in, cast_out):
        dst[...] = src[...].astype(BF16)
    i = pl.program_id(1)
    sm = PROJ_ROWS
    n_sub = tm // sm

    @pl.when(i == 0)
    def _():
        carry_ref[...] = jnp.zeros(carry_ref.shape, F32)

    sub = lax.broadcasted_iota(jnp.int32, (SUBLANES, CONV_BLK), 0)
    ri = lax.broadcasted_iota(jnp.int32, (GM_CHUNK, GM_CHUNK), 0)
    ci = lax.broadcasted_iota(jnp.int32, (GM_CHUNK, GM_CHUNK), 1)
    nc = sm // GM_CHUNK
    hs, tails, mixes = {}, {}, {}

    def proj(s, lo, hi):
        if s not in hs:
            hs[s] = _rmsnorm(x_ref[s * sm:(s + 1) * sm, :], g_ref[...]).astype(BF16)
        return jnp.dot(hs[s], win_ref[:, lo:hi], preferred_element_type=F32)

    def conv_act(s, j, pj):
        rows = slice(s * sm, (s + 1) * sm)
        cols = slice(j * CONV_BLK, (j + 1) * CONV_BLK)
        cw = cw_ref[:, cols]
        prev = carry_ref[:, cols] if s == 0 else tails[s - 1, j]

        def shift(x, x_prev, k):
            sh = pltpu.roll(x, k, axis=0)
            head = jnp.where(sub < k, pltpu.roll(x_prev, k, axis=0), sh[0:SUBLANES])
            return jnp.concatenate([head, sh[SUBLANES:]], axis=0)

        w0, w1, w2, w3 = (cw[k:k + 1, :] for k in range(CONV_W))
        pj1 = shift(pj, prev, 1)
        far = pj * w1 + pj1 * w0
        far_prev = prev * w1 + pltpu.roll(prev, 1, axis=0) * w0
        y = (pj * w3 + pj1 * w2) + shift(far, far_prev, 2)
        tails[s, j] = pj[sm - SUBLANES:sm, :]
        if s == n_sub - 1:
            carry_ref[:, cols] = tails[s, j]
            for k in range(CONV_W - 1):
                row = sm - (CONV_W - 1) + k
                ctail_ref[k, pl.ds(pl.program_id(0), 1), cols] = pj[row:row + 1, :]
        qkv_ref[rows, cols] = _qkv_activation(y, j)

    def gmlp_mix(s, pv):
        vb = _rmsnorm(pv, vg_ref[...]).astype(BF16)
        mixes[s] = []
        for hh in range(H_B):
            wm = jnp.where(ri >= ci, ws_ref[hh], 0.0).astype(BF16)
            cols = slice(hh * DH_B, (hh + 1) * DH_B)
            rhs = jnp.concatenate([vb[c * GM_CHUNK:(c + 1) * GM_CHUNK, cols] for c in range(nc)], axis=1)
            mixes[s].append(jnp.dot(wm, rhs, preferred_element_type=F32))

    def gmlp_gate(s, pu):
        for hh in range(H_B):
            cols = slice(hh * DH_B, (hh + 1) * DH_B)
            bias = bsb_ref[hh]
            for c in range(nc):
                src = slice(c * GM_CHUNK, (c + 1) * GM_CHUNK)
                dst = slice(s * sm + c * GM_CHUNK, s * sm + (c + 1) * GM_CHUNK)
                ob_ref[dst, cols] = pu[src, cols] * (mixes[s][hh][:, c * DH_B:(c + 1) * DH_B] + bias)

    def store_gb(s, pab):
        gb_ref[s * sm:(s + 1) * sm, :] = _decay_beta(pab, alog_ref[...], dtb_ref[...])

    def store_gate(s, pg):
        gate_ref[s * sm:(s + 1) * sm, :] = pg

    stages = []
    for s in range(n_sub):
        for j in range(QKV // CONV_BLK):
            stages.append((s, j * CONV_BLK, (j + 1) * CONV_BLK, functools.partial(conv_act, s, j)))
        stages.append((s, OFF_V, OFF_AB, functools.partial(gmlp_mix, s)))
        stages.append((s, OFF_U, OFF_V, functools.partial(gmlp_gate, s)))
        stages.append((s, OFF_AB, PROJ_PAD, functools.partial(store_gb, s)))
        stages.append((s, OFF_GATE, OFF_U, functools.partial(store_gate, s)))
    pending = proj(*stages[0][:3])
    for n, stage in enumerate(stages):
        ahead = proj(*stages[n + 1][:3]) if n + 1 < len(stages) else None
        stage[3](pending)
        pending = ahead


def _prompt_in(x2d, layer, prm, later_weights, *, batch, seq, tm):
    nt = seq // tm
    rows = batch * seq
    d = x2d.shape[1]
    steps = batch * nt
    row_spec = lambda w: pl.BlockSpec((tm, w), lambda b, i: (b * nt + i, 0))
    names = ("gmix", "win", "cw", "alog", "dtb", "vg", "ws", "bsb")
    slab_in, slab_out, slab_shape = [], [], []
    for w in later_weights:
        _, wr, wc = w.shape
        assert wr % (steps * 2 * SUBLANES) == 0
        slab_in.append(pl.BlockSpec((None, wr // steps, wc), lambda b, i: (layer, b * nt + i, 0)))
        slab_out.append(pl.BlockSpec((wr // steps, wc), lambda b, i: (b * nt + i, 0)))
        slab_shape.append(jax.ShapeDtypeStruct((wr, wc), BF16))
    outs = pl.pallas_call(
        functools.partial(_prompt_in_kernel, tm=tm, nt=nt, n_cast=len(later_weights)),
        grid=(batch, nt),
        in_specs=[row_spec(d)] + [_layer_spec(prm[n], layer) for n in names] + slab_in,
        out_specs=[row_spec(QKV), row_spec(W_A), row_spec(AB_W), row_spec(W_B),
                   pl.BlockSpec((CONV_W - 1, batch, QKV), lambda b, i: (0, 0, 0))] + slab_out,
        out_shape=[jax.ShapeDtypeStruct((rows, QKV), F32), jax.ShapeDtypeStruct((rows, W_A), F32),
                   jax.ShapeDtypeStruct((rows, AB_W), F32), jax.ShapeDtypeStruct((rows, W_B), F32),
                   jax.ShapeDtypeStruct((CONV_W - 1, batch, QKV), F32)] + slab_shape,
        scratch_shapes=[pltpu.VMEM((SUBLANES, QKV), F32)],
        compiler_params=pltpu.CompilerParams(dimension_semantics=("arbitrary", "arbitrary"),
                                             vmem_limit_bytes=VMEM_LIMIT),
        name="prompt_in",
    )(x2d, *[prm[n] for n in names], *later_weights)
    return outs[:5], outs[5:]


def _pair_blockdiag(m, left):
    zero = jnp.zeros_like(m)
    return jnp.concatenate([jnp.where(left, m, zero), jnp.where(left, zero, m)], axis=0)


def _unit_lower_inverses(lpairs, ri, cj, left):
    c = lpairs[0].shape[0]
    eye = (ri == cj).astype(F32)
    xs = None
    s = 1
    while s < c:
        mask = ((ri // (2 * s)) == (cj // (2 * s))) & (((ri // s) % 2) == 1) & (((cj // s) % 2) == 0)
        css = [jnp.where(mask, lp, 0.0) for lp in lpairs]
        if xs is None:
            xs = [eye - cs for cs in css]
        else:
            xbs = [x.astype(BF16) for x in xs]
            ys = [jnp.dot(xb, _pair_blockdiag(cs.astype(BF16), left), preferred_element_type=F32)
                  for xb, cs in zip(xbs, css)]
            zs = [jnp.dot(y.astype(BF16), _pair_blockdiag(xb, left), preferred_element_type=F32)
                  for y, xb in zip(ys, xbs)]
            xs = [x - z for x, z in zip(xs, zs)]
        s *= 2
    return xs


def _nt_dot(a, b):
    return lax.dot_general(a.astype(BF16), b.astype(BF16), (((1,), (1,)), ((), ())),
                           preferred_element_type=F32)


def _tn_dot(a, b):
    return lax.dot_general(a.astype(BF16), b.astype(BF16), (((0,), (0,)), ((), ())),
                           preferred_element_type=F32)


def _dot(a, b):
    return jnp.dot(a.astype(BF16), b.astype(BF16), preferred_element_type=F32)


def _delta_kernel(qkv_ref, gb_ref, gate_ref, og_ref, oa_ref, sout_ref,
                  s_ref, gcs_ref, u_ref, w_ref, qe_ref, qk_ref, kdt_ref, *, nb, tm, nt):
    i = pl.program_id(1)
    c = DN_CHUNK

    @pl.when(i == 0)
    def _():
        s_ref[...] = jnp.zeros(s_ref.shape, F32)

    assert 2 * c == LANES and H_A % 2 == 0
    ri = lax.broadcasted_iota(jnp.int32, (c, LANES), 0)
    lane = lax.broadcasted_iota(jnp.int32, (c, LANES), 1)
    left = lane < c
    cj = jnp.where(left, lane, lane - c)
    lower = ri >= cj
    strict = ri > cj
    og = og_ref[...]
    chains = [(b, hh) for b in range(nb) for hh in range(H_A)]
    pairs = [(b, p) for b in range(nb) for p in range(H_A // 2)]
    qcols = lambda hh: slice(hh * DK, (hh + 1) * DK)
    kcols = lambda hh: slice(H_A * DK + hh * DK, H_A * DK + (hh + 1) * DK)
    vcols = lambda hh: slice(2 * H_A * DK + hh * DV, 2 * H_A * DK + (hh + 1) * DV)
    pcols = lambda p: slice(p * LANES, (p + 1) * LANES)
    cat = jnp.concatenate

    def prepare(n, carry):
        chunk_ids = [n * PREP_CHUNKS + e for e in range(PREP_CHUNKS) for _ in range(nb)]
        units = [(b, pl.ds(pl.multiple_of((n * PREP_CHUNKS + e) * c, c), c))
                 for e in range(PREP_CHUNKS) for b in range(nb)]
        chains = [(ui, hh) for ui in range(len(units)) for hh in range(H_A)]
        pairs = [(ui, p) for ui in range(len(units)) for p in range(H_A // 2)]
        gbcs, gcss, gcsts = [], [], []
        for b, rows in units:
            gbc = gb_ref[b, rows, :]
            gcs = gbc
            sh = 1
            while sh < c:
                gcs = gcs + jnp.where(ri >= sh, pltpu.roll(gcs, sh, axis=0), 0.0)
                sh *= 2
            gcs_ref[b, rows, :] = gcs
            gbcs.append(gbc)
            gcss.append(gcs)
            gcsts.append(gcs.T)
        qs, ks, vs, kbs, betas, g_cols = {}, {}, {}, {}, {}, {}
        for ui, hh in chains:
            b, rows = units[ui]
            qs[ui, hh] = qkv_ref[b, rows, qcols(hh)]
            ks[ui, hh] = qkv_ref[b, rows, kcols(hh)]
            vs[ui, hh] = qkv_ref[b, rows, vcols(hh)]
            g_cols[ui, hh] = jnp.broadcast_to(gcss[ui][:, hh:hh + 1], (c, LANES))
            betas[ui, hh] = jnp.broadcast_to(gbcs[ui][:, H_A + hh:H_A + hh + 1], (c, LANES))
            kbs[ui, hh] = ks[ui, hh] * betas[ui, hh]
        decays, kqs = [], []
        for ui, p in pairs:
            b, rows = units[ui]
            h0, h1 = 2 * p, 2 * p + 1
            g_col = jnp.where(left, g_cols[ui, h0], g_cols[ui, h1])
            g_row = jnp.broadcast_to(cat([gcsts[ui][h0:h0 + 1, :], gcsts[ui][h1:h1 + 1, :]], axis=1), (c, LANES))
            decays.append(jnp.where(lower, jnp.exp(jnp.where(lower, g_col - g_row, 0.0)), 0.0))
            zero = jnp.zeros((c, DK), F32)
            lhs = cat([cat([kbs[ui, h0], kbs[ui, h1]], axis=1), cat([qs[ui, h0], qs[ui, h1]], axis=1)], axis=0)
            rhs = cat([cat([ks[ui, h0], zero], axis=1), cat([zero, ks[ui, h1]], axis=1)], axis=0)
            kqs.append(_nt_dot(lhs, rhs))
        lpairs = [jnp.where(strict, kq[:c] * decay, 0.0) for kq, decay in zip(kqs, decays)]
        ts = _unit_lower_inverses(lpairs, ri, cj, left)
        for j, (ui, p) in enumerate(pairs):
            b, rows = units[ui]
            h0, h1 = 2 * p, 2 * p + 1
            zero = jnp.zeros((c, DV), F32)
            vb0 = vs[ui, h0] * betas[ui, h0]
            vb1 = vs[ui, h1] * betas[ui, h1]
            kg0 = kbs[ui, h0] * jnp.exp(g_cols[ui, h0])
            kg1 = kbs[ui, h1] * jnp.exp(g_cols[ui, h1])
            rhs = cat([cat([vb0, zero, kg0, zero], axis=1), cat([zero, vb1, zero, kg1], axis=1)], axis=0)
            uw = _dot(ts[j], rhs)
            u_ref[b, rows, qcols(h0)] = uw[:, 0:DV]
            u_ref[b, rows, qcols(h1)] = uw[:, DV:2 * DV]
            w_ref[b, rows, qcols(h0)] = uw[:, 2 * DV:3 * DV].astype(BF16)
            w_ref[b, rows, qcols(h1)] = uw[:, 3 * DV:4 * DV].astype(BF16)
            qk_ref[b, rows, pcols(p)] = (kqs[j][c:] * decays[j]).astype(BF16)
        for ui, hh in chains:
            b, rows = units[ui]
            g_col = g_cols[ui, hh]
            qe_ref[b, rows, qcols(hh)] = (qs[ui, hh] * jnp.exp(g_col)).astype(BF16)
            k_dec = ks[ui, hh] * jnp.exp(g_col[c - 1:c, :] - g_col)
            kdt_ref[b, hh, pl.ds(pl.multiple_of(chunk_ids[ui] * DK, DK), DK), :] = k_dec.T.astype(BF16)
        return carry

    def recur(n, carry):
        rows = pl.ds(pl.multiple_of(n * c, c), c)
        gcss = [gcs_ref[b, rows, :] for b in range(nb)]
        s_olds = [s_ref[b, hh] for b, hh in chains]
        s_bfs = [s.astype(BF16) for s in s_olds]
        wqs = [jnp.dot(cat([w_ref[b, rows, qcols(hh)], qe_ref[b, rows, qcols(hh)]], axis=0), s_bf,
                       preferred_element_type=F32) for (b, hh), s_bf in zip(chains, s_bfs)]
        v_bfs = [(u_ref[b, rows, qcols(hh)] - wq[:c]).astype(BF16) for (b, hh), wq in zip(chains, wqs)]
        kdt_rows = pl.ds(pl.multiple_of(n * DK, DK), DK)
        upds = [jnp.dot(kdt_ref[b, hh, kdt_rows, :], v_bf, preferred_element_type=F32)
                for (b, hh), v_bf in zip(chains, v_bfs)]
        for j, (b, hh) in enumerate(chains):
            g_last = jnp.broadcast_to(gcss[b][c - 1:c, hh:hh + 1], (1, LANES))
            s_ref[b, hh] = s_olds[j] * jnp.exp(g_last) + upds[j]
        zero = jnp.zeros((c, DV), BF16)
        qkvs = []
        for j, (b, p) in enumerate(pairs):
            v0, v1 = v_bfs[2 * j], v_bfs[2 * j + 1]
            rhs = cat([cat([v0, zero], axis=1), cat([zero, v1], axis=1)], axis=0)
            both = jnp.dot(qk_ref[b, rows, pcols(p)], rhs, preferred_element_type=F32)
            qkvs += [both[:, :DV], both[:, DV:]]
        for j, (b, hh) in enumerate(chains):
            o = wqs[j][c:] + qkvs[j]
            oa_ref[b, rows, qcols(hh)] = _rmsnorm(o, og) * _silu(gate_ref[b, rows, qcols(hh)])
        return carry

    lax.fori_loop(0, tm // (c * PREP_CHUNKS), prepare, 0)
    lax.fori_loop(0, tm // c, recur, 0)

    @pl.when(i == nt - 1)
    def _():
        sout_ref[...] = s_ref[...]


def _delta_prompt(qkv, gb, gate, layer, prm, *, nb, tm):
    batch, seq, _ = qkv.shape
    nt = seq // tm
    blk = lambda w: pl.BlockSpec((nb, tm, w), lambda b, i: (b, i, 0))
    return pl.pallas_call(
        functools.partial(_delta_kernel, nb=nb, tm=tm, nt=nt),
        grid=(batch // nb, nt),
        in_specs=[blk(QKV), blk(AB_W), blk(W_A), _layer_spec(prm["og"], layer)],
        out_specs=[blk(W_A), pl.BlockSpec((nb, H_A, DK, DV), lambda b, i: (b, 0, 0, 0))],
        out_shape=[jax.ShapeDtypeStruct((batch, seq, W_A), F32),
                   jax.ShapeDtypeStruct((batch, H_A, DK, DV), F32)],
        scratch_shapes=[pltpu.VMEM((nb, H_A, DK, DV), F32),
                        pltpu.VMEM((nb, tm, AB_W), F32),
                        pltpu.VMEM((nb, tm, W_A), F32),
                        pltpu.VMEM((nb, tm, W_A), BF16),
                        pltpu.VMEM((nb, tm, W_A), BF16),
                        pltpu.VMEM((nb, tm, H_A // 2 * LANES), BF16),
                        pltpu.VMEM((nb, H_A, tm // DN_CHUNK * DK, DN_CHUNK), BF16)],
        compiler_params=pltpu.CompilerParams(dimension_semantics=("arbitrary", "arbitrary"),
                                             vmem_limit_bytes=VMEM_LIMIT),
        name="delta_prompt",
    )(qkv, gb, gate, prm["og"])


def _out_ffn_kernel(x_ref, oa_ref, ob_ref, wo_ref, gf_ref, wup_ref, wdn_ref, gl_ref, out_ref, *, final_norm):
    y = jnp.dot(oa_ref[...].astype(BF16), wo_ref[0:W_A, :], preferred_element_type=F32)
    y = y + jnp.dot(ob_ref[...].astype(BF16), wo_ref[W_A:W_A + W_B, :], preferred_element_type=F32)
    x1 = x_ref[...] + y
    h = _rmsnorm(x1, gf_ref[...]).astype(BF16)
    d_ff = wup_ref.shape[1]
    ffn = None
    for j in range(d_ff // FF_BLOCK):
        cols = slice(j * FF_BLOCK, (j + 1) * FF_BLOCK)
        a = jnp.dot(h, wup_ref[:, cols], preferred_element_type=F32)
        a = jnp.square(jnp.maximum(a, 0.0)).astype(BF16)
        part = jnp.dot(a, wdn_ref[cols, :], preferred_element_type=F32)
        ffn = part if ffn is None else ffn + part
    x2 = x1 + ffn
    if final_norm:
        x2 = _rmsnorm(x2, gl_ref[...])
    out_ref[...] = x2


def _out_ffn(x2d, oa, ob, layer, prm, weights, *, tm, final_norm):
    rows, d = x2d.shape
    wo, wup, wdn = weights
    row_spec = lambda w: pl.BlockSpec((tm, w), lambda i: (i, 0))
    once = lambda w: pl.BlockSpec(w.shape, lambda i: (0, 0), pipeline_mode=pl.Buffered(1))
    return pl.pallas_call(
        functools.partial(_out_ffn_kernel, final_norm=final_norm),
        grid=(rows // tm,),
        in_specs=[row_spec(d), row_spec(W_A), row_spec(W_B), once(wo), _layer_spec(prm["gffn"], layer),
                  once(wup), once(wdn), pl.BlockSpec((1, d), lambda i: (0, 0))],
        out_specs=row_spec(d),
        out_shape=jax.ShapeDtypeStruct((rows, d), F32),
        compiler_params=pltpu.CompilerParams(dimension_semantics=("arbitrary",),
                                             vmem_limit_bytes=VMEM_LIMIT),
        name="out_ffn",
    )(x2d, oa, ob, wo, prm["gffn"], wup, wdn, prm["gl"])


def _sample_in_kernel(x_ref, g_ref, win_ref, cw_ref, alog_ref, dtb_ref, vg_ref, ws0_ref, bs0_ref,
                      c0_ref, c1_ref, c2_ref, qkv_ref, gate_ref, gb_ref, ob_ref, vb_ref, cnew_ref):
    h = _rmsnorm(x_ref[...], g_ref[...]).astype(BF16)
    p = jnp.dot(h, win_ref[...], preferred_element_type=F32)
    pq = p[:, :QKV]
    cw = cw_ref[...]
    c1 = c1_ref[...]
    c2 = c2_ref[...]
    y = c0_ref[...] * cw[0:1, :] + c1 * cw[1:2, :] + c2 * cw[2:3, :] + pq * cw[3:4, :]
    cnew_ref[0] = c1
    cnew_ref[1] = c2
    cnew_ref[2] = pq
    for j in range(QKV // CONV_BLK):
        cols = slice(j * CONV_BLK, (j + 1) * CONV_BLK)
        qkv_ref[:, cols] = _qkv_activation(y[:, cols], j)
    gate_ref[...] = p[:, OFF_GATE:OFF_U]
    gb_ref[...] = _decay_beta(p[:, OFF_AB:PROJ_PAD], alog_ref[...], dtb_ref[...])
    vb = _rmsnorm(p[:, OFF_V:OFF_AB], vg_ref[...])
    vb_ref[...] = vb
    ob_ref[...] = p[:, OFF_U:OFF_V] * (vb * ws0_ref[...] + bs0_ref[...])


def _sample_in(x2d, conv_rows, layer, prm):
    n, d = x2d.shape
    full = lambda w: pl.BlockSpec((n, w), lambda i: (0, 0))
    conv_row = lambda j: pl.BlockSpec((None, None, n, QKV), lambda i: (layer, j, 0, 0))
    names = ("gmix", "win", "cw", "alog", "dtb", "vg", "ws0", "bs0")
    return pl.pallas_call(
        _sample_in_kernel,
        grid=(1,),
        in_specs=[full(d)] + [_layer_spec(prm[k], layer) for k in names] + [conv_row(j) for j in range(CONV_W - 1)],
        out_specs=[full(QKV), full(W_A), full(AB_W), full(W_B), full(W_B),
                   pl.BlockSpec((CONV_W - 1, n, QKV), lambda i: (0, 0, 0))],
        out_shape=[jax.ShapeDtypeStruct((n, QKV), F32), jax.ShapeDtypeStruct((n, W_A), F32),
                   jax.ShapeDtypeStruct((n, AB_W), F32), jax.ShapeDtypeStruct((n, W_B), F32),
                   jax.ShapeDtypeStruct((n, W_B), F32),
                   jax.ShapeDtypeStruct((CONV_W - 1, n, QKV), F32)],
        compiler_params=pltpu.CompilerParams(dimension_semantics=("arbitrary",),
                                             vmem_limit_bytes=VMEM_LIMIT),
        name="sample_in",
    )(x2d, *[prm[k] for k in names], conv_rows, conv_rows, conv_rows)


def _delta_step_kernel(*refs, tb, chained):
    if chained:
        qkv_ref, gb_ref, gate_ref, og_ref, s_ref, _, oa_ref, snew_ref = refs
    else:
        qkv_ref, gb_ref, gate_ref, og_ref, s_ref, oa_ref, snew_all_ref = refs
        snew_ref = snew_all_ref.at[0]
        snew_all_ref[1:] = jnp.zeros((snew_all_ref.shape[0] - 1,) + snew_all_ref.shape[1:], F32)
    og = og_ref[...]
    gb = gb_ref[...]
    spread = (lax.broadcasted_iota(jnp.int32, (tb, tb * DV), 1) // DV
              == lax.broadcasted_iota(jnp.int32, (tb, tb * DV), 0)).astype(BF16)
    for hh in range(H_A):
        q = qkv_ref[:, hh * DK:(hh + 1) * DK]
        k = qkv_ref[:, H_A * DK + hh * DK:H_A * DK + (hh + 1) * DK]
        v = qkv_ref[:, 2 * H_A * DK + hh * DV:2 * H_A * DK + (hh + 1) * DV]
        q_cols = jnp.dot(q.T.astype(BF16), spread, preferred_element_type=F32)
        k_cols = jnp.dot(k.T.astype(BF16), spread, preferred_element_type=F32)
        decay = jnp.exp(gb[:, hh:hh + 1])
        beta = gb[:, H_A + hh:H_A + hh + 1]
        o_rows = []
        for t in range(tb):
            s = s_ref[t, hh] * decay[t:t + 1, :]
            k_col = k_cols[:, t * DV:(t + 1) * DV]
            q_col = q_cols[:, t * DV:(t + 1) * DV]
            kv = jnp.sum(s * k_col, axis=0, keepdims=True)
            delta = (v[t:t + 1, :] - kv) * beta[t:t + 1, :]
            s = s + k_col * delta
            snew_ref[t, hh] = s
            o_rows.append(jnp.sum(s * q_col, axis=0, keepdims=True))
        o = jnp.concatenate(o_rows, axis=0)
        gate = gate_ref[:, hh * DV:(hh + 1) * DV]
        oa_ref[:, hh * DV:(hh + 1) * DV] = _rmsnorm(o, og) * _silu(gate)


def _delta_step(qkv, gb, gate, state_all, new_all, layer, prm, *, tb):
    n = qkv.shape[0]
    depth = state_all.shape[0]
    row_spec = lambda w: pl.BlockSpec((tb, w), lambda i: (i, 0))
    st_spec = pl.BlockSpec((None, tb, H_A, DK, DV), lambda i: (layer, i, 0, 0, 0))
    chained = new_all is not None
    in_specs = [row_spec(QKV), row_spec(AB_W), row_spec(W_A), _layer_spec(prm["og"], layer), st_spec]
    args = [qkv, gb, gate, prm["og"], state_all]
    if chained:
        in_specs.append(pl.BlockSpec(memory_space=pl.ANY))
        args.append(new_all)
        new_spec = st_spec
    else:
        assert layer == 0
        new_spec = pl.BlockSpec((depth, tb, H_A, DK, DV), lambda i: (0, i, 0, 0, 0))
    return pl.pallas_call(
        functools.partial(_delta_step_kernel, tb=tb, chained=chained),
        grid=(n // tb,),
        in_specs=in_specs,
        out_specs=[row_spec(W_A), new_spec],
        out_shape=[jax.ShapeDtypeStruct((n, W_A), F32), jax.ShapeDtypeStruct(state_all.shape, F32)],
        input_output_aliases={len(args) - 1: 1} if chained else {},
        compiler_params=pltpu.CompilerParams(dimension_semantics=("arbitrary",),
                                             vmem_limit_bytes=VMEM_LIMIT),
        name="delta_step",
    )(*args)


def _reorder_win_kernel(wt_ref, out_ref):
    off_a = QKV + W_A
    off_u = off_a + 2 * H_A
    src_of = lambda dst: dst if dst < off_a else dst + 2 * H_A
    for dst in range(0, OFF_AB, LANES):
        src = src_of(dst)
        out_ref[:, dst:dst + LANES] = wt_ref[src:src + LANES, :].T.astype(BF16)
    ab = jnp.concatenate([wt_ref[off_a:off_u, :], jnp.zeros((AB_W - 2 * H_A, wt_ref.shape[1]), F32)], axis=0)
    out_ref[:, OFF_AB:PROJ_PAD] = ab.T.astype(BF16)


def _reorder_win(wt, *, cols):
    depth, width, d = wt.shape
    return pl.pallas_call(
        _reorder_win_kernel,
        grid=(depth, d // cols),
        in_specs=[pl.BlockSpec((None, width, cols), lambda l, i: (l, 0, i))],
        out_specs=pl.BlockSpec((None, cols, PROJ_PAD), lambda l, i: (l, i, 0)),
        out_shape=jax.ShapeDtypeStruct((depth, d, PROJ_PAD), BF16),
        compiler_params=pltpu.CompilerParams(dimension_semantics=("arbitrary", "arbitrary"),
                                             vmem_limit_bytes=VMEM_LIMIT),
        name="reorder_win",
    )(wt)


def _pick_tile(seq, candidates):
    for tm in candidates:
        if seq % tm == 0:
            return tm
    raise ValueError(f"prompt length must be a multiple of {candidates[-1]}")


def kernel(x_prompt, x_sample, state_delta, state_conv, norm_mix_g, w_in, conv_w, A_log, dt_bias, o_norm_g,
           v_norm_g, w_s, b_s, w_o, norm_ffn_g, w_up, w_down, norm_f_g):
    batch, seq, d = x_prompt.shape
    n_dec, dec_seq, _ = x_sample.shape
    depth = w_in.shape[0]
    assert dec_seq == 1 and seq % GM_CHUNK == 0 and n_dec % SUBLANES == 0
    assert seq % PROJ_ROWS == 0
    tm = _pick_tile(seq, (512, 256, 128))
    ptm = _pick_tile(seq, (2 * PROJ_ROWS, PROJ_ROWS))
    tb = 2 * SUBLANES if n_dec % (2 * SUBLANES) == 0 else SUBLANES
    nb = next(n for n in (8, 4, 2, 1) if batch % n == 0)
    dtm = GM_CHUNK

    lane_pad = lambda v: jnp.pad(v.astype(F32), ((0, 0), (0, AB_W - v.shape[1]))).reshape(depth, 1, AB_W)
    prm = {
        "win": _reorder_win(jnp.swapaxes(w_in, 1, 2), cols=256 if d % 256 == 0 else LANES),
        "gmix": norm_mix_g.reshape(depth, 1, d),
        "gffn": norm_ffn_g.reshape(depth, 1, d),
        "gl": norm_f_g.reshape(1, d),
        "cw": conv_w,
        "alog": lane_pad(A_log),
        "dtb": lane_pad(dt_bias),
        "vg": v_norm_g.reshape(depth, 1, W_B),
        "og": o_norm_g.reshape(depth, 1, DV),
        "ws": w_s,
        "bsb": jnp.broadcast_to(b_s[:, :, :, None], b_s.shape + (DH_B,)).astype(F32),
        "ws0": jnp.repeat(w_s[:, :, 0, 0], DH_B, axis=-1).reshape(depth, 1, W_B),
        "bs0": jnp.repeat(b_s[:, :, 0], DH_B, axis=-1).reshape(depth, 1, W_B),
    }
    conv_rows = jnp.swapaxes(state_conv, 1, 2)

    xp = x_prompt.reshape(batch * seq, d)
    xs = x_sample.reshape(n_dec, d)
    dp, cp, cs, vs = [], [], [], []
    new_state = None
    for l in range(depth):
        last = l == depth - 1
        (qkv, gate, gb, ob, ctail), ffn_w = _prompt_in(xp, l, prm, (w_o, w_up, w_down),
                                                        batch=batch, seq=seq, tm=ptm)
        oa, s_fin = _delta_prompt(qkv.reshape(batch, seq, QKV), gb.reshape(batch, seq, AB_W),
                                  gate.reshape(batch, seq, W_A), l, prm, nb=nb, tm=dtm)
        xp = _out_ffn(xp, oa.reshape(batch * seq, W_A), ob, l, prm, ffn_w, tm=tm, final_norm=last)
        dp.append(s_fin)
        cp.append(ctail)

        qkv_s, gate_s, gb_s, ob_s, vb_s, cnew = _sample_in(xs, conv_rows, l, prm)
        oa_s, new_state = _delta_step(qkv_s, gb_s, gate_s, state_delta, new_state, l, prm,
                                      tb=SUBLANES if new_state is None else tb)
        xs = _out_ffn(xs, oa_s, ob_s, l, prm, ffn_w, tm=n_dec, final_norm=last)
        cs.append(cnew)
        vs.append(vb_s.reshape(n_dec, 1, W_B))

    return (xp.reshape(batch, seq, d), xs.reshape(n_dec, 1, d), jnp.stack(dp), jnp.swapaxes(jnp.stack(cp), 1, 2),
            new_state, jnp.swapaxes(jnp.stack(cs), 1, 2), jnp.stack(vs))
```

```python
import functools

import jax
import jax.numpy as jnp
from jax import lax
from jax.experimental import pallas as pl
from jax.experimental.pallas import tpu as pltpu

F32 = jnp.float32
BF16 = jnp.bfloat16

H_A = 4
DK = 128
DV = 128
W_A = H_A * DV
QKV = 2 * H_A * DK + H_A * DV
CONV_W = 4
DN_CHUNK = 64
H_B = 4
DH_B = 128
W_B = H_B * DH_B
GM_CHUNK = 128
EPS = 1e-6

LANES = 128
SUBLANES = 8
AB_W = LANES
OFF_GATE = QKV
OFF_U = OFF_GATE + W_A
OFF_V = OFF_U + W_B
OFF_AB = OFF_V + W_B
PROJ_PAD = OFF_AB + AB_W
CONV_BLK = H_A * DK

VMEM_LIMIT = 56 * 1024 * 1024
FF_BLOCK = 1024
PROJ_ROWS = 512
PREP_CHUNKS = 2


def _rmsnorm(x, g):
    return x * lax.rsqrt(jnp.mean(x * x, axis=-1, keepdims=True) + EPS) * g


def _silu(x):
    return x * jax.nn.sigmoid(x)


def _softplus(x):
    return jnp.maximum(x, 0.0) + jnp.log1p(jnp.exp(-jnp.abs(x)))


def _layer_spec(arr, layer):
    nd = arr.ndim - 1
    return pl.BlockSpec((None,) + arr.shape[1:], lambda *_: (layer,) + (0,) * nd,
                        pipeline_mode=pl.Buffered(1))


def _qkv_activation(y, j):
    a = _silu(y)
    if j >= 2:
        return a
    blocks = []
    for hh in range(H_A):
        blk = a[:, hh * DK:(hh + 1) * DK]
        inv = lax.rsqrt(jnp.sum(blk * blk, axis=-1, keepdims=True) + EPS)
        if j == 0:
            inv = inv * (DK ** -0.5)
        blocks.append(blk * inv)
    return jnp.concatenate(blocks, axis=1)


def _decay_beta(ab, alog, dtb):
    lane = lax.broadcasted_iota(jnp.int32, ab.shape, 1)
    gdec = -jnp.exp(alog) * _softplus(ab + dtb)
    beta = jax.nn.sigmoid(ab)
    return jnp.where(lane < H_A, gdec, beta)


def _prompt_in_kernel(x_ref, g_ref, win_ref, cw_ref, alog_ref, dtb_ref, vg_ref, ws_ref, bsb_ref, *rest,
                      tm, nt, n_cast):
    cast_in, rest = rest[:n_cast], rest[n_cast:]
    (qkv_ref, gate_ref, gb_ref, ob_ref, ctail_ref), rest = rest[:5], rest[5:]
    cast_out, (carry_ref,) = rest[:n_cast], rest[n_cast:]
    for src, dst in zip(cast_in, cast_out):
        dst[...] = src[...].astype(BF16)
    i = pl.program_id(1)
    sm = PROJ_ROWS
    n_sub = tm // sm

    @pl.when(i == 0)
    def _():
        carry_ref[...] = jnp.zeros(carry_ref.shape, F32)

    sub = lax.broadcasted_iota(jnp.int32, (SUBLANES, CONV_BLK), 0)
    ri = lax.broadcasted_iota(jnp.int32, (GM_CHUNK, GM_CHUNK), 0)
    ci = lax.broadcasted_iota(jnp.int32, (GM_CHUNK, GM_CHUNK), 1)
    nc = sm // GM_CHUNK
    hs, tails, mixes = {}, {}, {}

    def proj(s, lo, hi):
        if s not in hs:
            hs[s] = _rmsnorm(x_ref[s * sm:(s + 1) * sm, :], g_ref[...]).astype(BF16)
        return jnp.dot(hs[s], win_ref[:, lo:hi], preferred_element_type=F32)

    def conv_act(s, j, pj):
        rows = slice(s * sm, (s + 1) * sm)
        cols = slice(j * CONV_BLK, (j + 1) * CONV_BLK)
        cw = cw_ref[:, cols]
        prev = carry_ref[:, cols] if s == 0 else tails[s - 1, j]

        def shift(x, x_prev, k):
            sh = pltpu.roll(x, k, axis=0)
            head = jnp.where(sub < k, pltpu.roll(x_prev, k, axis=0), sh[0:SUBLANES])
            return jnp.concatenate([head, sh[SUBLANES:]], axis=0)

        w0, w1, w2, w3 = (cw[k:k + 1, :] for k in range(CONV_W))
        pj1 = shift(pj, prev, 1)
        far = pj * w1 + pj1 * w0
        far_prev = prev * w1 + pltpu.roll(prev, 1, axis=0) * w0
        y = (pj * w3 + pj1 * w2) + shift(far, far_prev, 2)
        tails[s, j] = pj[sm - SUBLANES:sm, :]
        if s == n_sub - 1:
            carry_ref[:, cols] = tails[s, j]
            for k in range(CONV_W - 1):
                row = sm - (CONV_W - 1) + k
                ctail_ref[k, pl.ds(pl.program_id(0), 1), cols] = pj[row:row + 1, :]
        qkv_ref[rows, cols] = _qkv_activation(y, j)

    def gmlp_mix(s, pv):
        vb = _rmsnorm(pv, vg_ref[...]).astype(BF16)
        mixes[s] = []
        for hh in range(H_B):
            wm = jnp.where(ri >= ci, ws_ref[hh], 0.0).astype(BF16)
            cols = slice(hh * DH_B, (hh + 1) * DH_B)
            rhs = jnp.concatenate([vb[c * GM_CHUNK:(c + 1) * GM_CHUNK, cols] for c in range(nc)], axis=1)
            mixes[s].append(jnp.dot(wm, rhs, preferred_element_type=F32))

    def gmlp_gate(s, pu):
        for hh in range(H_B):
            cols = slice(hh * DH_B, (hh + 1) * DH_B)
            bias = bsb_ref[hh]
            for c in range(nc):
                src = slice(c * GM_CHUNK, (c + 1) * GM_CHUNK)
                dst = slice(s * sm + c * GM_CHUNK, s * sm + (c + 1) * GM_CHUNK)
                ob_ref[dst, cols] = pu[src, cols] * (mixes[s][hh][:, c * DH_B:(c + 1) * DH_B] + bias)

    def store_gb(s, pab):
        gb_ref[s * sm:(s + 1) * sm, :] = _decay_beta(pab, alog_ref[...], dtb_ref[...])

    def store_gate(s, pg):
        gate_ref[s * sm:(s + 1) * sm, :] = pg

    stages = []
    for s in range(n_sub):
        for j in range(QKV // CONV_BLK):
            stages.append((s, j * CONV_BLK, (j + 1) * CONV_BLK, functools.partial(conv_act, s, j)))
        stages.append((s, OFF_V, OFF_AB, functools.partial(gmlp_mix, s)))
        stages.append((s, OFF_U, OFF_V, functools.partial(gmlp_gate, s)))
        stages.append((s, OFF_AB, PROJ_PAD, functools.partial(store_gb, s)))
        stages.append((s, OFF_GATE, OFF_U, functools.partial(store_gate, s)))
    pending = proj(*stages[0][:3])
    for n, stage in enumerate(stages):
        ahead = proj(*stages[n + 1][:3]) if n + 1 < len(stages) else None
        stage[3](pending)
        pending = ahead


def _prompt_in(x2d, layer, prm, later_weights, *, batch, seq, tm):
    nt = seq // tm
    rows = batch * seq
    d = x2d.shape[1]
    steps = batch * nt
    row_spec = lambda w: pl.BlockSpec((tm, w), lambda b, i: (b * nt + i, 0))
    names = ("gmix", "win", "cw", "alog", "dtb", "vg", "ws", "bsb")
    slab_in, slab_out, slab_shape = [], [], []
    for w in later_weights:
        _, wr, wc = w.shape
        assert wr % (steps * 2 * SUBLANES) == 0
        slab_in.append(pl.BlockSpec((None, wr // steps, wc), lambda b, i: (layer, b * nt + i, 0)))
        slab_out.append(pl.BlockSpec((wr // steps, wc), lambda b, i: (b * nt + i, 0)))
        slab_shape.append(jax.ShapeDtypeStruct((wr, wc), BF16))
    outs = pl.pallas_call(
        functools.partial(_prompt_in_kernel, tm=tm, nt=nt, n_cast=len(later_weights)),
        grid=(batch, nt),
        in_specs=[row_spec(d)] + [_layer_spec(prm[n], layer) for n in names] + slab_in,
        out_specs=[row_spec(QKV), row_spec(W_A), row_spec(AB_W), row_spec(W_B),
                   pl.BlockSpec((CONV_W - 1, batch, QKV), lambda b, i: (0, 0, 0))] + slab_out,
        out_shape=[jax.ShapeDtypeStruct((rows, QKV), F32), jax.ShapeDtypeStruct((rows, W_A), F32),
                   jax.ShapeDtypeStruct((rows, AB_W), F32), jax.ShapeDtypeStruct((rows, W_B), F32),
                   jax.ShapeDtypeStruct((CONV_W - 1, batch, QKV), F32)] + slab_shape,
        scratch_shapes=[pltpu.VMEM((SUBLANES, QKV), F32)],
        compiler_params=pltpu.CompilerParams(dimension_semantics=("arbitrary", "arbitrary"),
                                             vmem_limit_bytes=VMEM_LIMIT),
        name="prompt_in",
    )(x2d, *[prm[n] for n in names], *later_weights)
    return outs[:5], outs[5:]


def _pair_blockdiag(m, left):
    zero = jnp.zeros_like(m)
    return jnp.concatenate([jnp.where(left, m, zero), jnp.where(left, zero, m)], axis=0)


def _unit_lower_inverses(lpairs, ri, cj, left):
    c = lpairs[0].shape[0]
    eye = (ri == cj).astype(F32)
    xs = None
    s = 1
    while s < c:
        mask = ((ri // (2 * s)) == (cj // (2 * s))) & (((ri // s) % 2) == 1) & (((cj // s) % 2) == 0)
        css = [jnp.where(mask, lp, 0.0) for lp in lpairs]
        if xs is None:
            xs = [eye - cs for cs in css]
        else:
            xbs = [x.astype(BF16) for x in xs]
            ys = [jnp.dot(xb, _pair_blockdiag(cs.astype(BF16), left), preferred_element_type=F32)
                  for xb, cs in zip(xbs, css)]
            zs = [jnp.dot(y.astype(BF16), _pair_blockdiag(xb, left), preferred_element_type=F32)
                  for y, xb in zip(ys, xbs)]
            xs = [x - z for x, z in zip(xs, zs)]
        s *= 2
    return xs


def _nt_dot(a, b):
    return lax.dot_general(a.astype(BF16), b.astype(BF16), (((1,), (1,)), ((), ())),
                           preferred_element_type=F32)


def _tn_dot(a, b):
    return lax.dot_general(a.astype(BF16), b.astype(BF16), (((0,), (0,)), ((), ())),
                           preferred_element_type=F32)


def _dot(a, b):
    return jnp.dot(a.astype(BF16), b.astype(BF16), preferred_element_type=F32)


def _delta_kernel(qkv_ref, gb_ref, gate_ref, og_ref, oa_ref, sout_ref,
                  s_ref, gcs_ref, u_ref, w_ref, qe_ref, qk_ref, kdt_ref, *, nb, tm, nt):
    i = pl.program_id(1)
    c = DN_CHUNK

    @pl.when(i == 0)
    def _():
        s_ref[...] = jnp.zeros(s_ref.shape, F32)

    assert 2 * c == LANES and H_A % 2 == 0
    ri = lax.broadcasted_iota(jnp.int32, (c, LANES), 0)
    lane = lax.broadcasted_iota(jnp.int32, (c, LANES), 1)
    left = lane < c
    cj = jnp.where(left, lane, lane - c)
    lower = ri >= cj
    strict = ri > cj
    og = og_ref[...]
    chains = [(b, hh) for b in range(nb) for hh in range(H_A)]
    pairs = [(b, p) for b in range(nb) for p in range(H_A // 2)]
    qcols = lambda hh: slice(hh * DK, (hh + 1) * DK)
    kcols = lambda hh: slice(H_A * DK + hh * DK, H_A * DK + (hh + 1) * DK)
    vcols = lambda hh: slice(2 * H_A * DK + hh * DV, 2 * H_A * DK + (hh + 1) * DV)
    pcols = lambda p: slice(p * LANES, (p + 1) * LANES)
    cat = jnp.concatenate

    def prepare(n, carry):
        chunk_ids = [n * PREP_CHUNKS + e for e in range(PREP_CHUNKS) for _ in range(nb)]
        units = [(b, pl.ds(pl.multiple_of((n * PREP_CHUNKS + e) * c, c), c))
                 for e in range(PREP_CHUNKS) for b in range(nb)]
        chains = [(ui, hh) for ui in range(len(units)) for hh in range(H_A)]
        pairs = [(ui, p) for ui in range(len(units)) for p in range(H_A // 2)]
        gbcs, gcss, gcsts = [], [], []
        for b, rows in units:
            gbc = gb_ref[b, rows, :]
            gcs = gbc
            sh = 1
            while sh < c:
                gcs = gcs + jnp.where(ri >= sh, pltpu.roll(gcs, sh, axis=0), 0.0)
                sh *= 2
            gcs_ref[b, rows, :] = gcs
            gbcs.append(gbc)
            gcss.append(gcs)
            gcsts.append(gcs.T)
        qs, ks, vs, kbs, betas, g_cols = {}, {}, {}, {}, {}, {}
        for ui, hh in chains:
            b, rows = units[ui]
            qs[ui, hh] = qkv_ref[b, rows, qcols(hh)]
            ks[ui, hh] = qkv_ref[b, rows, kcols(hh)]
            vs[ui, hh] = qkv_ref[b, rows, vcols(hh)]
            g_cols[ui, hh] = jnp.broadcast_to(gcss[ui][:, hh:hh + 1], (c, LANES))
            betas[ui, hh] = jnp.broadcast_to(gbcs[ui][:, H_A + hh:H_A + hh + 1], (c, LANES))
            kbs[ui, hh] = ks[ui, hh] * betas[ui, hh]
        decays, kqs = [], []
        for ui, p in pairs:
            b, rows = units[ui]
            h0, h1 = 2 * p, 2 * p + 1
            g_col = jnp.where(left, g_cols[ui, h0], g_cols[ui, h1])
            g_row = jnp.broadcast_to(cat([gcsts[ui][h0:h0 + 1, :], gcsts[ui][h1:h1 + 1, :]], axis=1), (c, LANES))
            decays.append(jnp.where(lower, jnp.exp(jnp.where(lower, g_col - g_row, 0.0)), 0.0))
            zero = jnp.zeros((c, DK), F32)
            lhs = cat([cat([kbs[ui, h0], kbs[ui, h1]], axis=1), cat([qs[ui, h0], qs[ui, h1]], axis=1)], axis=0)
            rhs = cat([cat([ks[ui, h0], zero], axis=1), cat([zero, ks[ui, h1]], axis=1)], axis=0)
            kqs.append(_nt_dot(lhs, rhs))
        lpairs = [jnp.where(strict, kq[:c] * decay, 0.0) for kq, decay in zip(kqs, decays)]
        ts = _unit_lower_inverses(lpairs, ri, cj, left)
        for j, (ui, p) in enumerate(pairs):
            b, rows = units[ui]
            h0, h1 = 2 * p, 2 * p + 1
            zero = jnp.zeros((c, DV), F32)
            vb0 = vs[ui, h0] * betas[ui, h0]
            vb1 = vs[ui, h1] * betas[ui, h1]
            kg0 = kbs[ui, h0] * jnp.exp(g_cols[ui, h0])
            kg1 = kbs[ui, h1] * jnp.exp(g_cols[ui, h1])
            rhs = cat([cat([vb0, zero, kg0, zero], axis=1), cat([zero, vb1, zero, kg1], axis=1)], axis=0)
            uw = _dot(ts[j], rhs)
            u_ref[b, rows, qcols(h0)] = uw[:, 0:DV]
            u_ref[b, rows, qcols(h1)] = uw[:, DV:2 * DV]
            w_ref[b, rows, qcols(h0)] = uw[:, 2 * DV:3 * DV].astype(BF16)
            w_ref[b, rows, qcols(h1)] = uw[:, 3 * DV:4 * DV].astype(BF16)
            qk_ref[b, rows, pcols(p)] = (kqs[j][c:] * decays[j]).astype(BF16)
        for ui, hh in chains:
            b, rows = units[ui]
            g_col = g_cols[ui, hh]
            qe_ref[b, rows, qcols(hh)] = (qs[ui, hh] * jnp.exp(g_col)).astype(BF16)
            k_dec = ks[ui, hh] * jnp.exp(g_col[c - 1:c, :] - g_col)
            kdt_ref[b, hh, pl.ds(pl.multiple_of(chunk_ids[ui] * DK, DK), DK), :] = k_dec.T.astype(BF16)
        return carry

    def recur(n, carry):
        rows = pl.ds(pl.multiple_of(n * c, c), c)
        gcss = [gcs_ref[b, rows, :] for b in range(nb)]
        s_olds = [s_ref[b, hh] for b, hh in chains]
        s_bfs = [s.astype(BF16) for s in s_olds]
        wqs = [jnp.dot(cat([w_ref[b, rows, qcols(hh)], qe_ref[b, rows, qcols(hh)]], axis=0), s_bf,
                       preferred_element_type=F32) for (b, hh), s_bf in zip(chains, s_bfs)]
        v_bfs = [(u_ref[b, rows, qcols(hh)] - wq[:c]).astype(BF16) for (b, hh), wq in zip(chains, wqs)]
        kdt_rows = pl.ds(pl.multiple_of(n * DK, DK), DK)
        upds = [jnp.dot(kdt_ref[b, hh, kdt_rows, :], v_bf, preferred_element_type=F32)
                for (b, hh), v_bf in zip(chains, v_bfs)]
        for j, (b, hh) in enumerate(chains):
            g_last = jnp.broadcast_to(gcss[b][c - 1:c, hh:hh + 1], (1, LANES))
            s_ref[b, hh] = s_olds[j] * jnp.exp(g_last) + upds[j]
        zero = jnp.zeros((c, DV), BF16)
        qkvs = []
        for j, (b, p) in enumerate(pairs):
            v0, v1 = v_bfs[2 * j], v_bfs[2 * j + 1]
            rhs = cat([cat([v0, zero], axis=1), cat([zero, v1], axis=1)], axis=0)
            both = jnp.dot(qk_ref[b, rows, pcols(p)], rhs, preferred_element_type=F32)
            qkvs += [both[:, :DV], both[:, DV:]]
        for j, (b, hh) in enumerate(chains):
            o = wqs[j][c:] + qkvs[j]
            oa_ref[b, rows, qcols(hh)] = _rmsnorm(o, og) * _silu(gate_ref[b, rows, qcols(hh)])
        return carry

    lax.fori_loop(0, tm // (c * PREP_CHUNKS), prepare, 0)
    lax.fori_loop(0, tm // c, recur, 0)

    @pl.when(i == nt - 1)
    def _():
        sout_ref[...] = s_ref[...]


def _delta_prompt(qkv, gb, gate, layer, prm, *, nb, tm):
    batch, seq, _ = qkv.shape
    nt = seq // tm
    blk = lambda w: pl.BlockSpec((nb, tm, w), lambda b, i: (b, i, 0))
    return pl.pallas_call(
        functools.partial(_delta_kernel, nb=nb, tm=tm, nt=nt),
        grid=(batch // nb, nt),
        in_specs=[blk(QKV), blk(AB_W), blk(W_A), _layer_spec(prm["og"], layer)],
        out_specs=[blk(W_A), pl.BlockSpec((nb, H_A, DK, DV), lambda b, i: (b, 0, 0, 0))],
        out_shape=[jax.ShapeDtypeStruct((batch, seq, W_A), F32),
                   jax.ShapeDtypeStruct((batch, H_A, DK, DV), F32)],
        scratch_shapes=[pltpu.VMEM((nb, H_A, DK, DV), F32),
                        pltpu.VMEM((nb, tm, AB_W), F32),
                        pltpu.VMEM((nb, tm, W_A), F32),
                        pltpu.VMEM((nb, tm, W_A), BF16),
                        pltpu.VMEM((nb, tm, W_A), BF16),
                        pltpu.VMEM((nb, tm, H_A // 2 * LANES), BF16),
                        pltpu.VMEM((nb, H_A, tm // DN_CHUNK * DK, DN_CHUNK), BF16)],
        compiler_params=pltpu.CompilerParams(dimension_semantics=("arbitrary", "arbitrary"),
                                             vmem_limit_bytes=VMEM_LIMIT),
        name="delta_prompt",
    )(qkv, gb, gate, prm["og"])


def _out_ffn_kernel(x_ref, oa_ref, ob_ref, wo_ref, gf_ref, wup_ref, wdn_ref, gl_ref, out_ref, *, final_norm):
    y = jnp.dot(oa_ref[...].astype(BF16), wo_ref[0:W_A, :], preferred_element_type=F32)
    y = y + jnp.dot(ob_ref[...].astype(BF16), wo_ref[W_A:W_A + W_B, :], preferred_element_type=F32)
    x1 = x_ref[...] + y
    h = _rmsnorm(x1, gf_ref[...]).astype(BF16)
    d_ff = wup_ref.shape[1]
    ffn = None
    for j in range(d_ff // FF_BLOCK):
        cols = slice(j * FF_BLOCK, (j + 1) * FF_BLOCK)
        a = jnp.dot(h, wup_ref[:, cols], preferred_element_type=F32)
        a = jnp.square(jnp.maximum(a, 0.0)).astype(BF16)
        part = jnp.dot(a, wdn_ref[cols, :], preferred_element_type=F32)
        ffn = part if ffn is None else ffn + part
    x2 = x1 + ffn
    if final_norm:
        x2 = _rmsnorm(x2, gl_ref[...])
    out_ref[...] = x2


def _out_ffn(x2d, oa, ob, layer, prm, weights, *, tm, final_norm):
    rows, d = x2d.shape
    wo, wup, wdn = weights
    row_spec = lambda w: pl.BlockSpec((tm, w), lambda i: (i, 0))
    once = lambda w: pl.BlockSpec(w.shape, lambda i: (0, 0), pipeline_mode=pl.Buffered(1))
    return pl.pallas_call(
        functools.partial(_out_ffn_kernel, final_norm=final_norm),
        grid=(rows // tm,),
        in_specs=[row_spec(d), row_spec(W_A), row_spec(W_B), once(wo), _layer_spec(prm["gffn"], layer),
                  once(wup), once(wdn), pl.BlockSpec((1, d), lambda i: (0, 0))],
        out_specs=row_spec(d),
        out_shape=jax.ShapeDtypeStruct((rows, d), F32),
        compiler_params=pltpu.CompilerParams(dimension_semantics=("arbitrary",),
                                             vmem_limit_bytes=VMEM_LIMIT),
        name="out_ffn",
    )(x2d, oa, ob, wo, prm["gffn"], wup, wdn, prm["gl"])


def _delta_stages(qkv_ref, gb_ref, gate_ref, og_ref, oa_out, s_ref, gcs_ref, u_ref, w_ref, qe_ref, qk_ref,
                  kdt_ref, *, nb, tm):
    c = DN_CHUNK
    n_chunks = tm // c
    assert 2 * c == LANES and H_A % 2 == 0
    ri = lax.broadcasted_iota(jnp.int32, (c, LANES), 0)
    lane = lax.broadcasted_iota(jnp.int32, (c, LANES), 1)
    left = lane < c
    cj = jnp.where(left, lane, lane - c)
    lower = ri >= cj
    strict = ri > cj
    eye = (ri == cj).astype(F32)
    cat = jnp.concatenate
    qcols = lambda hh: slice(hh * DK, (hh + 1) * DK)
    kcols = lambda hh: slice(H_A * DK + hh * DK, H_A * DK + (hh + 1) * DK)
    vcols = lambda hh: slice(2 * H_A * DK + hh * DV, 2 * H_A * DK + (hh + 1) * DV)
    pcols = lambda p: slice(p * LANES, (p + 1) * LANES)
    units = [(b, e, slice(e * c, (e + 1) * c)) for e in range(n_chunks) for b in range(nb)]
    uchains = [(ui, hh) for ui in range(len(units)) for hh in range(H_A)]
    upairs = [(ui, p) for ui in range(len(units)) for p in range(H_A // 2)]
    v = {}

    def local_inputs():
        for ui, (b, e, rows) in enumerate(units):
            gbc = gb_ref[b, rows, :]
            gcs = gbc
            sh = 1
            while sh < c:
                gcs = gcs + jnp.where(ri >= sh, pltpu.roll(gcs, sh, axis=0), 0.0)
                sh *= 2
            gcs_ref[b, rows, :] = gcs
            gcs_t = gcs.T
            for hh in range(H_A):
                v["q", ui, hh] = qkv_ref[b, rows, qcols(hh)]
                v["k", ui, hh] = qkv_ref[b, rows, kcols(hh)]
                v["g", ui, hh] = jnp.broadcast_to(gcs[:, hh:hh + 1], (c, LANES))
                v["beta", ui, hh] = jnp.broadcast_to(gbc[:, H_A + hh:H_A + hh + 1], (c, LANES))
                v["kb", ui, hh] = v["k", ui, hh] * v["beta", ui, hh]
            for p in range(H_A // 2):
                h0, h1 = 2 * p, 2 * p + 1
                g_col = jnp.where(left, v["g", ui, h0], v["g", ui, h1])
                g_row = jnp.broadcast_to(cat([gcs_t[h0:h0 + 1, :], gcs_t[h1:h1 + 1, :]], axis=1), (c, LANES))
                v["decay", ui, p] = jnp.where(lower, jnp.exp(jnp.where(lower, g_col - g_row, 0.0)), 0.0)

    def kq_products():
        zero = jnp.zeros((c, DK), F32)
        for ui, p in upairs:
            h0, h1 = 2 * p, 2 * p + 1
            lhs = cat([cat([v["kb", ui, h0], v["kb", ui, h1]], axis=1),
                       cat([v["q", ui, h0], v["q", ui, h1]], axis=1)], axis=0)
            rhs = cat([cat([v["k", ui, h0], zero], axis=1), cat([zero, v["k", ui, h1]], axis=1)], axis=0)
            v["kq", ui, p] = _nt_dot(lhs, rhs)

    def level_mask(s):
        return ((ri // (2 * s)) == (cj // (2 * s))) & (((ri // s) % 2) == 1) & (((cj // s) % 2) == 0)

    def first_level():
        for ui, p in upairs:
            v["l", ui, p] = jnp.where(strict, v["kq", ui, p][:c] * v["decay", ui, p], 0.0)
            v["x", ui, p] = eye - jnp.where(level_mask(1), v["l", ui, p], 0.0)

    def level_first_half(s):
        def run():
            for ui, p in upairs:
                xb = v["x", ui, p].astype(BF16)
                cs = jnp.where(level_mask(s), v["l", ui, p], 0.0).astype(BF16)
                v["xb", ui, p] = xb
                v["y", ui, p] = jnp.dot(xb, _pair_blockdiag(cs, left), preferred_element_type=F32)
        return run

    def level_second_half():
        for ui, p in upairs:
            z = jnp.dot(v["y", ui, p].astype(BF16), _pair_blockdiag(v["xb", ui, p], left),
                        preferred_element_type=F32)
            v["x", ui, p] = v["x", ui, p] - z

    def uw_products():
        zero = jnp.zeros((c, DV), F32)
        for ui, p in upairs:
            b, e, rows = units[ui]
            h0, h1 = 2 * p, 2 * p + 1
            vb0 = qkv_ref[b, rows, vcols(h0)] * v["beta", ui, h0]
            vb1 = qkv_ref[b, rows, vcols(h1)] * v["beta", ui, h1]
            kg0 = v["kb", ui, h0] * jnp.exp(v["g", ui, h0])
            kg1 = v["kb", ui, h1] * jnp.exp(v["g", ui, h1])
            rhs = cat([cat([vb0, zero, kg0, zero], axis=1), cat([zero, vb1, zero, kg1], axis=1)], axis=0)
            uw = _dot(v["x", ui, p], rhs)
            u_ref[b, rows, qcols(h0)] = uw[:, 0:DV]
            u_ref[b, rows, qcols(h1)] = uw[:, DV:2 * DV]
            w_ref[b, rows, qcols(h0)] = uw[:, 2 * DV:3 * DV].astype(BF16)
            w_ref[b, rows, qcols(h1)] = uw[:, 3 * DV:4 * DV].astype(BF16)
            qk_ref[b, rows, pcols(p)] = (v["kq", ui, p][c:] * v["decay", ui, p]).astype(BF16)
        for ui, hh in uchains:
            b, e, rows = units[ui]
            g_col = v["g", ui, hh]
            qe_ref[b, rows, qcols(hh)] = (v["q", ui, hh] * jnp.exp(g_col)).astype(BF16)
            k_dec = v["k", ui, hh] * jnp.exp(g_col[c - 1:c, :] - g_col)
            kdt_ref[b, hh, e * DK:(e + 1) * DK, :] = k_dec.T.astype(BF16)

    chains = [(b, hh) for b in range(nb) for hh in range(H_A)]

    def recur_first(e):
        rows = slice(e * c, (e + 1) * c)

        def run():
            for b, hh in chains:
                s_old = s_ref[b, hh]
                s_bf = s_old.astype(BF16)
                v["wq", b, hh] = jnp.dot(cat([w_ref[b, rows, qcols(hh)], qe_ref[b, rows, qcols(hh)]], axis=0),
                                         s_bf, preferred_element_type=F32)
        return run

    def recur_second(e):
        rows = slice(e * c, (e + 1) * c)

        def run():
            zero = jnp.zeros((c, DV), BF16)
            for b, hh in chains:
                v["vn", b, hh] = (u_ref[b, rows, qcols(hh)] - v["wq", b, hh][:c]).astype(BF16)
                v["upd", b, hh] = jnp.dot(kdt_ref[b, hh, e * DK:(e + 1) * DK, :], v["vn", b, hh],
                                          preferred_element_type=F32)
            for b in range(nb):
                for p in range(H_A // 2):
                    rhs = cat([cat([v["vn", b, 2 * p], zero], axis=1), cat([zero, v["vn", b, 2 * p + 1]], axis=1)],
                              axis=0)
                    both = jnp.dot(qk_ref[b, rows, pcols(p)], rhs, preferred_element_type=F32)
                    v["qkv", b, 2 * p], v["qkv", b, 2 * p + 1] = both[:, :DV], both[:, DV:]
        return run

    def recur_third(e):
        rows = slice(e * c, (e + 1) * c)

        def run():
            og = og_ref[...]
            for b, hh in chains:
                g_last = jnp.broadcast_to(gcs_ref[b, e * c + c - 1:e * c + c, hh:hh + 1], (1, LANES))
                s_ref[b, hh] = s_ref[b, hh] * jnp.exp(g_last) + v["upd", b, hh]
                o = v["wq", b, hh][c:] + v["qkv", b, hh]
                oa_out[b, rows, qcols(hh)] = (_rmsnorm(o, og) * _silu(gate_ref[b, rows, qcols(hh)])).astype(
                    oa_out.dtype)
        return run

    stages = [local_inputs, kq_products, first_level]
    s = 2
    while s < c:
        stages += [level_first_half(s), level_second_half]
        s *= 2
    stages.append(uw_products)
    for e in range(n_chunks):
        stages += [recur_first(e), recur_second(e), recur_third(e)]
    return stages


def _ffn_stages(x_ref, ob_ref, oa_in, wo_ref, gf_ref, wup_ref, wdn_ref, gl_ref, out_ref, *, nb, tm, final_norm):
    half = nb // 2 if nb % 2 == 0 else nb
    groups = nb // half
    d_ff = wup_ref.shape[1]
    v = {}

    def mixer_out(gi):
        def run():
            seqs = slice(gi * half, (gi + 1) * half)
            flat = lambda a: a.reshape(half * tm, a.shape[-1])
            y = jnp.dot(flat(oa_in[seqs]), wo_ref[0:W_A, :], preferred_element_type=F32)
            y = y + jnp.dot(flat(ob_ref[seqs]).astype(BF16), wo_ref[W_A:W_A + W_B, :], preferred_element_type=F32)
            x1 = flat(x_ref[seqs]) + y
            v["x1", gi] = x1
            v["h", gi] = _rmsnorm(x1, gf_ref[...]).astype(BF16)
        return run

    def up(gi, j):
        def run():
            a = jnp.dot(v["h", gi], wup_ref[:, j * FF_BLOCK:(j + 1) * FF_BLOCK], preferred_element_type=F32)
            v["a", gi, j] = jnp.square(jnp.maximum(a, 0.0)).astype(BF16)
        return run

    def down(gi, j):
        def run():
            part = jnp.dot(v["a", gi, j], wdn_ref[j * FF_BLOCK:(j + 1) * FF_BLOCK, :], preferred_element_type=F32)
            v["acc", gi] = part if j == 0 else v["acc", gi] + part
        return run

    def finish(gi):
        def run():
            x2 = v["x1", gi] + v["acc", gi]
            if final_norm:
                x2 = _rmsnorm(x2, gl_ref[...])
            out_ref[gi * half:(gi + 1) * half] = x2.reshape(half, tm, x2.shape[-1])
        return run

    stages = []
    for gi in range(groups):
        stages.append(mixer_out(gi))
        for j in range(d_ff // FF_BLOCK):
            stages += [up(gi, j), down(gi, j)]
        stages.append(finish(gi))
    return stages


def _delta_ffn_kernel(qkv_ref, gb_ref, gate_ref, x_ref, ob_ref, og_ref, wo_ref, gf_ref, wup_ref, wdn_ref, gl_ref,
                      out_ref, sout_ref,
                      s_ref, gcs_ref, u_ref, w_ref, qe_ref, qk_ref, kdt_ref, oa_ref,
                      *, nb, tm, nt, n_tiles, final_norm):
    t = pl.program_id(0)
    i = lax.rem(t, nt)
    slot = lax.rem(t, 2)

    @pl.when(t == 0)
    def _():
        oa_ref[...] = jnp.zeros(oa_ref.shape, oa_ref.dtype)

    @pl.when(i == 0)
    def _():
        s_ref[...] = jnp.zeros(s_ref.shape, F32)

    delta = _delta_stages(qkv_ref, gb_ref, gate_ref, og_ref, oa_ref.at[slot], s_ref, gcs_ref, u_ref, w_ref,
                          qe_ref, qk_ref, kdt_ref, nb=nb, tm=tm)
    ffn = _ffn_stages(x_ref, ob_ref, oa_ref.at[1 - slot], wo_ref, gf_ref, wup_ref, wdn_ref, gl_ref, out_ref,
                      nb=nb, tm=tm, final_norm=final_norm)
    order, fi = [], 0
    for di, stage in enumerate(delta):
        order.append(stage)
        while fi * len(delta) < (di + 1) * len(ffn):
            order.append(ffn[fi])
            fi += 1
    for stage in order:
        stage()

    @pl.when((i == nt - 1) & (t < n_tiles))
    def _():
        sout_ref[...] = s_ref[...]


def _delta_ffn(qkv, gb, gate, x, ob, layer, prm, weights, *, nb, tm, final_norm):
    batch, seq, d = x.shape
    nt = seq // tm
    n_tiles = (batch // nb) * nt
    wo, wup, wdn = weights
    cur = lambda t: jnp.minimum(t, n_tiles - 1)
    prev = lambda t: jnp.maximum(t - 1, 0)
    blk = lambda w, tile: pl.BlockSpec((nb, tm, w), lambda t: (tile(t) // nt, tile(t) % nt, 0))
    once = lambda w: pl.BlockSpec(w.shape, lambda t: (0, 0), pipeline_mode=pl.Buffered(1))
    return pl.pallas_call(
        functools.partial(_delta_ffn_kernel, nb=nb, tm=tm, nt=nt, n_tiles=n_tiles, final_norm=final_norm),
        grid=(n_tiles + 1,),
        in_specs=[blk(QKV, cur), blk(AB_W, cur), blk(W_A, cur), blk(d, prev), blk(W_B, prev),
                  _layer_spec(prm["og"], layer), once(wo), _layer_spec(prm["gffn"], layer), once(wup), once(wdn),
                  pl.BlockSpec((1, d), lambda t: (0, 0))],
        out_specs=[blk(d, prev), pl.BlockSpec((nb, H_A, DK, DV), lambda t: (cur(t) // nt, 0, 0, 0))],
        out_shape=[jax.ShapeDtypeStruct((batch, seq, d), F32),
                   jax.ShapeDtypeStruct((batch, H_A, DK, DV), F32)],
        scratch_shapes=[pltpu.VMEM((nb, H_A, DK, DV), F32),
                        pltpu.VMEM((nb, tm, AB_W), F32),
                        pltpu.VMEM((nb, tm, W_A), F32),
                        pltpu.VMEM((nb, tm, W_A), BF16),
                        pltpu.VMEM((nb, tm, W_A), BF16),
                        pltpu.VMEM((nb, tm, H_A // 2 * LANES), BF16),
                        pltpu.VMEM((nb, H_A, tm // DN_CHUNK * DK, DN_CHUNK), BF16),
                        pltpu.VMEM((2, nb, tm, W_A), BF16)],
        compiler_params=pltpu.CompilerParams(dimension_semantics=("arbitrary",),
                                             vmem_limit_bytes=VMEM_LIMIT),
        name="delta_ffn",
    )(qkv, gb, gate, x, ob, prm["og"], wo, prm["gffn"], wup, wdn, prm["gl"])


def _sample_in_kernel(x_ref, g_ref, win_ref, cw_ref, alog_ref, dtb_ref, vg_ref, ws0_ref, bs0_ref,
                      c0_ref, c1_ref, c2_ref, qkv_ref, gate_ref, gb_ref, ob_ref, vb_ref, cnew_ref):
    h = _rmsnorm(x_ref[...], g_ref[...]).astype(BF16)
    p = jnp.dot(h, win_ref[...], preferred_element_type=F32)
    pq = p[:, :QKV]
    cw = cw_ref[...]
    c1 = c1_ref[...]
    c2 = c2_ref[...]
    y = c0_ref[...] * cw[0:1, :] + c1 * cw[1:2, :] + c2 * cw[2:3, :] + pq * cw[3:4, :]
    cnew_ref[0] = c1
    cnew_ref[1] = c2
    cnew_ref[2] = pq
    for j in range(QKV // CONV_BLK):
        cols = slice(j * CONV_BLK, (j + 1) * CONV_BLK)
        qkv_ref[:, cols] = _qkv_activation(y[:, cols], j)
    gate_ref[...] = p[:, OFF_GATE:OFF_U]
    gb_ref[...] = _decay_beta(p[:, OFF_AB:PROJ_PAD], alog_ref[...], dtb_ref[...])
    vb = _rmsnorm(p[:, OFF_V:OFF_AB], vg_ref[...])
    vb_ref[...] = vb
    ob_ref[...] = p[:, OFF_U:OFF_V] * (vb * ws0_ref[...] + bs0_ref[...])


def _sample_in(x2d, conv_rows, layer, prm):
    n, d = x2d.shape
    full = lambda w: pl.BlockSpec((n, w), lambda i: (0, 0))
    conv_row = lambda j: pl.BlockSpec((None, None, n, QKV), lambda i: (layer, j, 0, 0))
    names = ("gmix", "win", "cw", "alog", "dtb", "vg", "ws0", "bs0")
    return pl.pallas_call(
        _sample_in_kernel,
        grid=(1,),
        in_specs=[full(d)] + [_layer_spec(prm[k], layer) for k in names] + [conv_row(j) for j in range(CONV_W - 1)],
        out_specs=[full(QKV), full(W_A), full(AB_W), full(W_B), full(W_B),
                   pl.BlockSpec((CONV_W - 1, n, QKV), lambda i: (0, 0, 0))],
        out_shape=[jax.ShapeDtypeStruct((n, QKV), F32), jax.ShapeDtypeStruct((n, W_A), F32),
                   jax.ShapeDtypeStruct((n, AB_W), F32), jax.ShapeDtypeStruct((n, W_B), F32),
                   jax.ShapeDtypeStruct((n, W_B), F32),
                   jax.ShapeDtypeStruct((CONV_W - 1, n, QKV), F32)],
        compiler_params=pltpu.CompilerParams(dimension_semantics=("arbitrary",),
                                             vmem_limit_bytes=VMEM_LIMIT),
        name="sample_in",
    )(x2d, *[prm[k] for k in names], conv_rows, conv_rows, conv_rows)


def _delta_step_kernel(*refs, tb, chained):
    if chained:
        qkv_ref, gb_ref, gate_ref, og_ref, s_ref, _, oa_ref, snew_ref = refs
    else:
        qkv_ref, gb_ref, gate_ref, og_ref, s_ref, oa_ref, snew_all_ref = refs
        snew_ref = snew_all_ref.at[0]
        snew_all_ref[1:] = jnp.zeros((snew_all_ref.shape[0] - 1,) + snew_all_ref.shape[1:], F32)
    og = og_ref[...]
    gb = gb_ref[...]
    spread = (lax.broadcasted_iota(jnp.int32, (tb, tb * DV), 1) // DV
              == lax.broadcasted_iota(jnp.int32, (tb, tb * DV), 0)).astype(BF16)
    for hh in range(H_A):
        q = qkv_ref[:, hh * DK:(hh + 1) * DK]
        k = qkv_ref[:, H_A * DK + hh * DK:H_A * DK + (hh + 1) * DK]
        v = qkv_ref[:, 2 * H_A * DK + hh * DV:2 * H_A * DK + (hh + 1) * DV]
        q_cols = jnp.dot(q.T.astype(BF16), spread, preferred_element_type=F32)
        k_cols = jnp.dot(k.T.astype(BF16), spread, preferred_element_type=F32)
        decay = jnp.exp(gb[:, hh:hh + 1])
        beta = gb[:, H_A + hh:H_A + hh + 1]
        o_rows = []
        for t in range(tb):
            s = s_ref[t, hh] * decay[t:t + 1, :]
            k_col = k_cols[:, t * DV:(t + 1) * DV]
            q_col = q_cols[:, t * DV:(t + 1) * DV]
            kv = jnp.sum(s * k_col, axis=0, keepdims=True)
            delta = (v[t:t + 1, :] - kv) * beta[t:t + 1, :]
            s = s + k_col * delta
            snew_ref[t, hh] = s
            o_rows.append(jnp.sum(s * q_col, axis=0, keepdims=True))
        o = jnp.concatenate(o_rows, axis=0)
        gate = gate_ref[:, hh * DV:(hh + 1) * DV]
        oa_ref[:, hh * DV:(hh + 1) * DV] = _rmsnorm(o, og) * _silu(gate)


def _delta_step(qkv, gb, gate, state_all, new_all, layer, prm, *, tb):
    n = qkv.shape[0]
    depth = state_all.shape[0]
    row_spec = lambda w: pl.BlockSpec((tb, w), lambda i: (i, 0))
    st_spec = pl.BlockSpec((None, tb, H_A, DK, DV), lambda i: (layer, i, 0, 0, 0))
    chained = new_all is not None
    in_specs = [row_spec(QKV), row_spec(AB_W), row_spec(W_A), _layer_spec(prm["og"], layer), st_spec]
    args = [qkv, gb, gate, prm["og"], state_all]
    if chained:
        in_specs.append(pl.BlockSpec(memory_space=pl.ANY))
        args.append(new_all)
        new_spec = st_spec
    else:
        assert layer == 0
        new_spec = pl.BlockSpec((depth, tb, H_A, DK, DV), lambda i: (0, i, 0, 0, 0))
    return pl.pallas_call(
        functools.partial(_delta_step_kernel, tb=tb, chained=chained),
        grid=(n // tb,),
        in_specs=in_specs,
        out_specs=[row_spec(W_A), new_spec],
        out_shape=[jax.ShapeDtypeStruct((n, W_A), F32), jax.ShapeDtypeStruct(state_all.shape, F32)],
        input_output_aliases={len(args) - 1: 1} if chained else {},
        compiler_params=pltpu.CompilerParams(dimension_semantics=("arbitrary",),
                                             vmem_limit_bytes=VMEM_LIMIT),
        name="delta_step",
    )(*args)


def _reorder_win_kernel(wt_ref, out_ref):
    off_a = QKV + W_A
    off_u = off_a + 2 * H_A
    src_of = lambda dst: dst if dst < off_a else dst + 2 * H_A
    for dst in range(0, OFF_AB, LANES):
        src = src_of(dst)
        out_ref[:, dst:dst + LANES] = wt_ref[src:src + LANES, :].T.astype(BF16)
    ab = jnp.concatenate([wt_ref[off_a:off_u, :], jnp.zeros((AB_W - 2 * H_A, wt_ref.shape[1]), F32)], axis=0)
    out_ref[:, OFF_AB:PROJ_PAD] = ab.T.astype(BF16)


def _reorder_win(wt, *, cols):
    depth, width, d = wt.shape
    return pl.pallas_call(
        _reorder_win_kernel,
        grid=(depth, d // cols),
        in_specs=[pl.BlockSpec((None, width, cols), lambda l, i: (l, 0, i))],
        out_specs=pl.BlockSpec((None, cols, PROJ_PAD), lambda l, i: (l, i, 0)),
        out_shape=jax.ShapeDtypeStruct((depth, d, PROJ_PAD), BF16),
        compiler_params=pltpu.CompilerParams(dimension_semantics=("arbitrary", "arbitrary"),
                                             vmem_limit_bytes=VMEM_LIMIT),
        name="reorder_win",
    )(wt)


def _pick_tile(seq, candidates):
    for tm in candidates:
        if seq % tm == 0:
            return tm
    raise ValueError(f"prompt length must be a multiple of {candidates[-1]}")


def kernel(x_prompt, x_sample, state_delta, state_conv, norm_mix_g, w_in, conv_w, A_log, dt_bias, o_norm_g,
           v_norm_g, w_s, b_s, w_o, norm_ffn_g, w_up, w_down, norm_f_g):
    batch, seq, d = x_prompt.shape
    n_dec, dec_seq, _ = x_sample.shape
    depth = w_in.shape[0]
    assert dec_seq == 1 and seq % GM_CHUNK == 0 and n_dec % SUBLANES == 0
    assert seq % PROJ_ROWS == 0
    ptm = _pick_tile(seq, (2 * PROJ_ROWS, PROJ_ROWS))
    tb = 2 * SUBLANES if n_dec % (2 * SUBLANES) == 0 else SUBLANES
    nb = next(n for n in (4, 2, 1) if batch % n == 0)
    dtm = GM_CHUNK

    lane_pad = lambda v: jnp.pad(v.astype(F32), ((0, 0), (0, AB_W - v.shape[1]))).reshape(depth, 1, AB_W)
    prm = {
        "win": _reorder_win(jnp.swapaxes(w_in, 1, 2), cols=256 if d % 256 == 0 else LANES),
        "gmix": norm_mix_g.reshape(depth, 1, d),
        "gffn": norm_ffn_g.reshape(depth, 1, d),
        "gl": norm_f_g.reshape(1, d),
        "cw": conv_w,
        "alog": lane_pad(A_log),
        "dtb": lane_pad(dt_bias),
        "vg": v_norm_g.reshape(depth, 1, W_B),
        "og": o_norm_g.reshape(depth, 1, DV),
        "ws": w_s,
        "bsb": jnp.broadcast_to(b_s[:, :, :, None], b_s.shape + (DH_B,)).astype(F32),
        "ws0": jnp.repeat(w_s[:, :, 0, 0], DH_B, axis=-1).reshape(depth, 1, W_B),
        "bs0": jnp.repeat(b_s[:, :, 0], DH_B, axis=-1).reshape(depth, 1, W_B),
    }
    conv_rows = jnp.swapaxes(state_conv, 1, 2)

    xp = x_prompt.reshape(batch * seq, d)
    xs = x_sample.reshape(n_dec, d)
    dp, cp, cs, vs = [], [], [], []
    new_state = None
    for l in range(depth):
        last = l == depth - 1
        (qkv, gate, gb, ob, ctail), ffn_w = _prompt_in(xp, l, prm, (w_o, w_up, w_down),
                                                        batch=batch, seq=seq, tm=ptm)
        xp, s_fin = _delta_ffn(qkv.reshape(batch, seq, QKV), gb.reshape(batch, seq, AB_W),
                               gate.reshape(batch, seq, W_A), xp.reshape(batch, seq, d),
                               ob.reshape(batch, seq, W_B), l, prm, ffn_w, nb=nb, tm=dtm, final_norm=last)
        xp = xp.reshape(batch * seq, d)
        dp.append(s_fin)
        cp.append(ctail)

        qkv_s, gate_s, gb_s, ob_s, vb_s, cnew = _sample_in(xs, conv_rows, l, prm)
        oa_s, new_state = _delta_step(qkv_s, gb_s, gate_s, state_delta, new_state, l, prm,
                                      tb=SUBLANES if new_state is None else tb)
        xs = _out_ffn(xs, oa_s, ob_s, l, prm, ffn_w, tm=n_dec, final_norm=last)
        cs.append(cnew)
        vs.append(vb_s.reshape(n_dec, 1, W_B))

    return (xp.reshape(batch, seq, d), xs.reshape(n_dec, 1, d), jnp.stack(dp), jnp.swapaxes(jnp.stack(cp), 1, 2),
            new_state, jnp.swapaxes(jnp.stack(cs), 1, 2), jnp.stack(vs))
```

```python
import functools

import jax
import jax.numpy as jnp
from jax import lax
from jax.experimental import pallas as pl
from jax.experimental.pallas import tpu as pltpu

F32 = jnp.float32
BF16 = jnp.bfloat16

H_A = 4
DK = 128
DV = 128
W_A = H_A * DV
QKV = 2 * H_A * DK + H_A * DV
CONV_W = 4
DN_CHUNK = 64
H_B = 4
DH_B = 128
W_B = H_B * DH_B
GM_CHUNK = 128
EPS = 1e-6

LANES = 128
SUBLANES = 8
AB_W = LANES
OFF_GATE = QKV
OFF_U = OFF_GATE + W_A
OFF_V = OFF_U + W_B
OFF_AB = OFF_V + W_B
PROJ_PAD = OFF_AB + AB_W
CONV_BLK = H_A * DK

VMEM_LIMIT = 56 * 1024 * 1024
FF_BLOCK = 1024
PROJ_ROWS = 512
PREP_CHUNKS = 2


def _rmsnorm(x, g):
    return x * lax.rsqrt(jnp.mean(x * x, axis=-1, keepdims=True) + EPS) * g


def _silu(x):
    return x * jax.nn.sigmoid(x)


def _softplus(x):
    return jnp.maximum(x, 0.0) + jnp.log1p(jnp.exp(-jnp.abs(x)))


def _layer_spec(arr, layer):
    nd = arr.ndim - 1
    return pl.BlockSpec((None,) + arr.shape[1:], lambda *_: (layer,) + (0,) * nd,
                        pipeline_mode=pl.Buffered(1))


def _qkv_activation(y, j):
    a = _silu(y)
    if j >= 2:
        return a
    blocks = []
    for hh in range(H_A):
        blk = a[:, hh * DK:(hh + 1) * DK]
        inv = lax.rsqrt(jnp.sum(blk * blk, axis=-1, keepdims=True) + EPS)
        if j == 0:
            inv = inv * (DK ** -0.5)
        blocks.append(blk * inv)
    return jnp.concatenate(blocks, axis=1)


def _decay_beta(ab, alog, dtb):
    lane = lax.broadcasted_iota(jnp.int32, ab.shape, 1)
    gdec = -jnp.exp(alog) * _softplus(ab + dtb)
    beta = jax.nn.sigmoid(ab)
    return jnp.where(lane < H_A, gdec, beta)


def _prompt_in_kernel(x_ref, g_ref, win_ref, cw_ref, alog_ref, dtb_ref, vg_ref, ws_ref, bsb_ref, *rest,
                      tm, nt, n_cast):
    cast_in, rest = rest[:n_cast], rest[n_cast:]
    (qkv_ref, gate_ref, gb_ref, ob_ref, ctail_ref), rest = rest[:5], rest[5:]
    cast_out, (carry_ref,) = rest[:n_cast], rest[n_cast:]
    for src, dst in zip(cast_in, cast_out):
        dst[...] = src[...].astype(BF16)
    i = pl.program_id(1)
    sm = PROJ_ROWS
    n_sub = tm // sm

    @pl.when(i == 0)
    def _():
        carry_ref[...] = jnp.zeros(carry_ref.shape, F32)

    sub = lax.broadcasted_iota(jnp.int32, (SUBLANES, CONV_BLK), 0)
    ri = lax.broadcasted_iota(jnp.int32, (GM_CHUNK, GM_CHUNK), 0)
    ci = lax.broadcasted_iota(jnp.int32, (GM_CHUNK, GM_CHUNK), 1)
    nc = sm // GM_CHUNK
    hs, tails, mixes = {}, {}, {}

    def proj(s, lo, hi):
        if s not in hs:
            hs[s] = _rmsnorm(x_ref[s * sm:(s + 1) * sm, :], g_ref[...]).astype(BF16)
        return jnp.dot(hs[s], win_ref[:, lo:hi], preferred_element_type=F32)

    def conv_act(s, j, pj):
        rows = slice(s * sm, (s + 1) * sm)
        cols = slice(j * CONV_BLK, (j + 1) * CONV_BLK)
        cw = cw_ref[:, cols]
        prev = carry_ref[:, cols] if s == 0 else tails[s - 1, j]

        def shift(x, x_prev, k):
            sh = pltpu.roll(x, k, axis=0)
            head = jnp.where(sub < k, pltpu.roll(x_prev, k, axis=0), sh[0:SUBLANES])
            return jnp.concatenate([head, sh[SUBLANES:]], axis=0)

        w0, w1, w2, w3 = (cw[k:k + 1, :] for k in range(CONV_W))
        pj1 = shift(pj, prev, 1)
        far = pj * w1 + pj1 * w0
        far_prev = prev * w1 + pltpu.roll(prev, 1, axis=0) * w0
        y = (pj * w3 + pj1 * w2) + shift(far, far_prev, 2)
        tails[s, j] = pj[sm - SUBLANES:sm, :]
        if s == n_sub - 1:
            carry_ref[:, cols] = tails[s, j]
            for k in range(CONV_W - 1):
                row = sm - (CONV_W - 1) + k
                ctail_ref[k, pl.ds(pl.program_id(0), 1), cols] = pj[row:row + 1, :]
        qkv_ref[rows, cols] = _qkv_activation(y, j)

    def gmlp_mix(s, pv):
        vb = _rmsnorm(pv, vg_ref[...]).astype(BF16)
        mixes[s] = []
        for hh in range(H_B):
            wm = jnp.where(ri >= ci, ws_ref[hh], 0.0).astype(BF16)
            cols = slice(hh * DH_B, (hh + 1) * DH_B)
            rhs = jnp.concatenate([vb[c * GM_CHUNK:(c + 1) * GM_CHUNK, cols] for c in range(nc)], axis=1)
            mixes[s].append(jnp.dot(wm, rhs, preferred_element_type=F32))

    def gmlp_gate(s, pu):
        for hh in range(H_B):
            cols = slice(hh * DH_B, (hh + 1) * DH_B)
            bias = bsb_ref[hh]
            for c in range(nc):
                src = slice(c * GM_CHUNK, (c + 1) * GM_CHUNK)
                dst = slice(s * sm + c * GM_CHUNK, s * sm + (c + 1) * GM_CHUNK)
                ob_ref[dst, cols] = pu[src, cols] * (mixes[s][hh][:, c * DH_B:(c + 1) * DH_B] + bias)

    def store_gb(s, pab):
        gb_ref[s * sm:(s + 1) * sm, :] = _decay_beta(pab, alog_ref[...], dtb_ref[...])

    def store_gate(s, pg):
        gate_ref[s * sm:(s + 1) * sm, :] = pg

    stages = []
    for s in range(n_sub):
        for j in range(QKV // CONV_BLK):
            stages.append((s, j * CONV_BLK, (j + 1) * CONV_BLK, functools.partial(conv_act, s, j)))
    for s in range(n_sub):
        stages.append((s, OFF_V, OFF_AB, functools.partial(gmlp_mix, s)))
        stages.append((s, OFF_U, OFF_V, functools.partial(gmlp_gate, s)))
        stages.append((s, OFF_AB, PROJ_PAD, functools.partial(store_gb, s)))
        stages.append((s, OFF_GATE, OFF_U, functools.partial(store_gate, s)))
    pending = proj(*stages[0][:3])
    for n, stage in enumerate(stages):
        ahead = proj(*stages[n + 1][:3]) if n + 1 < len(stages) else None
        stage[3](pending)
        pending = ahead


def _prompt_in(x2d, layer, prm, later_weights, *, batch, seq, tm):
    nt = seq // tm
    rows = batch * seq
    d = x2d.shape[1]
    steps = batch * nt
    row_spec = lambda w: pl.BlockSpec((tm, w), lambda b, i: (b * nt + i, 0))
    names = ("gmix", "win", "cw", "alog", "dtb", "vg", "ws", "bsb")
    slab_in, slab_out, slab_shape = [], [], []
    for w in later_weights:
        _, wr, wc = w.shape
        assert wr % (steps * 2 * SUBLANES) == 0
        slab_in.append(pl.BlockSpec((None, wr // steps, wc), lambda b, i: (layer, b * nt + i, 0)))
        slab_out.append(pl.BlockSpec((wr // steps, wc), lambda b, i: (b * nt + i, 0)))
        slab_shape.append(jax.ShapeDtypeStruct((wr, wc), BF16))
    outs = pl.pallas_call(
        functools.partial(_prompt_in_kernel, tm=tm, nt=nt, n_cast=len(later_weights)),
        grid=(batch, nt),
        in_specs=[row_spec(d)] + [_layer_spec(prm[n], layer) for n in names] + slab_in,
        out_specs=[row_spec(QKV), row_spec(W_A), row_spec(AB_W), row_spec(W_B),
                   pl.BlockSpec((CONV_W - 1, batch, QKV), lambda b, i: (0, 0, 0))] + slab_out,
        out_shape=[jax.ShapeDtypeStruct((rows, QKV), F32), jax.ShapeDtypeStruct((rows, W_A), F32),
                   jax.ShapeDtypeStruct((rows, AB_W), F32), jax.ShapeDtypeStruct((rows, W_B), F32),
                   jax.ShapeDtypeStruct((CONV_W - 1, batch, QKV), F32)] + slab_shape,
        scratch_shapes=[pltpu.VMEM((SUBLANES, QKV), F32)],
        compiler_params=pltpu.CompilerParams(dimension_semantics=("arbitrary", "arbitrary"),
                                             vmem_limit_bytes=VMEM_LIMIT),
        name="prompt_in",
    )(x2d, *[prm[n] for n in names], *later_weights)
    return outs[:5], outs[5:]


def _pair_blockdiag(m, top, bottom):
    zero = jnp.zeros_like(m)
    return jnp.concatenate([jnp.where(top, m, zero), jnp.where(bottom, m, zero)], axis=0)


def _unit_lower_inverses(lpairs, ri, cj, left):
    c = lpairs[0].shape[0]
    eye = (ri == cj).astype(F32)
    right = jnp.logical_not(left)
    lbs = [lp.astype(BF16) for lp in lpairs]
    xs = None
    s = 1
    while s < c:
        mask = ((ri // (2 * s)) == (cj // (2 * s))) & (((ri // s) % 2) == 1) & (((cj // s) % 2) == 0)
        if xs is None:
            xs = [eye - jnp.where(mask, lp, 0.0) for lp in lpairs]
        else:
            xbs = [x.astype(BF16) for x in xs]
            ys = [jnp.dot(xb, _pair_blockdiag(lb, left & mask, right & mask), preferred_element_type=F32)
                  for xb, lb in zip(xbs, lbs)]
            zs = [jnp.dot(y.astype(BF16), _pair_blockdiag(xb, left, right), preferred_element_type=F32)
                  for y, xb in zip(ys, xbs)]
            xs = [x - z for x, z in zip(xs, zs)]
        s *= 2
    return xs


def _delta_kernel(qkv_ref, gb_ref, gate_ref, og_ref, oa_ref, sout_ref,
                  s_ref, gcs_ref, u_ref, w_ref, qe_ref, qk_ref, kdt_ref, *, nb, tm, nt):
    i = pl.program_id(1)
    c = DN_CHUNK

    @pl.when(i == 0)
    def _():
        s_ref[...] = jnp.zeros(s_ref.shape, F32)

    assert 2 * c == LANES and H_A % 2 == 0
    ri = lax.broadcasted_iota(jnp.int32, (c, LANES), 0)
    lane = lax.broadcasted_iota(jnp.int32, (c, LANES), 1)
    left = lane < c
    cj = jnp.where(left, lane, lane - c)
    lower = ri >= cj
    strict = ri > cj
    og = og_ref[...]
    chains = [(b, hh) for b in range(nb) for hh in range(H_A)]
    pairs = [(b, p) for b in range(nb) for p in range(H_A // 2)]
    qcols = lambda hh: slice(hh * DK, (hh + 1) * DK)
    kcols = lambda hh: slice(H_A * DK + hh * DK, H_A * DK + (hh + 1) * DK)
    vcols = lambda hh: slice(2 * H_A * DK + hh * DV, 2 * H_A * DK + (hh + 1) * DV)
    pcols = lambda p: slice(p * LANES, (p + 1) * LANES)
    cat = jnp.concatenate
    bf = lambda a: a.astype(BF16)
    zero_bf = jnp.zeros((c, LANES), BF16)

    def prepare(n, carry):
        chunk_ids = [n * PREP_CHUNKS + e for e in range(PREP_CHUNKS) for _ in range(nb)]
        units = [(b, pl.ds(pl.multiple_of((n * PREP_CHUNKS + e) * c, c), c))
                 for e in range(PREP_CHUNKS) for b in range(nb)]
        chains = [(ui, hh) for ui in range(len(units)) for hh in range(H_A)]
        pairs = [(ui, p) for ui in range(len(units)) for p in range(H_A // 2)]
        gbcs, gcss, gcsts = [], [], []
        for b, rows in units:
            gbc = gb_ref[b, rows, :]
            gcs = gbc
            sh = 1
            while sh < c:
                gcs = gcs + jnp.where(ri >= sh, pltpu.roll(gcs, sh, axis=0), 0.0)
                sh *= 2
            gcs_ref[b, rows, :] = gcs
            gbcs.append(gbc)
            gcss.append(gcs)
            gcsts.append(gcs.T)
        qs, ks, vs, kbs, betas, g_cols = {}, {}, {}, {}, {}, {}
        for ui, hh in chains:
            b, rows = units[ui]
            qs[ui, hh] = qkv_ref[b, rows, qcols(hh)]
            ks[ui, hh] = qkv_ref[b, rows, kcols(hh)]
            vs[ui, hh] = qkv_ref[b, rows, vcols(hh)]
            g_cols[ui, hh] = jnp.broadcast_to(gcss[ui][:, hh:hh + 1], (c, LANES))
            betas[ui, hh] = jnp.broadcast_to(gbcs[ui][:, H_A + hh:H_A + hh + 1], (c, LANES))
            kbs[ui, hh] = ks[ui, hh] * betas[ui, hh]
        decays, kqs = [], []
        for ui, p in pairs:
            b, rows = units[ui]
            h0, h1 = 2 * p, 2 * p + 1
            g_col = jnp.where(left, g_cols[ui, h0], g_cols[ui, h1])
            g_row = jnp.broadcast_to(cat([gcsts[ui][h0:h0 + 1, :], gcsts[ui][h1:h1 + 1, :]], axis=1), (c, LANES))
            decays.append(jnp.where(lower, jnp.exp(jnp.where(lower, g_col - g_row, 0.0)), 0.0))
            lhs = cat([cat([bf(kbs[ui, h0]), bf(kbs[ui, h1])], axis=1),
                       cat([bf(qs[ui, h0]), bf(qs[ui, h1])], axis=1)], axis=0)
            rhs = cat([cat([bf(ks[ui, h0]), zero_bf], axis=1), cat([zero_bf, bf(ks[ui, h1])], axis=1)], axis=0)
            kqs.append(lax.dot_general(lhs, rhs, (((1,), (1,)), ((), ())), preferred_element_type=F32))
        lpairs = [jnp.where(strict, kq[:c] * decay, 0.0) for kq, decay in zip(kqs, decays)]
        ts = _unit_lower_inverses(lpairs, ri, cj, left)
        for j, (ui, p) in enumerate(pairs):
            b, rows = units[ui]
            h0, h1 = 2 * p, 2 * p + 1
            vb0 = bf(vs[ui, h0] * betas[ui, h0])
            vb1 = bf(vs[ui, h1] * betas[ui, h1])
            kg0 = bf(kbs[ui, h0] * jnp.exp(g_cols[ui, h0]))
            kg1 = bf(kbs[ui, h1] * jnp.exp(g_cols[ui, h1]))
            rhs = cat([cat([vb0, zero_bf, kg0, zero_bf], axis=1), cat([zero_bf, vb1, zero_bf, kg1], axis=1)], axis=0)
            uw = jnp.dot(bf(ts[j]), rhs, preferred_element_type=F32)
            u_ref[b, rows, qcols(h0)] = uw[:, 0:DV]
            u_ref[b, rows, qcols(h1)] = uw[:, DV:2 * DV]
            w_ref[b, rows, qcols(h0)] = uw[:, 2 * DV:3 * DV].astype(BF16)
            w_ref[b, rows, qcols(h1)] = uw[:, 3 * DV:4 * DV].astype(BF16)
            qk_ref[b, rows, pcols(p)] = (kqs[j][c:] * decays[j]).astype(BF16)
        for ui, hh in chains:
            b, rows = units[ui]
            g_col = g_cols[ui, hh]
            qe_ref[b, rows, qcols(hh)] = (qs[ui, hh] * jnp.exp(g_col)).astype(BF16)
            k_dec = ks[ui, hh] * jnp.exp(g_col[c - 1:c, :] - g_col)
            kdt_ref[b, hh, pl.ds(pl.multiple_of(chunk_ids[ui] * DK, DK), DK), :] = k_dec.T.astype(BF16)
        return carry

    def recur(n, carry):
        rows = pl.ds(pl.multiple_of(n * c, c), c)
        gcss = [gcs_ref[b, rows, :] for b in range(nb)]
        s_olds = [s_ref[b, hh] for b, hh in chains]
        s_bfs = [s.astype(BF16) for s in s_olds]
        wqs = [jnp.dot(cat([w_ref[b, rows, qcols(hh)], qe_ref[b, rows, qcols(hh)]], axis=0), s_bf,
                       preferred_element_type=F32) for (b, hh), s_bf in zip(chains, s_bfs)]
        v_bfs = [(u_ref[b, rows, qcols(hh)] - wq[:c]).astype(BF16) for (b, hh), wq in zip(chains, wqs)]
        kdt_rows = pl.ds(pl.multiple_of(n * DK, DK), DK)
        upds = [jnp.dot(kdt_ref[b, hh, kdt_rows, :], v_bf, preferred_element_type=F32)
                for (b, hh), v_bf in zip(chains, v_bfs)]
        for j, (b, hh) in enumerate(chains):
            g_last = jnp.broadcast_to(gcss[b][c - 1:c, hh:hh + 1], (1, LANES))
            s_ref[b, hh] = s_olds[j] * jnp.exp(g_last) + upds[j]
        qkvs = []
        for j, (b, p) in enumerate(pairs):
            v0, v1 = v_bfs[2 * j], v_bfs[2 * j + 1]
            rhs = cat([cat([v0, zero_bf], axis=1), cat([zero_bf, v1], axis=1)], axis=0)
            both = jnp.dot(qk_ref[b, rows, pcols(p)], rhs, preferred_element_type=F32)
            qkvs += [both[:, :DV], both[:, DV:]]
        for j, (b, hh) in enumerate(chains):
            o = wqs[j][c:] + qkvs[j]
            oa_ref[b, rows, qcols(hh)] = _rmsnorm(o, og) * _silu(gate_ref[b, rows, qcols(hh)])
        return carry

    lax.fori_loop(0, tm // (c * PREP_CHUNKS), prepare, 0)
    lax.fori_loop(0, tm // c, recur, 0)

    @pl.when(i == nt - 1)
    def _():
        sout_ref[...] = s_ref[...]


def _delta_prompt(qkv, gb, gate, layer, prm, *, nb, tm):
    batch, seq, _ = qkv.shape
    nt = seq // tm
    blk = lambda w: pl.BlockSpec((nb, tm, w), lambda b, i: (b, i, 0))
    return pl.pallas_call(
        functools.partial(_delta_kernel, nb=nb, tm=tm, nt=nt),
        grid=(batch // nb, nt),
        in_specs=[blk(QKV), blk(AB_W), blk(W_A), _layer_spec(prm["og"], layer)],
        out_specs=[blk(W_A), pl.BlockSpec((nb, H_A, DK, DV), lambda b, i: (b, 0, 0, 0))],
        out_shape=[jax.ShapeDtypeStruct((batch, seq, W_A), F32),
                   jax.ShapeDtypeStruct((batch, H_A, DK, DV), F32)],
        scratch_shapes=[pltpu.VMEM((nb, H_A, DK, DV), F32),
                        pltpu.VMEM((nb, tm, AB_W), F32),
                        pltpu.VMEM((nb, tm, W_A), F32),
                        pltpu.VMEM((nb, tm, W_A), BF16),
                        pltpu.VMEM((nb, tm, W_A), BF16),
                        pltpu.VMEM((nb, tm, H_A // 2 * LANES), BF16),
                        pltpu.VMEM((nb, H_A, tm // DN_CHUNK * DK, DN_CHUNK), BF16)],
        compiler_params=pltpu.CompilerParams(dimension_semantics=("arbitrary", "arbitrary"),
                                             vmem_limit_bytes=VMEM_LIMIT),
        name="delta_prompt",
    )(qkv, gb, gate, prm["og"])


def _out_ffn_kernel(x_ref, oa_ref, ob_ref, wo_ref, gf_ref, wup_ref, wdn_ref, gl_ref, out_ref, *, final_norm):
    y = jnp.dot(oa_ref[...].astype(BF16), wo_ref[0:W_A, :], preferred_element_type=F32)
    y = y + jnp.dot(ob_ref[...].astype(BF16), wo_ref[W_A:W_A + W_B, :], preferred_element_type=F32)
    x1 = x_ref[...] + y
    h = _rmsnorm(x1, gf_ref[...]).astype(BF16)
    d_ff = wup_ref.shape[1]
    ffn = None
    for j in range(d_ff // FF_BLOCK):
        cols = slice(j * FF_BLOCK, (j + 1) * FF_BLOCK)
        a = jnp.dot(h, wup_ref[:, cols], preferred_element_type=F32)
        a = jnp.square(jnp.maximum(a, 0.0)).astype(BF16)
        part = jnp.dot(a, wdn_ref[cols, :], preferred_element_type=F32)
        ffn = part if ffn is None else ffn + part
    x2 = x1 + ffn
    if final_norm:
        x2 = _rmsnorm(x2, gl_ref[...])
    out_ref[...] = x2


def _out_ffn(x2d, oa, ob, layer, prm, weights, *, tm, final_norm):
    rows, d = x2d.shape
    wo, wup, wdn = weights
    row_spec = lambda w: pl.BlockSpec((tm, w), lambda i: (i, 0))
    once = lambda w: pl.BlockSpec(w.shape, lambda i: (0, 0), pipeline_mode=pl.Buffered(1))
    return pl.pallas_call(
        functools.partial(_out_ffn_kernel, final_norm=final_norm),
        grid=(rows // tm,),
        in_specs=[row_spec(d), row_spec(W_A), row_spec(W_B), once(wo), _layer_spec(prm["gffn"], layer),
                  once(wup), once(wdn), pl.BlockSpec((1, d), lambda i: (0, 0))],
        out_specs=row_spec(d),
        out_shape=jax.ShapeDtypeStruct((rows, d), F32),
        compiler_params=pltpu.CompilerParams(dimension_semantics=("arbitrary",),
                                             vmem_limit_bytes=VMEM_LIMIT),
        name="out_ffn",
    )(x2d, oa, ob, wo, prm["gffn"], wup, wdn, prm["gl"])


def _sample_in_kernel(x_ref, g_ref, win_ref, cw_ref, alog_ref, dtb_ref, vg_ref, ws0_ref, bs0_ref,
                      c0_ref, c1_ref, c2_ref, qkv_ref, gate_ref, gb_ref, ob_ref, vb_ref, cnew_ref):
    h = _rmsnorm(x_ref[...], g_ref[...]).astype(BF16)
    p = jnp.dot(h, win_ref[...], preferred_element_type=F32)
    pq = p[:, :QKV]
    cw = cw_ref[...]
    c1 = c1_ref[...]
    c2 = c2_ref[...]
    y = c0_ref[...] * cw[0:1, :] + c1 * cw[1:2, :] + c2 * cw[2:3, :] + pq * cw[3:4, :]
    cnew_ref[0] = c1
    cnew_ref[1] = c2
    cnew_ref[2] = pq
    for j in range(QKV // CONV_BLK):
        cols = slice(j * CONV_BLK, (j + 1) * CONV_BLK)
        qkv_ref[:, cols] = _qkv_activation(y[:, cols], j)
    gate_ref[...] = p[:, OFF_GATE:OFF_U]
    gb_ref[...] = _decay_beta(p[:, OFF_AB:PROJ_PAD], alog_ref[...], dtb_ref[...])
    vb = _rmsnorm(p[:, OFF_V:OFF_AB], vg_ref[...])
    vb_ref[...] = vb
    ob_ref[...] = p[:, OFF_U:OFF_V] * (vb * ws0_ref[...] + bs0_ref[...])


def _sample_in(x2d, conv_rows, layer, prm):
    n, d = x2d.shape
    full = lambda w: pl.BlockSpec((n, w), lambda i: (0, 0))
    conv_row = lambda j: pl.BlockSpec((None, None, n, QKV), lambda i: (layer, j, 0, 0))
    names = ("gmix", "win", "cw", "alog", "dtb", "vg", "ws0", "bs0")
    return pl.pallas_call(
        _sample_in_kernel,
        grid=(1,),
        in_specs=[full(d)] + [_layer_spec(prm[k], layer) for k in names] + [conv_row(j) for j in range(CONV_W - 1)],
        out_specs=[full(QKV), full(W_A), full(AB_W), full(W_B), full(W_B),
                   pl.BlockSpec((CONV_W - 1, n, QKV), lambda i: (0, 0, 0))],
        out_shape=[jax.ShapeDtypeStruct((n, QKV), F32), jax.ShapeDtypeStruct((n, W_A), F32),
                   jax.ShapeDtypeStruct((n, AB_W), F32), jax.ShapeDtypeStruct((n, W_B), F32),
                   jax.ShapeDtypeStruct((n, W_B), F32),
                   jax.ShapeDtypeStruct((CONV_W - 1, n, QKV), F32)],
        compiler_params=pltpu.CompilerParams(dimension_semantics=("arbitrary",),
                                             vmem_limit_bytes=VMEM_LIMIT),
        name="sample_in",
    )(x2d, *[prm[k] for k in names], conv_rows, conv_rows, conv_rows)


def _delta_step_kernel(*refs, tb, chained):
    if chained:
        qkv_ref, gb_ref, gate_ref, og_ref, s_ref, _, oa_ref, snew_ref = refs
    else:
        qkv_ref, gb_ref, gate_ref, og_ref, s_ref, oa_ref, snew_all_ref = refs
        snew_ref = snew_all_ref.at[0]
        snew_all_ref[1:] = jnp.zeros((snew_all_ref.shape[0] - 1,) + snew_all_ref.shape[1:], F32)
    og = og_ref[...]
    gb = gb_ref[...]
    spread = (lax.broadcasted_iota(jnp.int32, (tb, tb * DV), 1) // DV
              == lax.broadcasted_iota(jnp.int32, (tb, tb * DV), 0)).astype(BF16)
    for hh in range(H_A):
        q = qkv_ref[:, hh * DK:(hh + 1) * DK]
        k = qkv_ref[:, H_A * DK + hh * DK:H_A * DK + (hh + 1) * DK]
        v = qkv_ref[:, 2 * H_A * DK + hh * DV:2 * H_A * DK + (hh + 1) * DV]
        q_cols = jnp.dot(q.T.astype(BF16), spread, preferred_element_type=F32)
        k_cols = jnp.dot(k.T.astype(BF16), spread, preferred_element_type=F32)
        decay = jnp.exp(gb[:, hh:hh + 1])
        beta = gb[:, H_A + hh:H_A + hh + 1]
        o_rows = []
        for t in range(tb):
            s = s_ref[t, hh] * decay[t:t + 1, :]
            k_col = k_cols[:, t * DV:(t + 1) * DV]
            q_col = q_cols[:, t * DV:(t + 1) * DV]
            kv = jnp.sum(s * k_col, axis=0, keepdims=True)
            delta = (v[t:t + 1, :] - kv) * beta[t:t + 1, :]
            s = s + k_col * delta
            snew_ref[t, hh] = s
            o_rows.append(jnp.sum(s * q_col, axis=0, keepdims=True))
        o = jnp.concatenate(o_rows, axis=0)
        gate = gate_ref[:, hh * DV:(hh + 1) * DV]
        oa_ref[:, hh * DV:(hh + 1) * DV] = _rmsnorm(o, og) * _silu(gate)


def _delta_step(qkv, gb, gate, state_all, new_all, layer, prm, *, tb):
    n = qkv.shape[0]
    depth = state_all.shape[0]
    row_spec = lambda w: pl.BlockSpec((tb, w), lambda i: (i, 0))
    st_spec = pl.BlockSpec((None, tb, H_A, DK, DV), lambda i: (layer, i, 0, 0, 0))
    chained = new_all is not None
    in_specs = [row_spec(QKV), row_spec(AB_W), row_spec(W_A), _layer_spec(prm["og"], layer), st_spec]
    args = [qkv, gb, gate, prm["og"], state_all]
    if chained:
        in_specs.append(pl.BlockSpec(memory_space=pl.ANY))
        args.append(new_all)
        new_spec = st_spec
    else:
        assert layer == 0
        new_spec = pl.BlockSpec((depth, tb, H_A, DK, DV), lambda i: (0, i, 0, 0, 0))
    return pl.pallas_call(
        functools.partial(_delta_step_kernel, tb=tb, chained=chained),
        grid=(n // tb,),
        in_specs=in_specs,
        out_specs=[row_spec(W_A), new_spec],
        out_shape=[jax.ShapeDtypeStruct((n, W_A), F32), jax.ShapeDtypeStruct(state_all.shape, F32)],
        input_output_aliases={len(args) - 1: 1} if chained else {},
        compiler_params=pltpu.CompilerParams(dimension_semantics=("arbitrary",),
                                             vmem_limit_bytes=VMEM_LIMIT),
        name="delta_step",
    )(*args)


def _reorder_win_kernel(wt_ref, out_ref):
    off_a = QKV + W_A
    off_u = off_a + 2 * H_A
    src_of = lambda dst: dst if dst < off_a else dst + 2 * H_A
    for dst in range(0, OFF_AB, LANES):
        src = src_of(dst)
        out_ref[:, dst:dst + LANES] = wt_ref[src:src + LANES, :].T.astype(BF16)
    ab = jnp.concatenate([wt_ref[off_a:off_u, :], jnp.zeros((AB_W - 2 * H_A, wt_ref.shape[1]), F32)], axis=0)
    out_ref[:, OFF_AB:PROJ_PAD] = ab.T.astype(BF16)


def _reorder_win(wt, *, cols):
    depth, width, d = wt.shape
    return pl.pallas_call(
        _reorder_win_kernel,
        grid=(depth, d // cols),
        in_specs=[pl.BlockSpec((None, width, cols), lambda l, i: (l, 0, i))],
        out_specs=pl.BlockSpec((None, cols, PROJ_PAD), lambda l, i: (l, i, 0)),
        out_shape=jax.ShapeDtypeStruct((depth, d, PROJ_PAD), BF16),
        compiler_params=pltpu.CompilerParams(dimension_semantics=("arbitrary", "arbitrary"),
                                             vmem_limit_bytes=VMEM_LIMIT),
        name="reorder_win",
    )(wt)


def _pick_tile(seq, candidates):
    for tm in candidates:
        if seq % tm == 0:
            return tm
    raise ValueError(f"prompt length must be a multiple of {candidates[-1]}")


def kernel(x_prompt, x_sample, state_delta, state_conv, norm_mix_g, w_in, conv_w, A_log, dt_bias, o_norm_g,
           v_norm_g, w_s, b_s, w_o, norm_ffn_g, w_up, w_down, norm_f_g):
    batch, seq, d = x_prompt.shape
    n_dec, dec_seq, _ = x_sample.shape
    depth = w_in.shape[0]
    assert dec_seq == 1 and seq % GM_CHUNK == 0 and n_dec % SUBLANES == 0
    assert seq % PROJ_ROWS == 0
    tm = _pick_tile(seq, (1024, 512, 256, 128))
    ptm = _pick_tile(seq, (2 * PROJ_ROWS, PROJ_ROWS))
    tb = 2 * SUBLANES if n_dec % (2 * SUBLANES) == 0 else SUBLANES
    nb = next(n for n in (8, 4, 2, 1) if batch % n == 0)
    dtm = GM_CHUNK

    lane_pad = lambda v: jnp.pad(v.astype(F32), ((0, 0), (0, AB_W - v.shape[1]))).reshape(depth, 1, AB_W)
    prm = {
        "win": _reorder_win(jnp.swapaxes(w_in, 1, 2), cols=256 if d % 256 == 0 else LANES),
        "gmix": norm_mix_g.reshape(depth, 1, d),
        "gffn": norm_ffn_g.reshape(depth, 1, d),
        "gl": norm_f_g.reshape(1, d),
        "cw": conv_w,
        "alog": lane_pad(A_log),
        "dtb": lane_pad(dt_bias),
        "vg": v_norm_g.reshape(depth, 1, W_B),
        "og": o_norm_g.reshape(depth, 1, DV),
        "ws": w_s,
        "bsb": jnp.broadcast_to(b_s[:, :, :, None], b_s.shape + (DH_B,)).astype(F32),
        "ws0": jnp.repeat(w_s[:, :, 0, 0], DH_B, axis=-1).reshape(depth, 1, W_B),
        "bs0": jnp.repeat(b_s[:, :, 0], DH_B, axis=-1).reshape(depth, 1, W_B),
    }
    conv_rows = jnp.swapaxes(state_conv, 1, 2)

    xp = x_prompt.reshape(batch * seq, d)
    xs = x_sample.reshape(n_dec, d)
    dp, cp, cs, vs = [], [], [], []
    new_state = None
    for l in range(depth):
        last = l == depth - 1
        (qkv, gate, gb, ob, ctail), ffn_w = _prompt_in(xp, l, prm, (w_o, w_up, w_down),
                                                        batch=batch, seq=seq, tm=ptm)
        oa, s_fin = _delta_prompt(qkv.reshape(batch, seq, QKV), gb.reshape(batch, seq, AB_W),
                                  gate.reshape(batch, seq, W_A), l, prm, nb=nb, tm=dtm)
        xp = _out_ffn(xp, oa.reshape(batch * seq, W_A), ob, l, prm, ffn_w, tm=tm, final_norm=last)
        dp.append(s_fin)
        cp.append(ctail)

        qkv_s, gate_s, gb_s, ob_s, vb_s, cnew = _sample_in(xs, conv_rows, l, prm)
        oa_s, new_state = _delta_step(qkv_s, gb_s, gate_s, state_delta, new_state, l, prm,
                                      tb=SUBLANES if new_state is None else tb)
        xs = _out_ffn(xs, oa_s, ob_s, l, prm, ffn_w, tm=n_dec, final_norm=last)
        cs.append(cnew)
        vs.append(vb_s.reshape(n_dec, 1, W_B))

    return (xp.reshape(batch, seq, d), xs.reshape(n_dec, 1, d), jnp.stack(dp), jnp.swapaxes(jnp.stack(cp), 1, 2),
            new_state, jnp.swapaxes(jnp.stack(cs), 1, 2), jnp.stack(vs))
```

```python
import functools

import jax
import jax.numpy as jnp
from jax import lax
from jax.experimental import pallas as pl
from jax.experimental.pallas import tpu as pltpu

F32 = jnp.float32
BF16 = jnp.bfloat16

H_A = 4
DK = 128
DV = 128
W_A = H_A * DV
QKV = 2 * H_A * DK + H_A * DV
CONV_W = 4
DN_CHUNK = 64
H_B = 4
DH_B = 128
W_B = H_B * DH_B
GM_CHUNK = 128
EPS = 1e-6

LANES = 128
SUBLANES = 8
AB_W = LANES
OFF_GATE = QKV
OFF_U = OFF_GATE + W_A
OFF_V = OFF_U + W_B
OFF_AB = OFF_V + W_B
PROJ_PAD = OFF_AB + AB_W
CONV_BLK = H_A * DK

VMEM_LIMIT = 60 * 1024 * 1024
FF_BLOCK = 1024
PROJ_ROWS = 512
PREP_CHUNKS = 2


def _rmsnorm(x, g):
    return x * lax.rsqrt(jnp.mean(x * x, axis=-1, keepdims=True) + EPS) * g


def _silu(x):
    return x * jax.nn.sigmoid(x)


def _softplus(x):
    return jnp.maximum(x, 0.0) + jnp.log1p(jnp.exp(-jnp.abs(x)))


def _layer_spec(arr, layer):
    nd = arr.ndim - 1
    return pl.BlockSpec((None,) + arr.shape[1:], lambda *_: (layer,) + (0,) * nd,
                        pipeline_mode=pl.Buffered(1))


def _qkv_activation(y, j):
    a = _silu(y)
    if j >= 2:
        return a
    blocks = []
    for hh in range(H_A):
        blk = a[:, hh * DK:(hh + 1) * DK]
        inv = lax.rsqrt(jnp.sum(blk * blk, axis=-1, keepdims=True) + EPS)
        if j == 0:
            inv = inv * (DK ** -0.5)
        blocks.append(blk * inv)
    return jnp.concatenate(blocks, axis=1)


def _decay_beta(ab, alog, dtb):
    lane = lax.broadcasted_iota(jnp.int32, ab.shape, 1)
    gdec = -jnp.exp(alog) * _softplus(ab + dtb)
    beta = jax.nn.sigmoid(ab)
    return jnp.where(lane < H_A, gdec, beta)


def _prompt_in_kernel(x_ref, g_ref, win_ref, cw_ref, alog_ref, dtb_ref, vg_ref, ws_ref, bsb_ref, *rest,
                      tm, nt, n_tiles, n_cast):
    cast_in, rest = rest[:n_cast], rest[n_cast:]
    dec_in, rest = rest[:6], rest[6:]
    prompt_out, rest = rest[:5], rest[5:]
    cast_out, rest = rest[:n_cast], rest[n_cast:]
    dec_out, (carry_ref,) = rest[:6], rest[6:]
    for src, dst in zip(cast_in, cast_out):
        dst[...] = src[...].astype(BF16)
    step = pl.program_id(0)

    @pl.when(step < n_tiles)
    def _():
        _prompt_tile(x_ref, g_ref, win_ref, cw_ref, alog_ref, dtb_ref, vg_ref, ws_ref, bsb_ref, *prompt_out,
                     carry_ref, tm=tm, i=lax.rem(step, nt), b=step // nt)

    @pl.when(step == n_tiles)
    def _():
        xs_ref, ws0_ref, bs0_ref, c0_ref, c1_ref, c2_ref = dec_in
        _sample_in_kernel(xs_ref, g_ref, win_ref, cw_ref, alog_ref, dtb_ref, vg_ref, ws0_ref, bs0_ref,
                          c0_ref, c1_ref, c2_ref, *dec_out)


def _prompt_tile(x_ref, g_ref, win_ref, cw_ref, alog_ref, dtb_ref, vg_ref, ws_ref, bsb_ref,
                 qkv_ref, gate_ref, gb_ref, ob_ref, ctail_ref, carry_ref, *, tm, i, b):
    sm = PROJ_ROWS
    n_sub = tm // sm

    @pl.when(i == 0)
    def _():
        carry_ref[...] = jnp.zeros(carry_ref.shape, F32)

    sub = lax.broadcasted_iota(jnp.int32, (SUBLANES, CONV_BLK), 0)
    ri = lax.broadcasted_iota(jnp.int32, (GM_CHUNK, GM_CHUNK), 0)
    ci = lax.broadcasted_iota(jnp.int32, (GM_CHUNK, GM_CHUNK), 1)
    nc = sm // GM_CHUNK
    hs, tails, mixes = {}, {}, {}

    def proj(s, lo, hi):
        if s not in hs:
            hs[s] = _rmsnorm(x_ref[s * sm:(s + 1) * sm, :], g_ref[...]).astype(BF16)
        return jnp.dot(hs[s], win_ref[:, lo:hi], preferred_element_type=F32)

    def conv_act(s, j, pj):
        rows = slice(s * sm, (s + 1) * sm)
        cols = slice(j * CONV_BLK, (j + 1) * CONV_BLK)
        cw = cw_ref[:, cols]
        prev = carry_ref[:, cols] if s == 0 else tails[s - 1, j]

        def shift(x, x_prev, k):
            sh = pltpu.roll(x, k, axis=0)
            head = jnp.where(sub < k, pltpu.roll(x_prev, k, axis=0), sh[0:SUBLANES])
            return jnp.concatenate([head, sh[SUBLANES:]], axis=0)

        w0, w1, w2, w3 = (cw[k:k + 1, :] for k in range(CONV_W))
        pj1 = shift(pj, prev, 1)
        far = pj * w1 + pj1 * w0
        far_prev = prev * w1 + pltpu.roll(prev, 1, axis=0) * w0
        y = (pj * w3 + pj1 * w2) + shift(far, far_prev, 2)
        tails[s, j] = pj[sm - SUBLANES:sm, :]
        if s == n_sub - 1:
            carry_ref[:, cols] = tails[s, j]
            for k in range(CONV_W - 1):
                row = sm - (CONV_W - 1) + k
                ctail_ref[k, pl.ds(b, 1), cols] = pj[row:row + 1, :]
        qkv_ref[rows, cols] = _qkv_activation(y, j)

    def gmlp_mix(s, pv):
        vb = _rmsnorm(pv, vg_ref[...]).astype(BF16)
        mixes[s] = []
        for hh in range(H_B):
            wm = jnp.where(ri >= ci, ws_ref[hh], 0.0).astype(BF16)
            cols = slice(hh * DH_B, (hh + 1) * DH_B)
            rhs = jnp.concatenate([vb[c * GM_CHUNK:(c + 1) * GM_CHUNK, cols] for c in range(nc)], axis=1)
            mixes[s].append(jnp.dot(wm, rhs, preferred_element_type=F32))

    def gmlp_gate(s, pu):
        for hh in range(H_B):
            cols = slice(hh * DH_B, (hh + 1) * DH_B)
            bias = bsb_ref[hh]
            for c in range(nc):
                src = slice(c * GM_CHUNK, (c + 1) * GM_CHUNK)
                dst = slice(s * sm + c * GM_CHUNK, s * sm + (c + 1) * GM_CHUNK)
                ob_ref[dst, cols] = pu[src, cols] * (mixes[s][hh][:, c * DH_B:(c + 1) * DH_B] + bias)

    def store_gb(s, pab):
        gb_ref[s * sm:(s + 1) * sm, :] = _decay_beta(pab, alog_ref[...], dtb_ref[...])

    def store_gate(s, pg):
        gate_ref[s * sm:(s + 1) * sm, :] = pg

    stages = []
    for s in range(n_sub):
        for j in range(QKV // CONV_BLK):
            stages.append((s, j * CONV_BLK, (j + 1) * CONV_BLK, functools.partial(conv_act, s, j)))
    for s in range(n_sub):
        stages.append((s, OFF_V, OFF_AB, functools.partial(gmlp_mix, s)))
        stages.append((s, OFF_U, OFF_V, functools.partial(gmlp_gate, s)))
        stages.append((s, OFF_AB, PROJ_PAD, functools.partial(store_gb, s)))
        stages.append((s, OFF_GATE, OFF_U, functools.partial(store_gate, s)))
    pending = proj(*stages[0][:3])
    for n, stage in enumerate(stages):
        ahead = proj(*stages[n + 1][:3]) if n + 1 < len(stages) else None
        stage[3](pending)
        pending = ahead


def _prompt_in(x2d, xs, conv_rows, layer, prm, later_weights, *, batch, seq, tm):
    nt = seq // tm
    rows = batch * seq
    d = x2d.shape[1]
    n = xs.shape[0]
    steps = batch * nt
    tile = lambda t: jnp.minimum(t, steps - 1)
    row_spec = lambda w: pl.BlockSpec((tm, w), lambda t: (tile(t), 0))
    dec_spec = lambda w: pl.BlockSpec((n, w), lambda t: (0, 0))
    conv_row = lambda j: pl.BlockSpec((None, None, n, QKV), lambda t: (layer, j, 0, 0))
    names = ("gmix", "win", "cw", "alog", "dtb", "vg", "ws", "bsb")
    slab_in, slab_out, slab_shape = [], [], []
    for w in later_weights:
        _, wr, wc = w.shape
        assert wr % (steps * 2 * SUBLANES) == 0
        slab_in.append(pl.BlockSpec((None, wr // steps, wc), lambda t: (layer, tile(t), 0)))
        slab_out.append(pl.BlockSpec((wr // steps, wc), lambda t: (tile(t), 0)))
        slab_shape.append(jax.ShapeDtypeStruct((wr, wc), BF16))
    dec_shape = lambda w: jax.ShapeDtypeStruct((n, w), F32)
    outs = pl.pallas_call(
        functools.partial(_prompt_in_kernel, tm=tm, nt=nt, n_tiles=steps, n_cast=len(later_weights)),
        grid=(steps + 1,),
        in_specs=[row_spec(d)] + [_layer_spec(prm[k], layer) for k in names] + slab_in
                 + [dec_spec(d), _layer_spec(prm["ws0"], layer), _layer_spec(prm["bs0"], layer)]
                 + [conv_row(j) for j in range(CONV_W - 1)],
        out_specs=[row_spec(QKV), row_spec(W_A), row_spec(AB_W), row_spec(W_B),
                   pl.BlockSpec((CONV_W - 1, batch, QKV), lambda t: (0, 0, 0))] + slab_out
                  + [dec_spec(QKV), dec_spec(W_A), dec_spec(AB_W), dec_spec(W_B), dec_spec(W_B),
                     pl.BlockSpec((CONV_W - 1, n, QKV), lambda t: (0, 0, 0))],
        out_shape=[jax.ShapeDtypeStruct((rows, QKV), F32), jax.ShapeDtypeStruct((rows, W_A), F32),
                   jax.ShapeDtypeStruct((rows, AB_W), F32), jax.ShapeDtypeStruct((rows, W_B), F32),
                   jax.ShapeDtypeStruct((CONV_W - 1, batch, QKV), F32)] + slab_shape
                  + [dec_shape(QKV), dec_shape(W_A), dec_shape(AB_W), dec_shape(W_B), dec_shape(W_B),
                     jax.ShapeDtypeStruct((CONV_W - 1, n, QKV), F32)],
        scratch_shapes=[pltpu.VMEM((SUBLANES, QKV), F32)],
        compiler_params=pltpu.CompilerParams(dimension_semantics=("arbitrary",),
                                             vmem_limit_bytes=VMEM_LIMIT),
        name="prompt_in",
    )(x2d, *[prm[k] for k in names], *later_weights, xs, prm["ws0"], prm["bs0"], conv_rows, conv_rows, conv_rows)
    n_cast = len(later_weights)
    return outs[:5], outs[5:5 + n_cast], outs[5 + n_cast:]


def _pair_blockdiag(m, top, bottom):
    zero = jnp.zeros_like(m)
    return jnp.concatenate([jnp.where(top, m, zero), jnp.where(bottom, m, zero)], axis=0)


def _unit_lower_inverses(lpairs, ri, cj, left):
    c = lpairs[0].shape[0]
    eye = (ri == cj).astype(F32)
    right = jnp.logical_not(left)
    lbs = [lp.astype(BF16) for lp in lpairs]
    xs = None
    s = 1
    while s < c:
        mask = ((ri // (2 * s)) == (cj // (2 * s))) & (((ri // s) % 2) == 1) & (((cj // s) % 2) == 0)
        if xs is None:
            xs = [eye - jnp.where(mask, lp, 0.0) for lp in lpairs]
        else:
            xbs = [x.astype(BF16) for x in xs]
            ys = [jnp.dot(xb, _pair_blockdiag(lb, left & mask, right & mask), preferred_element_type=F32)
                  for xb, lb in zip(xbs, lbs)]
            zs = [jnp.dot(y.astype(BF16), _pair_blockdiag(xb, left, right), preferred_element_type=F32)
                  for y, xb in zip(ys, xbs)]
            xs = [x - z for x, z in zip(xs, zs)]
        s *= 2
    return xs


def _delta_kernel(qkv_ref, gb_ref, gate_ref, og_ref, oa_ref, sout_ref,
                  s_ref, gcs_ref, u_ref, w_ref, qe_ref, qk_ref, kdt_ref, *, nb, tm, nt):
    i = pl.program_id(1)
    c = DN_CHUNK

    @pl.when(i == 0)
    def _():
        s_ref[...] = jnp.zeros(s_ref.shape, F32)

    assert 2 * c == LANES and H_A % 2 == 0
    ri = lax.broadcasted_iota(jnp.int32, (c, LANES), 0)
    lane = lax.broadcasted_iota(jnp.int32, (c, LANES), 1)
    left = lane < c
    cj = jnp.where(left, lane, lane - c)
    lower = ri >= cj
    strict = ri > cj
    og = og_ref[...]
    chains = [(b, hh) for b in range(nb) for hh in range(H_A)]
    pairs = [(b, p) for b in range(nb) for p in range(H_A // 2)]
    qcols = lambda hh: slice(hh * DK, (hh + 1) * DK)
    kcols = lambda hh: slice(H_A * DK + hh * DK, H_A * DK + (hh + 1) * DK)
    vcols = lambda hh: slice(2 * H_A * DK + hh * DV, 2 * H_A * DK + (hh + 1) * DV)
    pcols = lambda p: slice(p * LANES, (p + 1) * LANES)
    cat = jnp.concatenate
    bf = lambda a: a.astype(BF16)
    zero_bf = jnp.zeros((c, LANES), BF16)

    def prepare(n, carry):
        chunk_ids = [n * PREP_CHUNKS + e for e in range(PREP_CHUNKS) for _ in range(nb)]
        units = [(b, pl.ds(pl.multiple_of((n * PREP_CHUNKS + e) * c, c), c))
                 for e in range(PREP_CHUNKS) for b in range(nb)]
        chains = [(ui, hh) for ui in range(len(units)) for hh in range(H_A)]
        pairs = [(ui, p) for ui in range(len(units)) for p in range(H_A // 2)]
        gbcs, gcss, gcsts = [], [], []
        for b, rows in units:
            gbc = gb_ref[b, rows, :]
            gcs = gbc
            sh = 1
            while sh < c:
                gcs = gcs + jnp.where(ri >= sh, pltpu.roll(gcs, sh, axis=0), 0.0)
                sh *= 2
            gcs_ref[b, rows, :] = gcs
            gbcs.append(gbc)
            gcss.append(gcs)
            gcsts.append(gcs.T)
        qs, ks, vs, kbs, betas, g_cols = {}, {}, {}, {}, {}, {}
        for ui, hh in chains:
            b, rows = units[ui]
            qs[ui, hh] = qkv_ref[b, rows, qcols(hh)]
            ks[ui, hh] = qkv_ref[b, rows, kcols(hh)]
            vs[ui, hh] = qkv_ref[b, rows, vcols(hh)]
            g_cols[ui, hh] = jnp.broadcast_to(gcss[ui][:, hh:hh + 1], (c, LANES))
            betas[ui, hh] = jnp.broadcast_to(gbcs[ui][:, H_A + hh:H_A + hh + 1], (c, LANES))
            kbs[ui, hh] = ks[ui, hh] * betas[ui, hh]
        decays, kqs = [], []
        for ui, p in pairs:
            b, rows = units[ui]
            h0, h1 = 2 * p, 2 * p + 1
            g_col = jnp.where(left, g_cols[ui, h0], g_cols[ui, h1])
            g_row = jnp.broadcast_to(cat([gcsts[ui][h0:h0 + 1, :], gcsts[ui][h1:h1 + 1, :]], axis=1), (c, LANES))
            decays.append(jnp.where(lower, jnp.exp(jnp.where(lower, g_col - g_row, 0.0)), 0.0))
            lhs = cat([cat([bf(kbs[ui, h0]), bf(kbs[ui, h1])], axis=1),
                       cat([bf(qs[ui, h0]), bf(qs[ui, h1])], axis=1)], axis=0)
            rhs = cat([cat([bf(ks[ui, h0]), zero_bf], axis=1), cat([zero_bf, bf(ks[ui, h1])], axis=1)], axis=0)
            kqs.append(lax.dot_general(lhs, rhs, (((1,), (1,)), ((), ())), preferred_element_type=F32))
        lpairs = [jnp.where(strict, kq[:c] * decay, 0.0) for kq, decay in zip(kqs, decays)]
        ts = _unit_lower_inverses(lpairs, ri, cj, left)
        for j, (ui, p) in enumerate(pairs):
            b, rows = units[ui]
            h0, h1 = 2 * p, 2 * p + 1
            vb0 = bf(vs[ui, h0] * betas[ui, h0])
            vb1 = bf(vs[ui, h1] * betas[ui, h1])
            kg0 = bf(kbs[ui, h0] * jnp.exp(g_cols[ui, h0]))
            kg1 = bf(kbs[ui, h1] * jnp.exp(g_cols[ui, h1]))
            rhs = cat([cat([vb0, zero_bf, kg0, zero_bf], axis=1), cat([zero_bf, vb1, zero_bf, kg1], axis=1)], axis=0)
            uw = jnp.dot(bf(ts[j]), rhs, preferred_element_type=F32)
            u_ref[b, rows, qcols(h0)] = uw[:, 0:DV]
            u_ref[b, rows, qcols(h1)] = uw[:, DV:2 * DV]
            w_ref[b, rows, qcols(h0)] = uw[:, 2 * DV:3 * DV].astype(BF16)
            w_ref[b, rows, qcols(h1)] = uw[:, 3 * DV:4 * DV].astype(BF16)
            qk_ref[b, rows, pcols(p)] = (kqs[j][c:] * decays[j]).astype(BF16)
        for ui, hh in chains:
            b, rows = units[ui]
            g_col = g_cols[ui, hh]
            qe_ref[b, rows, qcols(hh)] = (qs[ui, hh] * jnp.exp(g_col)).astype(BF16)
            k_dec = ks[ui, hh] * jnp.exp(g_col[c - 1:c, :] - g_col)
            kdt_ref[b, hh, pl.ds(pl.multiple_of(chunk_ids[ui] * DK, DK), DK), :] = k_dec.T.astype(BF16)
        return carry

    def recur(n, carry):
        rows = pl.ds(pl.multiple_of(n * c, c), c)
        gcss = [gcs_ref[b, rows, :] for b in range(nb)]
        s_olds = [s_ref[b, hh] for b, hh in chains]
        s_bfs = [s.astype(BF16) for s in s_olds]
        wqs = [jnp.dot(cat([w_ref[b, rows, qcols(hh)], qe_ref[b, rows, qcols(hh)]], axis=0), s_bf,
                       preferred_element_type=F32) for (b, hh), s_bf in zip(chains, s_bfs)]
        v_bfs = [(u_ref[b, rows, qcols(hh)] - wq[:c]).astype(BF16) for (b, hh), wq in zip(chains, wqs)]
        kdt_rows = pl.ds(pl.multiple_of(n * DK, DK), DK)
        upds = [jnp.dot(kdt_ref[b, hh, kdt_rows, :], v_bf, preferred_element_type=F32)
                for (b, hh), v_bf in zip(chains, v_bfs)]
        for j, (b, hh) in enumerate(chains):
            g_last = jnp.broadcast_to(gcss[b][c - 1:c, hh:hh + 1], (1, LANES))
            s_ref[b, hh] = s_olds[j] * jnp.exp(g_last) + upds[j]
        qkvs = []
        for j, (b, p) in enumerate(pairs):
            v0, v1 = v_bfs[2 * j], v_bfs[2 * j + 1]
            rhs = cat([cat([v0, zero_bf], axis=1), cat([zero_bf, v1], axis=1)], axis=0)
            both = jnp.dot(qk_ref[b, rows, pcols(p)], rhs, preferred_element_type=F32)
            qkvs += [both[:, :DV], both[:, DV:]]
        for j, (b, hh) in enumerate(chains):
            o = wqs[j][c:] + qkvs[j]
            oa_ref[b, rows, qcols(hh)] = _rmsnorm(o, og) * _silu(gate_ref[b, rows, qcols(hh)])
        return carry

    lax.fori_loop(0, tm // (c * PREP_CHUNKS), prepare, 0)
    lax.fori_loop(0, tm // c, recur, 0)

    @pl.when(i == nt - 1)
    def _():
        sout_ref[...] = s_ref[...]


def _delta_prompt(qkv, gb, gate, layer, prm, *, nb, tm):
    batch, seq, _ = qkv.shape
    nt = seq // tm
    blk = lambda w: pl.BlockSpec((nb, tm, w), lambda b, i: (b, i, 0))
    return pl.pallas_call(
        functools.partial(_delta_kernel, nb=nb, tm=tm, nt=nt),
        grid=(batch // nb, nt),
        in_specs=[blk(QKV), blk(AB_W), blk(W_A), _layer_spec(prm["og"], layer)],
        out_specs=[blk(W_A), pl.BlockSpec((nb, H_A, DK, DV), lambda b, i: (b, 0, 0, 0))],
        out_shape=[jax.ShapeDtypeStruct((batch, seq, W_A), F32),
                   jax.ShapeDtypeStruct((batch, H_A, DK, DV), F32)],
        scratch_shapes=[pltpu.VMEM((nb, H_A, DK, DV), F32),
                        pltpu.VMEM((nb, tm, AB_W), F32),
                        pltpu.VMEM((nb, tm, W_A), F32),
                        pltpu.VMEM((nb, tm, W_A), BF16),
                        pltpu.VMEM((nb, tm, W_A), BF16),
                        pltpu.VMEM((nb, tm, H_A // 2 * LANES), BF16),
                        pltpu.VMEM((nb, H_A, tm // DN_CHUNK * DK, DN_CHUNK), BF16)],
        compiler_params=pltpu.CompilerParams(dimension_semantics=("arbitrary", "arbitrary"),
                                             vmem_limit_bytes=VMEM_LIMIT),
        name="delta_prompt",
    )(qkv, gb, gate, prm["og"])


def _out_ffn_kernel(x_ref, oa_ref, ob_ref, xs_ref, oas_ref, obs_ref, wo_ref, gf_ref, wup_ref, wdn_ref, gl_ref,
                    out_ref, outs_ref, *, n_tiles, final_norm):
    step = pl.program_id(0)
    weights = (wo_ref, gf_ref, wup_ref, wdn_ref, gl_ref)

    @pl.when(step < n_tiles)
    def _():
        _out_ffn_body(x_ref, oa_ref, ob_ref, *weights, out_ref, final_norm=final_norm)

    @pl.when(step == n_tiles)
    def _():
        _out_ffn_body(xs_ref, oas_ref, obs_ref, *weights, outs_ref, final_norm=final_norm)


def _out_ffn_body(x_ref, oa_ref, ob_ref, wo_ref, gf_ref, wup_ref, wdn_ref, gl_ref, out_ref, *, final_norm):
    y = jnp.dot(oa_ref[...].astype(BF16), wo_ref[0:W_A, :], preferred_element_type=F32)
    y = y + jnp.dot(ob_ref[...].astype(BF16), wo_ref[W_A:W_A + W_B, :], preferred_element_type=F32)
    x1 = x_ref[...] + y
    h = _rmsnorm(x1, gf_ref[...]).astype(BF16)
    d_ff = wup_ref.shape[1]
    ffn = None
    for j in range(d_ff // FF_BLOCK):
        cols = slice(j * FF_BLOCK, (j + 1) * FF_BLOCK)
        a = jnp.dot(h, wup_ref[:, cols], preferred_element_type=F32)
        a = jnp.square(jnp.maximum(a, 0.0)).astype(BF16)
        part = jnp.dot(a, wdn_ref[cols, :], preferred_element_type=F32)
        ffn = part if ffn is None else ffn + part
    x2 = x1 + ffn
    if final_norm:
        x2 = _rmsnorm(x2, gl_ref[...])
    out_ref[...] = x2


def _out_ffn(x2d, oa, ob, xs, oa_s, ob_s, layer, prm, weights, *, tm, final_norm):
    rows, d = x2d.shape
    n = xs.shape[0]
    n_tiles = rows // tm
    wo, wup, wdn = weights
    row_spec = lambda w: pl.BlockSpec((tm, w), lambda i: (jnp.minimum(i, n_tiles - 1), 0))
    dec_spec = lambda w: pl.BlockSpec((n, w), lambda i: (0, 0))
    once = lambda w: pl.BlockSpec(w.shape, lambda i: (0, 0), pipeline_mode=pl.Buffered(1))
    return pl.pallas_call(
        functools.partial(_out_ffn_kernel, n_tiles=n_tiles, final_norm=final_norm),
        grid=(n_tiles + 1,),
        in_specs=[row_spec(d), row_spec(W_A), row_spec(W_B), dec_spec(d), dec_spec(W_A), dec_spec(W_B),
                  once(wo), _layer_spec(prm["gffn"], layer), once(wup), once(wdn),
                  pl.BlockSpec((1, d), lambda i: (0, 0))],
        out_specs=[row_spec(d), dec_spec(d)],
        out_shape=[jax.ShapeDtypeStruct((rows, d), F32), jax.ShapeDtypeStruct((n, d), F32)],
        compiler_params=pltpu.CompilerParams(dimension_semantics=("arbitrary",),
                                             vmem_limit_bytes=VMEM_LIMIT),
        name="out_ffn",
    )(x2d, oa, ob, xs, oa_s, ob_s, wo, prm["gffn"], wup, wdn, prm["gl"])


def _sample_in_kernel(x_ref, g_ref, win_ref, cw_ref, alog_ref, dtb_ref, vg_ref, ws0_ref, bs0_ref,
                      c0_ref, c1_ref, c2_ref, qkv_ref, gate_ref, gb_ref, ob_ref, vb_ref, cnew_ref):
    h = _rmsnorm(x_ref[...], g_ref[...]).astype(BF16)
    p = jnp.dot(h, win_ref[...], preferred_element_type=F32)
    pq = p[:, :QKV]
    cw = cw_ref[...]
    c1 = c1_ref[...]
    c2 = c2_ref[...]
    y = c0_ref[...] * cw[0:1, :] + c1 * cw[1:2, :] + c2 * cw[2:3, :] + pq * cw[3:4, :]
    cnew_ref[0] = c1
    cnew_ref[1] = c2
    cnew_ref[2] = pq
    for j in range(QKV // CONV_BLK):
        cols = slice(j * CONV_BLK, (j + 1) * CONV_BLK)
        qkv_ref[:, cols] = _qkv_activation(y[:, cols], j)
    gate_ref[...] = p[:, OFF_GATE:OFF_U]
    gb_ref[...] = _decay_beta(p[:, OFF_AB:PROJ_PAD], alog_ref[...], dtb_ref[...])
    vb = _rmsnorm(p[:, OFF_V:OFF_AB], vg_ref[...])
    vb_ref[...] = vb
    ob_ref[...] = p[:, OFF_U:OFF_V] * (vb * ws0_ref[...] + bs0_ref[...])


def _delta_step_kernel(*refs, tb, chained):
    if chained:
        qkv_ref, gb_ref, gate_ref, og_ref, s_ref, _, oa_ref, snew_ref = refs
    else:
        qkv_ref, gb_ref, gate_ref, og_ref, s_ref, oa_ref, snew_all_ref = refs
        snew_ref = snew_all_ref.at[0]
        snew_all_ref[1:] = jnp.zeros((snew_all_ref.shape[0] - 1,) + snew_all_ref.shape[1:], F32)
    og = og_ref[...]
    gb = gb_ref[...]
    spread = (lax.broadcasted_iota(jnp.int32, (tb, tb * DV), 1) // DV
              == lax.broadcasted_iota(jnp.int32, (tb, tb * DV), 0)).astype(BF16)
    for hh in range(H_A):
        q = qkv_ref[:, hh * DK:(hh + 1) * DK]
        k = qkv_ref[:, H_A * DK + hh * DK:H_A * DK + (hh + 1) * DK]
        v = qkv_ref[:, 2 * H_A * DK + hh * DV:2 * H_A * DK + (hh + 1) * DV]
        q_cols = jnp.dot(q.T.astype(BF16), spread, preferred_element_type=F32)
        k_cols = jnp.dot(k.T.astype(BF16), spread, preferred_element_type=F32)
        decay = jnp.exp(gb[:, hh:hh + 1])
        beta = gb[:, H_A + hh:H_A + hh + 1]
        o_rows = []
        for t in range(tb):
            s = s_ref[t, hh] * decay[t:t + 1, :]
            k_col = k_cols[:, t * DV:(t + 1) * DV]
            q_col = q_cols[:, t * DV:(t + 1) * DV]
            kv = jnp.sum(s * k_col, axis=0, keepdims=True)
            delta = (v[t:t + 1, :] - kv) * beta[t:t + 1, :]
            s = s + k_col * delta
            snew_ref[t, hh] = s
            o_rows.append(jnp.sum(s * q_col, axis=0, keepdims=True))
        o = jnp.concatenate(o_rows, axis=0)
        gate = gate_ref[:, hh * DV:(hh + 1) * DV]
        oa_ref[:, hh * DV:(hh + 1) * DV] = _rmsnorm(o, og) * _silu(gate)


def _delta_step(qkv, gb, gate, state_all, new_all, layer, prm, *, tb):
    n = qkv.shape[0]
    depth = state_all.shape[0]
    row_spec = lambda w: pl.BlockSpec((tb, w), lambda i: (i, 0))
    st_spec = pl.BlockSpec((None, tb, H_A, DK, DV), lambda i: (layer, i, 0, 0, 0))
    chained = new_all is not None
    in_specs = [row_spec(QKV), row_spec(AB_W), row_spec(W_A), _layer_spec(prm["og"], layer), st_spec]
    args = [qkv, gb, gate, prm["og"], state_all]
    if chained:
        in_specs.append(pl.BlockSpec(memory_space=pl.ANY))
        args.append(new_all)
        new_spec = st_spec
    else:
        assert layer == 0
        new_spec = pl.BlockSpec((depth, tb, H_A, DK, DV), lambda i: (0, i, 0, 0, 0))
    return pl.pallas_call(
        functools.partial(_delta_step_kernel, tb=tb, chained=chained),
        grid=(n // tb,),
        in_specs=in_specs,
        out_specs=[row_spec(W_A), new_spec],
        out_shape=[jax.ShapeDtypeStruct((n, W_A), F32), jax.ShapeDtypeStruct(state_all.shape, F32)],
        input_output_aliases={len(args) - 1: 1} if chained else {},
        compiler_params=pltpu.CompilerParams(dimension_semantics=("arbitrary",),
                                             vmem_limit_bytes=VMEM_LIMIT),
        name="delta_step",
    )(*args)


def _reorder_win_kernel(wt_ref, out_ref):
    off_a = QKV + W_A
    off_u = off_a + 2 * H_A
    src_of = lambda dst: dst if dst < off_a else dst + 2 * H_A
    for dst in range(0, OFF_AB, LANES):
        src = src_of(dst)
        out_ref[:, dst:dst + LANES] = wt_ref[src:src + LANES, :].T.astype(BF16)
    ab = jnp.concatenate([wt_ref[off_a:off_u, :], jnp.zeros((AB_W - 2 * H_A, wt_ref.shape[1]), F32)], axis=0)
    out_ref[:, OFF_AB:PROJ_PAD] = ab.T.astype(BF16)


def _reorder_win(wt, *, cols):
    depth, width, d = wt.shape
    return pl.pallas_call(
        _reorder_win_kernel,
        grid=(depth, d // cols),
        in_specs=[pl.BlockSpec((None, width, cols), lambda l, i: (l, 0, i))],
        out_specs=pl.BlockSpec((None, cols, PROJ_PAD), lambda l, i: (l, i, 0)),
        out_shape=jax.ShapeDtypeStruct((depth, d, PROJ_PAD), BF16),
        compiler_params=pltpu.CompilerParams(dimension_semantics=("arbitrary", "arbitrary"),
                                             vmem_limit_bytes=VMEM_LIMIT),
        name="reorder_win",
    )(wt)


def _pick_tile(seq, candidates):
    for tm in candidates:
        if seq % tm == 0:
            return tm
    raise ValueError(f"prompt length must be a multiple of {candidates[-1]}")


def kernel(x_prompt, x_sample, state_delta, state_conv, norm_mix_g, w_in, conv_w, A_log, dt_bias, o_norm_g,
           v_norm_g, w_s, b_s, w_o, norm_ffn_g, w_up, w_down, norm_f_g):
    batch, seq, d = x_prompt.shape
    n_dec, dec_seq, _ = x_sample.shape
    depth = w_in.shape[0]
    assert dec_seq == 1 and seq % GM_CHUNK == 0 and n_dec % SUBLANES == 0
    assert seq % PROJ_ROWS == 0
    tm = _pick_tile(seq, (1024, 512, 256, 128))
    ptm = _pick_tile(seq, (2 * PROJ_ROWS, PROJ_ROWS))
    tb = _pick_tile(n_dec, (4 * SUBLANES, 2 * SUBLANES, SUBLANES))
    nb = next(n for n in (8, 4, 2, 1) if batch % n == 0)
    dtm = GM_CHUNK

    lane_pad = lambda v: jnp.pad(v.astype(F32), ((0, 0), (0, AB_W - v.shape[1]))).reshape(depth, 1, AB_W)
    prm = {
        "win": _reorder_win(jnp.swapaxes(w_in, 1, 2), cols=256 if d % 256 == 0 else LANES),
        "gmix": norm_mix_g.reshape(depth, 1, d),
        "gffn": norm_ffn_g.reshape(depth, 1, d),
        "gl": norm_f_g.reshape(1, d),
        "cw": conv_w,
        "alog": lane_pad(A_log),
        "dtb": lane_pad(dt_bias),
        "vg": v_norm_g.reshape(depth, 1, W_B),
        "og": o_norm_g.reshape(depth, 1, DV),
        "ws": w_s,
        "bsb": jnp.broadcast_to(b_s[:, :, :, None], b_s.shape + (DH_B,)).astype(F32),
        "ws0": jnp.repeat(w_s[:, :, 0, 0], DH_B, axis=-1).reshape(depth, 1, W_B),
        "bs0": jnp.repeat(b_s[:, :, 0], DH_B, axis=-1).reshape(depth, 1, W_B),
    }
    conv_rows = jnp.swapaxes(state_conv, 1, 2)

    xp = x_prompt.reshape(batch * seq, d)
    xs = x_sample.reshape(n_dec, d)
    dp, cp, cs, vs = [], [], [], []
    new_state = None
    for l in range(depth):
        last = l == depth - 1
        (qkv, gate, gb, ob, ctail), ffn_w, (qkv_s, gate_s, gb_s, ob_s, vb_s, cnew) = _prompt_in(
            xp, xs, conv_rows, l, prm, (w_o, w_up, w_down), batch=batch, seq=seq, tm=ptm)
        oa, s_fin = _delta_prompt(qkv.reshape(batch, seq, QKV), gb.reshape(batch, seq, AB_W),
                                  gate.reshape(batch, seq, W_A), l, prm, nb=nb, tm=dtm)
        dp.append(s_fin)
        cp.append(ctail)

        oa_s, new_state = _delta_step(qkv_s, gb_s, gate_s, state_delta, new_state, l, prm,
                                      tb=tb // depth if new_state is None else tb)
        xp, xs = _out_ffn(xp, oa.reshape(batch * seq, W_A), ob, xs, oa_s, ob_s, l, prm, ffn_w, tm=tm,
                          final_norm=last)
        cs.append(cnew)
        vs.append(vb_s.reshape(n_dec, 1, W_B))

    return (xp.reshape(batch, seq, d), xs.reshape(n_dec, 1, d), jnp.stack(dp), jnp.swapaxes(jnp.stack(cp), 1, 2),
            new_state, jnp.swapaxes(jnp.stack(cs), 1, 2), jnp.stack(vs))
```

```python
import functools

import jax
import jax.numpy as jnp
from jax import lax
from jax.experimental import pallas as pl
from jax.experimental.pallas import tpu as pltpu

F32 = jnp.float32
BF16 = jnp.bfloat16

H_A = 4
DK = 128
DV = 128
W_A = H_A * DV
QKV = 2 * H_A * DK + H_A * DV
CONV_W = 4
DN_CHUNK = 64
H_B = 4
DH_B = 128
W_B = H_B * DH_B
GM_CHUNK = 128
EPS = 1e-6

LANES = 128
SUBLANES = 8
AB_W = LANES
OFF_GATE = QKV
OFF_U = OFF_GATE + W_A
OFF_V = OFF_U + W_B
OFF_AB = OFF_V + W_B
PROJ_PAD = OFF_AB + AB_W
CONV_BLK = H_A * DK

VMEM_LIMIT = 60 * 1024 * 1024
FF_BLOCK = 1024
PROJ_ROWS = 512
PREP_CHUNKS = 2


def _rmsnorm(x, g):
    return x * lax.rsqrt(jnp.mean(x * x, axis=-1, keepdims=True) + EPS) * g


def _silu(x):
    return x * jax.nn.sigmoid(x)


def _softplus(x):
    return jnp.maximum(x, 0.0) + jnp.log1p(jnp.exp(-jnp.abs(x)))


def _layer_spec(arr, layer):
    nd = arr.ndim - 1
    return pl.BlockSpec((None,) + arr.shape[1:], lambda *_: (layer,) + (0,) * nd,
                        pipeline_mode=pl.Buffered(1))


def _qkv_activation(y, j):
    a = _silu(y)
    if j >= 2:
        return a
    blocks = []
    for hh in range(H_A):
        blk = a[:, hh * DK:(hh + 1) * DK]
        inv = lax.rsqrt(jnp.sum(blk * blk, axis=-1, keepdims=True) + EPS)
        if j == 0:
            inv = inv * (DK ** -0.5)
        blocks.append(blk * inv)
    return jnp.concatenate(blocks, axis=1)


def _decay_beta(ab, alog, dtb):
    lane = lax.broadcasted_iota(jnp.int32, ab.shape, 1)
    gdec = -jnp.exp(alog) * _softplus(ab + dtb)
    beta = jax.nn.sigmoid(ab)
    return jnp.where(lane < H_A, gdec, beta)


def _prompt_in_kernel(x_ref, g_ref, win_ref, cw_ref, alog_ref, dtb_ref, vg_ref, ws_ref, bsb_ref, *rest,
                      tm, nt, n_tiles, n_cast):
    cast_in, rest = rest[:n_cast], rest[n_cast:]
    dec_in, rest = rest[:6], rest[6:]
    prompt_out, rest = rest[:5], rest[5:]
    cast_out, rest = rest[:n_cast], rest[n_cast:]
    dec_out, (carry_ref,) = rest[:6], rest[6:]
    for src, dst in zip(cast_in, cast_out):
        dst[...] = src[...].astype(BF16)
    step = pl.program_id(0)

    @pl.when(step < n_tiles)
    def _():
        _prompt_tile(x_ref, g_ref, win_ref, cw_ref, alog_ref, dtb_ref, vg_ref, ws_ref, bsb_ref, *prompt_out,
                     carry_ref, tm=tm, i=lax.rem(step, nt), b=step // nt)

    @pl.when(step == n_tiles)
    def _():
        xs_ref, ws0_ref, bs0_ref, c0_ref, c1_ref, c2_ref = dec_in
        _sample_in_kernel(xs_ref, g_ref, win_ref, cw_ref, alog_ref, dtb_ref, vg_ref, ws0_ref, bs0_ref,
                          c0_ref, c1_ref, c2_ref, *dec_out)


def _prompt_tile(x_ref, g_ref, win_ref, cw_ref, alog_ref, dtb_ref, vg_ref, ws_ref, bsb_ref,
                 qkv_ref, gate_ref, gb_ref, ob_ref, ctail_ref, carry_ref, *, tm, i, b):
    sm = PROJ_ROWS
    n_sub = tm // sm

    @pl.when(i == 0)
    def _():
        carry_ref[...] = jnp.zeros(carry_ref.shape, F32)

    sub = lax.broadcasted_iota(jnp.int32, (SUBLANES, CONV_BLK), 0)
    ri = lax.broadcasted_iota(jnp.int32, (GM_CHUNK, GM_CHUNK), 0)
    ci = lax.broadcasted_iota(jnp.int32, (GM_CHUNK, GM_CHUNK), 1)
    nc = sm // GM_CHUNK
    hs, tails, mixes = {}, {}, {}

    def proj(s, lo, hi):
        if s not in hs:
            hs[s] = _rmsnorm(x_ref[s * sm:(s + 1) * sm, :], g_ref[...]).astype(BF16)
        return jnp.dot(hs[s], win_ref[:, lo:hi], preferred_element_type=F32)

    def conv_act(s, j, pj):
        rows = slice(s * sm, (s + 1) * sm)
        cols = slice(j * CONV_BLK, (j + 1) * CONV_BLK)
        cw = cw_ref[:, cols]
        prev = carry_ref[:, cols] if s == 0 else tails[s - 1, j]

        def shift(x, x_prev, k):
            sh = pltpu.roll(x, k, axis=0)
            head = jnp.where(sub < k, pltpu.roll(x_prev, k, axis=0), sh[0:SUBLANES])
            return jnp.concatenate([head, sh[SUBLANES:]], axis=0)

        w0, w1, w2, w3 = (cw[k:k + 1, :] for k in range(CONV_W))
        pj1 = shift(pj, prev, 1)
        far = pj * w1 + pj1 * w0
        far_prev = prev * w1 + pltpu.roll(prev, 1, axis=0) * w0
        y = (pj * w3 + pj1 * w2) + shift(far, far_prev, 2)
        tails[s, j] = pj[sm - SUBLANES:sm, :]
        if s == n_sub - 1:
            carry_ref[:, cols] = tails[s, j]
            for k in range(CONV_W - 1):
                row = sm - (CONV_W - 1) + k
                ctail_ref[k, pl.ds(b, 1), cols] = pj[row:row + 1, :]
        qkv_ref[rows, cols] = _qkv_activation(y, j)

    def gmlp_mix(s, pv):
        vb = _rmsnorm(pv, vg_ref[...]).astype(BF16)
        mixes[s] = []
        for hh in range(H_B):
            wm = jnp.where(ri >= ci, ws_ref[hh], 0.0).astype(BF16)
            cols = slice(hh * DH_B, (hh + 1) * DH_B)
            rhs = jnp.concatenate([vb[c * GM_CHUNK:(c + 1) * GM_CHUNK, cols] for c in range(nc)], axis=1)
            mixes[s].append(jnp.dot(wm, rhs, preferred_element_type=F32))

    def gmlp_gate(s, pu):
        for hh in range(H_B):
            cols = slice(hh * DH_B, (hh + 1) * DH_B)
            bias = bsb_ref[hh]
            for c in range(nc):
                src = slice(c * GM_CHUNK, (c + 1) * GM_CHUNK)
                dst = slice(s * sm + c * GM_CHUNK, s * sm + (c + 1) * GM_CHUNK)
                ob_ref[dst, cols] = pu[src, cols] * (mixes[s][hh][:, c * DH_B:(c + 1) * DH_B] + bias)

    def store_gb(s, pab):
        gb_ref[s * sm:(s + 1) * sm, :] = _decay_beta(pab, alog_ref[...], dtb_ref[...])

    def store_gate(s, pg):
        gate_ref[s * sm:(s + 1) * sm, :] = pg

    stages = []
    for s in range(n_sub):
        for j in range(QKV // CONV_BLK):
            stages.append((s, j * CONV_BLK, (j + 1) * CONV_BLK, functools.partial(conv_act, s, j)))
    for s in range(n_sub):
        stages.append((s, OFF_V, OFF_AB, functools.partial(gmlp_mix, s)))
        stages.append((s, OFF_U, OFF_V, functools.partial(gmlp_gate, s)))
        stages.append((s, OFF_AB, PROJ_PAD, functools.partial(store_gb, s)))
        stages.append((s, OFF_GATE, OFF_U, functools.partial(store_gate, s)))
    pending = proj(*stages[0][:3])
    for n, stage in enumerate(stages):
        ahead = proj(*stages[n + 1][:3]) if n + 1 < len(stages) else None
        stage[3](pending)
        pending = ahead


def _prompt_in(x2d, xs, conv_rows, layer, prm, later_weights, *, batch, seq, tm):
    nt = seq // tm
    rows = batch * seq
    d = x2d.shape[1]
    n = xs.shape[0]
    steps = batch * nt
    tile = lambda t: jnp.minimum(t, steps - 1)
    row_spec = lambda w: pl.BlockSpec((tm, w), lambda t: (tile(t), 0))
    dec_spec = lambda w: pl.BlockSpec((n, w), lambda t: (0, 0))
    conv_row = lambda j: pl.BlockSpec((None, None, n, QKV), lambda t: (layer, j, 0, 0))
    names = ("gmix", "win", "cw", "alog", "dtb", "vg", "ws", "bsb")
    slab_in, slab_out, slab_shape = [], [], []
    for w in later_weights:
        _, wr, wc = w.shape
        assert wr % (steps * 2 * SUBLANES) == 0
        slab_in.append(pl.BlockSpec((None, wr // steps, wc), lambda t: (layer, tile(t), 0)))
        slab_out.append(pl.BlockSpec((wr // steps, wc), lambda t: (tile(t), 0)))
        slab_shape.append(jax.ShapeDtypeStruct((wr, wc), BF16))
    dec_shape = lambda w: jax.ShapeDtypeStruct((n, w), F32)
    outs = pl.pallas_call(
        functools.partial(_prompt_in_kernel, tm=tm, nt=nt, n_tiles=steps, n_cast=len(later_weights)),
        grid=(steps + 1,),
        in_specs=[row_spec(d)] + [_layer_spec(prm[k], layer) for k in names] + slab_in
                 + [dec_spec(d), _layer_spec(prm["ws0"], layer), _layer_spec(prm["bs0"], layer)]
                 + [conv_row(j) for j in range(CONV_W - 1)],
        out_specs=[row_spec(QKV), row_spec(W_A), row_spec(AB_W), row_spec(W_B),
                   pl.BlockSpec((CONV_W - 1, batch, QKV), lambda t: (0, 0, 0))] + slab_out
                  + [dec_spec(QKV), dec_spec(W_A), dec_spec(AB_W), dec_spec(W_B), dec_spec(W_B),
                     pl.BlockSpec((CONV_W - 1, n, QKV), lambda t: (0, 0, 0))],
        out_shape=[jax.ShapeDtypeStruct((rows, QKV), F32), jax.ShapeDtypeStruct((rows, W_A), F32),
                   jax.ShapeDtypeStruct((rows, AB_W), F32), jax.ShapeDtypeStruct((rows, W_B), F32),
                   jax.ShapeDtypeStruct((CONV_W - 1, batch, QKV), F32)] + slab_shape
                  + [dec_shape(QKV), dec_shape(W_A), dec_shape(AB_W), dec_shape(W_B), dec_shape(W_B),
                     jax.ShapeDtypeStruct((CONV_W - 1, n, QKV), F32)],
        scratch_shapes=[pltpu.VMEM((SUBLANES, QKV), F32)],
        compiler_params=pltpu.CompilerParams(dimension_semantics=("arbitrary",),
                                             vmem_limit_bytes=VMEM_LIMIT),
        name="prompt_in",
    )(x2d, *[prm[k] for k in names], *later_weights, xs, prm["ws0"], prm["bs0"], conv_rows, conv_rows, conv_rows)
    n_cast = len(later_weights)
    return outs[:5], outs[5:5 + n_cast], outs[5 + n_cast:]


def _pair_blockdiag(m, top, bottom):
    zero = jnp.zeros_like(m)
    return jnp.concatenate([jnp.where(top, m, zero), jnp.where(bottom, m, zero)], axis=0)


def _unit_lower_inverses(lpairs, ri, cj, left):
    c = lpairs[0].shape[0]
    eye = (ri == cj).astype(F32)
    right = jnp.logical_not(left)
    lbs = [lp.astype(BF16) for lp in lpairs]
    xs = None
    s = 1
    while s < c:
        mask = ((ri // (2 * s)) == (cj // (2 * s))) & (((ri // s) % 2) == 1) & (((cj // s) % 2) == 0)
        if xs is None:
            xs = [eye - jnp.where(mask, lp, 0.0) for lp in lpairs]
        else:
            xbs = [x.astype(BF16) for x in xs]
            ys = [jnp.dot(xb, _pair_blockdiag(lb, left & mask, right & mask), preferred_element_type=F32)
                  for xb, lb in zip(xbs, lbs)]
            zs = [jnp.dot(y.astype(BF16), _pair_blockdiag(xb, left, right), preferred_element_type=F32)
                  for y, xb in zip(ys, xbs)]
            xs = [x - z for x, z in zip(xs, zs)]
        s *= 2
    return xs


def _delta_kernel(qkv_ref, gb_ref, gate_ref, og_ref, oa_ref, sout_ref,
                  s_ref, gcs_ref, u_ref, w_ref, qe_ref, qk_ref, kdt_ref, *, nb, tm, nt):
    i = pl.program_id(1)
    c = DN_CHUNK

    @pl.when(i == 0)
    def _():
        s_ref[...] = jnp.zeros(s_ref.shape, F32)

    assert 2 * c == LANES and H_A % 2 == 0
    ri = lax.broadcasted_iota(jnp.int32, (c, LANES), 0)
    lane = lax.broadcasted_iota(jnp.int32, (c, LANES), 1)
    left = lane < c
    cj = jnp.where(left, lane, lane - c)
    lower = ri >= cj
    strict = ri > cj
    og = og_ref[...]
    chains = [(b, hh) for b in range(nb) for hh in range(H_A)]
    pairs = [(b, p) for b in range(nb) for p in range(H_A // 2)]
    qcols = lambda hh: slice(hh * DK, (hh + 1) * DK)
    kcols = lambda hh: slice(H_A * DK + hh * DK, H_A * DK + (hh + 1) * DK)
    vcols = lambda hh: slice(2 * H_A * DK + hh * DV, 2 * H_A * DK + (hh + 1) * DV)
    pcols = lambda p: slice(p * LANES, (p + 1) * LANES)
    cat = jnp.concatenate
    bf = lambda a: a.astype(BF16)
    zero_bf = jnp.zeros((c, LANES), BF16)

    def prepare(n, carry):
        chunk_ids = [n * PREP_CHUNKS + e for e in range(PREP_CHUNKS) for _ in range(nb)]
        units = [(b, pl.ds(pl.multiple_of((n * PREP_CHUNKS + e) * c, c), c))
                 for e in range(PREP_CHUNKS) for b in range(nb)]
        chains = [(ui, hh) for ui in range(len(units)) for hh in range(H_A)]
        pairs = [(ui, p) for ui in range(len(units)) for p in range(H_A // 2)]
        gbcs, gcss, gcsts = [], [], []
        for b, rows in units:
            gbc = gb_ref[b, rows, :]
            gcs = gbc
            sh = 1
            while sh < c:
                gcs = gcs + jnp.where(ri >= sh, pltpu.roll(gcs, sh, axis=0), 0.0)
                sh *= 2
            gcs_ref[b, rows, :] = gcs
            gbcs.append(gbc)
            gcss.append(gcs)
            gcsts.append(gcs.T)
        qs, ks, vs, kbs, betas, g_cols = {}, {}, {}, {}, {}, {}
        for ui, hh in chains:
            b, rows = units[ui]
            qs[ui, hh] = qkv_ref[b, rows, qcols(hh)]
            ks[ui, hh] = qkv_ref[b, rows, kcols(hh)]
            vs[ui, hh] = qkv_ref[b, rows, vcols(hh)]
            g_cols[ui, hh] = jnp.broadcast_to(gcss[ui][:, hh:hh + 1], (c, LANES))
            betas[ui, hh] = jnp.broadcast_to(gbcs[ui][:, H_A + hh:H_A + hh + 1], (c, LANES))
            kbs[ui, hh] = ks[ui, hh] * betas[ui, hh]
        decays, kqs = [], []
        for ui, p in pairs:
            b, rows = units[ui]
            h0, h1 = 2 * p, 2 * p + 1
            g_col = jnp.where(left, g_cols[ui, h0], g_cols[ui, h1])
            g_row = jnp.broadcast_to(cat([gcsts[ui][h0:h0 + 1, :], gcsts[ui][h1:h1 + 1, :]], axis=1), (c, LANES))
            decays.append(jnp.where(lower, jnp.exp(jnp.where(lower, g_col - g_row, 0.0)), 0.0))
            lhs = cat([cat([bf(kbs[ui, h0]), bf(kbs[ui, h1])], axis=1),
                       cat([bf(qs[ui, h0]), bf(qs[ui, h1])], axis=1)], axis=0)
            rhs = cat([cat([bf(ks[ui, h0]), zero_bf], axis=1), cat([zero_bf, bf(ks[ui, h1])], axis=1)], axis=0)
            kqs.append(lax.dot_general(lhs, rhs, (((1,), (1,)), ((), ())), preferred_element_type=F32))
        lpairs = [jnp.where(strict, kq[:c] * decay, 0.0) for kq, decay in zip(kqs, decays)]
        ts = _unit_lower_inverses(lpairs, ri, cj, left)
        for j, (ui, p) in enumerate(pairs):
            b, rows = units[ui]
            h0, h1 = 2 * p, 2 * p + 1
            vb0 = bf(vs[ui, h0] * betas[ui, h0])
            vb1 = bf(vs[ui, h1] * betas[ui, h1])
            kg0 = bf(kbs[ui, h0] * jnp.exp(g_cols[ui, h0]))
            kg1 = bf(kbs[ui, h1] * jnp.exp(g_cols[ui, h1]))
            rhs = cat([cat([vb0, zero_bf, kg0, zero_bf], axis=1), cat([zero_bf, vb1, zero_bf, kg1], axis=1)], axis=0)
            uw = jnp.dot(bf(ts[j]), rhs, preferred_element_type=F32)
            u_ref[b, rows, qcols(h0)] = uw[:, 0:DV]
            u_ref[b, rows, qcols(h1)] = uw[:, DV:2 * DV]
            w_ref[b, rows, qcols(h0)] = uw[:, 2 * DV:3 * DV].astype(BF16)
            w_ref[b, rows, qcols(h1)] = uw[:, 3 * DV:4 * DV].astype(BF16)
            qk_ref[b, rows, pcols(p)] = (kqs[j][c:] * decays[j]).astype(BF16)
        for ui, hh in chains:
            b, rows = units[ui]
            g_col = g_cols[ui, hh]
            qe_ref[b, rows, qcols(hh)] = (qs[ui, hh] * jnp.exp(g_col)).astype(BF16)
            k_dec = ks[ui, hh] * jnp.exp(g_col[c - 1:c, :] - g_col)
            kdt_ref[b, hh, pl.ds(pl.multiple_of(chunk_ids[ui] * DK, DK), DK), :] = k_dec.T.astype(BF16)
        return carry

    def recur(n, carry):
        rows = pl.ds(pl.multiple_of(n * c, c), c)
        gcss = [gcs_ref[b, rows, :] for b in range(nb)]
        s_olds = [s_ref[b, hh] for b, hh in chains]
        s_bfs = [s.astype(BF16) for s in s_olds]
        wqs = [jnp.dot(cat([w_ref[b, rows, qcols(hh)], qe_ref[b, rows, qcols(hh)]], axis=0), s_bf,
                       preferred_element_type=F32) for (b, hh), s_bf in zip(chains, s_bfs)]
        v_bfs = [(u_ref[b, rows, qcols(hh)] - wq[:c]).astype(BF16) for (b, hh), wq in zip(chains, wqs)]
        kdt_rows = pl.ds(pl.multiple_of(n * DK, DK), DK)
        upds = [jnp.dot(kdt_ref[b, hh, kdt_rows, :], v_bf, preferred_element_type=F32)
                for (b, hh), v_bf in zip(chains, v_bfs)]
        for j, (b, hh) in enumerate(chains):
            g_last = jnp.broadcast_to(gcss[b][c - 1:c, hh:hh + 1], (1, LANES))
            s_ref[b, hh] = s_olds[j] * jnp.exp(g_last) + upds[j]
        qkvs = []
        for j, (b, p) in enumerate(pairs):
            v0, v1 = v_bfs[2 * j], v_bfs[2 * j + 1]
            rhs = cat([cat([v0, zero_bf], axis=1), cat([zero_bf, v1], axis=1)], axis=0)
            both = jnp.dot(qk_ref[b, rows, pcols(p)], rhs, preferred_element_type=F32)
            qkvs += [both[:, :DV], both[:, DV:]]
        for j, (b, hh) in enumerate(chains):
            o = wqs[j][c:] + qkvs[j]
            oa_ref[b, rows, qcols(hh)] = _rmsnorm(o, og) * _silu(gate_ref[b, rows, qcols(hh)])
        return carry

    lax.fori_loop(0, tm // (c * PREP_CHUNKS), prepare, 0)
    lax.fori_loop(0, tm // c, recur, 0, unroll=True)

    @pl.when(i == nt - 1)
    def _():
        sout_ref[...] = s_ref[...]


def _delta_prompt(qkv, gb, gate, layer, prm, *, nb, tm):
    batch, seq, _ = qkv.shape
    nt = seq // tm
    blk = lambda w: pl.BlockSpec((nb, tm, w), lambda b, i: (b, i, 0))
    return pl.pallas_call(
        functools.partial(_delta_kernel, nb=nb, tm=tm, nt=nt),
        grid=(batch // nb, nt),
        in_specs=[blk(QKV), blk(AB_W), blk(W_A), _layer_spec(prm["og"], layer)],
        out_specs=[blk(W_A), pl.BlockSpec((nb, H_A, DK, DV), lambda b, i: (b, 0, 0, 0))],
        out_shape=[jax.ShapeDtypeStruct((batch, seq, W_A), F32),
                   jax.ShapeDtypeStruct((batch, H_A, DK, DV), F32)],
        scratch_shapes=[pltpu.VMEM((nb, H_A, DK, DV), F32),
                        pltpu.VMEM((nb, tm, AB_W), F32),
                        pltpu.VMEM((nb, tm, W_A), F32),
                        pltpu.VMEM((nb, tm, W_A), BF16),
                        pltpu.VMEM((nb, tm, W_A), BF16),
                        pltpu.VMEM((nb, tm, H_A // 2 * LANES), BF16),
                        pltpu.VMEM((nb, H_A, tm // DN_CHUNK * DK, DN_CHUNK), BF16)],
        compiler_params=pltpu.CompilerParams(dimension_semantics=("arbitrary", "arbitrary"),
                                             vmem_limit_bytes=VMEM_LIMIT),
        name="delta_prompt",
    )(qkv, gb, gate, prm["og"])


def _out_ffn_kernel(x_ref, oa_ref, ob_ref, xs_ref, oas_ref, obs_ref, wo_ref, gf_ref, wup_ref, wdn_ref, gl_ref,
                    out_ref, outs_ref, *, n_tiles, final_norm):
    step = pl.program_id(0)
    weights = (wo_ref, gf_ref, wup_ref, wdn_ref, gl_ref)

    @pl.when(step < n_tiles)
    def _():
        _out_ffn_body(x_ref, oa_ref, ob_ref, *weights, out_ref, final_norm=final_norm)

    @pl.when(step == n_tiles)
    def _():
        _out_ffn_body(xs_ref, oas_ref, obs_ref, *weights, outs_ref, final_norm=final_norm)


def _out_ffn_body(x_ref, oa_ref, ob_ref, wo_ref, gf_ref, wup_ref, wdn_ref, gl_ref, out_ref, *, final_norm):
    y = jnp.dot(oa_ref[...].astype(BF16), wo_ref[0:W_A, :], preferred_element_type=F32)
    y = y + jnp.dot(ob_ref[...].astype(BF16), wo_ref[W_A:W_A + W_B, :], preferred_element_type=F32)
    x1 = x_ref[...] + y
    h = _rmsnorm(x1, gf_ref[...]).astype(BF16)
    d_ff = wup_ref.shape[1]
    ffn = None
    for j in range(d_ff // FF_BLOCK):
        cols = slice(j * FF_BLOCK, (j + 1) * FF_BLOCK)
        a = jnp.dot(h, wup_ref[:, cols], preferred_element_type=F32)
        a = jnp.square(jnp.maximum(a, 0.0)).astype(BF16)
        part = jnp.dot(a, wdn_ref[cols, :], preferred_element_type=F32)
        ffn = part if ffn is None else ffn + part
    x2 = x1 + ffn
    if final_norm:
        x2 = _rmsnorm(x2, gl_ref[...])
    out_ref[...] = x2


def _out_ffn(x2d, oa, ob, xs, oa_s, ob_s, layer, prm, weights, *, tm, final_norm):
    rows, d = x2d.shape
    n = xs.shape[0]
    n_tiles = rows // tm
    wo, wup, wdn = weights
    row_spec = lambda w: pl.BlockSpec((tm, w), lambda i: (jnp.minimum(i, n_tiles - 1), 0))
    dec_spec = lambda w: pl.BlockSpec((n, w), lambda i: (0, 0))
    once = lambda w: pl.BlockSpec(w.shape, lambda i: (0, 0), pipeline_mode=pl.Buffered(1))
    return pl.pallas_call(
        functools.partial(_out_ffn_kernel, n_tiles=n_tiles, final_norm=final_norm),
        grid=(n_tiles + 1,),
        in_specs=[row_spec(d), row_spec(W_A), row_spec(W_B), dec_spec(d), dec_spec(W_A), dec_spec(W_B),
                  once(wo), _layer_spec(prm["gffn"], layer), once(wup), once(wdn),
                  pl.BlockSpec((1, d), lambda i: (0, 0))],
        out_specs=[row_spec(d), dec_spec(d)],
        out_shape=[jax.ShapeDtypeStruct((rows, d), F32), jax.ShapeDtypeStruct((n, d), F32)],
        compiler_params=pltpu.CompilerParams(dimension_semantics=("arbitrary",),
                                             vmem_limit_bytes=VMEM_LIMIT),
        name="out_ffn",
    )(x2d, oa, ob, xs, oa_s, ob_s, wo, prm["gffn"], wup, wdn, prm["gl"])


def _sample_in_kernel(x_ref, g_ref, win_ref, cw_ref, alog_ref, dtb_ref, vg_ref, ws0_ref, bs0_ref,
                      c0_ref, c1_ref, c2_ref, qkv_ref, gate_ref, gb_ref, ob_ref, vb_ref, cnew_ref):
    h = _rmsnorm(x_ref[...], g_ref[...]).astype(BF16)
    p = jnp.dot(h, win_ref[...], preferred_element_type=F32)
    pq = p[:, :QKV]
    cw = cw_ref[...]
    c1 = c1_ref[...]
    c2 = c2_ref[...]
    y = c0_ref[...] * cw[0:1, :] + c1 * cw[1:2, :] + c2 * cw[2:3, :] + pq * cw[3:4, :]
    cnew_ref[0] = c1
    cnew_ref[1] = c2
    cnew_ref[2] = pq
    for j in range(QKV // CONV_BLK):
        cols = slice(j * CONV_BLK, (j + 1) * CONV_BLK)
        qkv_ref[:, cols] = _qkv_activation(y[:, cols], j)
    gate_ref[...] = p[:, OFF_GATE:OFF_U]
    gb_ref[...] = _decay_beta(p[:, OFF_AB:PROJ_PAD], alog_ref[...], dtb_ref[...])
    vb = _rmsnorm(p[:, OFF_V:OFF_AB], vg_ref[...])
    vb_ref[...] = vb
    ob_ref[...] = p[:, OFF_U:OFF_V] * (vb * ws0_ref[...] + bs0_ref[...])


def _delta_step_kernel(*refs, tb, chained):
    if chained:
        qkv_ref, gb_ref, gate_ref, og_ref, s_ref, _, oa_ref, snew_ref = refs
    else:
        qkv_ref, gb_ref, gate_ref, og_ref, s_ref, oa_ref, snew_all_ref = refs
        snew_ref = snew_all_ref.at[0]
        snew_all_ref[1:] = jnp.zeros((snew_all_ref.shape[0] - 1,) + snew_all_ref.shape[1:], F32)
    og = og_ref[...]
    gb = gb_ref[...]
    spread = (lax.broadcasted_iota(jnp.int32, (tb, tb * DV), 1) // DV
              == lax.broadcasted_iota(jnp.int32, (tb, tb * DV), 0)).astype(BF16)
    for hh in range(H_A):
        q = qkv_ref[:, hh * DK:(hh + 1) * DK]
        k = qkv_ref[:, H_A * DK + hh * DK:H_A * DK + (hh + 1) * DK]
        v = qkv_ref[:, 2 * H_A * DK + hh * DV:2 * H_A * DK + (hh + 1) * DV]
        q_cols = jnp.dot(q.T.astype(BF16), spread, preferred_element_type=F32)
        k_cols = jnp.dot(k.T.astype(BF16), spread, preferred_element_type=F32)
        decay = jnp.exp(gb[:, hh:hh + 1])
        beta = gb[:, H_A + hh:H_A + hh + 1]
        o_rows = []
        for t in range(tb):
            s = s_ref[t, hh] * decay[t:t + 1, :]
            k_col = k_cols[:, t * DV:(t + 1) * DV]
            q_col = q_cols[:, t * DV:(t + 1) * DV]
            kv = jnp.sum(s * k_col, axis=0, keepdims=True)
            delta = (v[t:t + 1, :] - kv) * beta[t:t + 1, :]
            s = s + k_col * delta
            snew_ref[t, hh] = s
            o_rows.append(jnp.sum(s * q_col, axis=0, keepdims=True))
        o = jnp.concatenate(o_rows, axis=0)
        gate = gate_ref[:, hh * DV:(hh + 1) * DV]
        oa_ref[:, hh * DV:(hh + 1) * DV] = _rmsnorm(o, og) * _silu(gate)


def _delta_step(qkv, gb, gate, state_all, new_all, layer, prm, *, tb):
    n = qkv.shape[0]
    depth = state_all.shape[0]
    row_spec = lambda w: pl.BlockSpec((tb, w), lambda i: (i, 0))
    st_spec = pl.BlockSpec((None, tb, H_A, DK, DV), lambda i: (layer, i, 0, 0, 0))
    chained = new_all is not None
    in_specs = [row_spec(QKV), row_spec(AB_W), row_spec(W_A), _layer_spec(prm["og"], layer), st_spec]
    args = [qkv, gb, gate, prm["og"], state_all]
    if chained:
        in_specs.append(pl.BlockSpec(memory_space=pl.ANY))
        args.append(new_all)
        new_spec = st_spec
    else:
        assert layer == 0
        new_spec = pl.BlockSpec((depth, tb, H_A, DK, DV), lambda i: (0, i, 0, 0, 0))
    return pl.pallas_call(
        functools.partial(_delta_step_kernel, tb=tb, chained=chained),
        grid=(n // tb,),
        in_specs=in_specs,
        out_specs=[row_spec(W_A), new_spec],
        out_shape=[jax.ShapeDtypeStruct((n, W_A), F32), jax.ShapeDtypeStruct(state_all.shape, F32)],
        input_output_aliases={len(args) - 1: 1} if chained else {},
        compiler_params=pltpu.CompilerParams(dimension_semantics=("arbitrary",),
                                             vmem_limit_bytes=VMEM_LIMIT),
        name="delta_step",
    )(*args)


def _reorder_win_kernel(wt_ref, out_ref):
    off_a = QKV + W_A
    off_u = off_a + 2 * H_A
    src_of = lambda dst: dst if dst < off_a else dst + 2 * H_A
    for dst in range(0, OFF_AB, LANES):
        src = src_of(dst)
        out_ref[:, dst:dst + LANES] = wt_ref[src:src + LANES, :].T.astype(BF16)
    ab = jnp.concatenate([wt_ref[off_a:off_u, :], jnp.zeros((AB_W - 2 * H_A, wt_ref.shape[1]), F32)], axis=0)
    out_ref[:, OFF_AB:PROJ_PAD] = ab.T.astype(BF16)


def _reorder_win(wt, *, cols):
    depth, width, d = wt.shape
    return pl.pallas_call(
        _reorder_win_kernel,
        grid=(depth, d // cols),
        in_specs=[pl.BlockSpec((None, width, cols), lambda l, i: (l, 0, i))],
        out_specs=pl.BlockSpec((None, cols, PROJ_PAD), lambda l, i: (l, i, 0)),
        out_shape=jax.ShapeDtypeStruct((depth, d, PROJ_PAD), BF16),
        compiler_params=pltpu.CompilerParams(dimension_semantics=("arbitrary", "arbitrary"),
                                             vmem_limit_bytes=VMEM_LIMIT),
        name="reorder_win",
    )(wt)


def _pick_tile(seq, candidates):
    for tm in candidates:
        if seq % tm == 0:
            return tm
    raise ValueError(f"prompt length must be a multiple of {candidates[-1]}")


def kernel(x_prompt, x_sample, state_delta, state_conv, norm_mix_g, w_in, conv_w, A_log, dt_bias, o_norm_g,
           v_norm_g, w_s, b_s, w_o, norm_ffn_g, w_up, w_down, norm_f_g):
    batch, seq, d = x_prompt.shape
    n_dec, dec_seq, _ = x_sample.shape
    depth = w_in.shape[0]
    assert dec_seq == 1 and seq % GM_CHUNK == 0 and n_dec % SUBLANES == 0
    assert seq % PROJ_ROWS == 0
    tm = _pick_tile(seq, (1024, 512, 256, 128))
    ptm = _pick_tile(seq, (2 * PROJ_ROWS, PROJ_ROWS))
    tb = _pick_tile(n_dec, (4 * SUBLANES, 2 * SUBLANES, SUBLANES))
    nb = next(n for n in (8, 4, 2, 1) if batch % n == 0)
    dtm = GM_CHUNK

    lane_pad = lambda v: jnp.pad(v.astype(F32), ((0, 0), (0, AB_W - v.shape[1]))).reshape(depth, 1, AB_W)
    prm = {
        "win": _reorder_win(jnp.swapaxes(w_in, 1, 2), cols=256 if d % 256 == 0 else LANES),
        "gmix": norm_mix_g.reshape(depth, 1, d),
        "gffn": norm_ffn_g.reshape(depth, 1, d),
        "gl": norm_f_g.reshape(1, d),
        "cw": conv_w,
        "alog": lane_pad(A_log),
        "dtb": lane_pad(dt_bias),
        "vg": v_norm_g.reshape(depth, 1, W_B),
        "og": o_norm_g.reshape(depth, 1, DV),
        "ws": w_s,
        "bsb": jnp.broadcast_to(b_s[:, :, :, None], b_s.shape + (DH_B,)).astype(F32),
        "ws0": jnp.repeat(w_s[:, :, 0, 0], DH_B, axis=-1).reshape(depth, 1, W_B),
        "bs0": jnp.repeat(b_s[:, :, 0], DH_B, axis=-1).reshape(depth, 1, W_B),
    }
    conv_rows = jnp.swapaxes(state_conv, 1, 2)

    xp = x_prompt.reshape(batch * seq, d)
    xs = x_sample.reshape(n_dec, d)
    dp, cp, cs, vs = [], [], [], []
    new_state = None
    for l in range(depth):
        last = l == depth - 1
        (qkv, gate, gb, ob, ctail), ffn_w, (qkv_s, gate_s, gb_s, ob_s, vb_s, cnew) = _prompt_in(
            xp, xs, conv_rows, l, prm, (w_o, w_up, w_down), batch=batch, seq=seq, tm=ptm)
        oa, s_fin = _delta_prompt(qkv.reshape(batch, seq, QKV), gb.reshape(batch, seq, AB_W),
                                  gate.reshape(batch, seq, W_A), l, prm, nb=nb, tm=dtm)
        dp.append(s_fin)
        cp.append(ctail)

        oa_s, new_state = _delta_step(qkv_s, gb_s, gate_s, state_delta, new_state, l, prm,
                                      tb=tb // depth if new_state is None else tb)
        xp, xs = _out_ffn(xp, oa.reshape(batch * seq, W_A), ob, xs, oa_s, ob_s, l, prm, ffn_w, tm=tm,
                          final_norm=last)
        cs.append(cnew)
        vs.append(vb_s.reshape(n_dec, 1, W_B))

    return (xp.reshape(batch, seq, d), xs.reshape(n_dec, 1, d), jnp.stack(dp), jnp.swapaxes(jnp.stack(cp), 1, 2),
            new_state, jnp.swapaxes(jnp.stack(cs), 1, 2), jnp.stack(vs))
```

```python
import functools

import jax
import jax.numpy as jnp
from jax import lax
from jax.experimental import pallas as pl
from jax.experimental.pallas import tpu as pltpu

F32 = jnp.float32
BF16 = jnp.bfloat16

H_A = 4
DK = 128
DV = 128
W_A = H_A * DV
QKV = 2 * H_A * DK + H_A * DV
CONV_W = 4
DN_CHUNK = 64
H_B = 4
DH_B = 128
W_B = H_B * DH_B
GM_CHUNK = 128
EPS = 1e-6

LANES = 128
SUBLANES = 8
AB_W = LANES
OFF_GATE = QKV
OFF_U = OFF_GATE + W_A
OFF_V = OFF_U + W_B
OFF_AB = OFF_V + W_B
PROJ_PAD = OFF_AB + AB_W
CONV_BLK = H_A * DK

VMEM_LIMIT = 60 * 1024 * 1024
FF_BLOCK = 1024
PROJ_ROWS = 256
PREP_CHUNKS = 2


def _rmsnorm(x, g):
    return x * lax.rsqrt(jnp.mean(x * x, axis=-1, keepdims=True) + EPS) * g


def _silu(x):
    return x * jax.nn.sigmoid(x)


def _softplus(x):
    return jnp.maximum(x, 0.0) + jnp.log1p(jnp.exp(-jnp.abs(x)))


def _layer_spec(arr, layer):
    nd = arr.ndim - 1
    return pl.BlockSpec((None,) + arr.shape[1:], lambda *_: (layer,) + (0,) * nd,
                        pipeline_mode=pl.Buffered(1))


def _qkv_activation(y, j):
    a = _silu(y)
    if j >= 2:
        return a
    blocks = []
    for hh in range(H_A):
        blk = a[:, hh * DK:(hh + 1) * DK]
        inv = lax.rsqrt(jnp.sum(blk * blk, axis=-1, keepdims=True) + EPS)
        if j == 0:
            inv = inv * (DK ** -0.5)
        blocks.append(blk * inv)
    return jnp.concatenate(blocks, axis=1)


def _decay_beta(ab, alog, dtb):
    lane = lax.broadcasted_iota(jnp.int32, ab.shape, 1)
    gdec = -jnp.exp(alog) * _softplus(ab + dtb)
    beta = jax.nn.sigmoid(ab)
    return jnp.where(lane < H_A, gdec, beta)


def _prompt_in_kernel(x_ref, g_ref, win_ref, cw_ref, alog_ref, dtb_ref, vg_ref, ws_ref, bsb_ref, *rest,
                      tm, nt, n_tiles, n_cast):
    cast_in, rest = rest[:n_cast], rest[n_cast:]
    dec_in, rest = rest[:6], rest[6:]
    prompt_out, rest = rest[:5], rest[5:]
    cast_out, rest = rest[:n_cast], rest[n_cast:]
    dec_out, (carry_ref,) = rest[:6], rest[6:]
    for src, dst in zip(cast_in, cast_out):
        dst[...] = src[...].astype(BF16)
    step = pl.program_id(0)

    @pl.when(step < n_tiles)
    def _():
        _prompt_tile(x_ref, g_ref, win_ref, cw_ref, alog_ref, dtb_ref, vg_ref, ws_ref, bsb_ref, *prompt_out,
                     carry_ref, tm=tm, i=lax.rem(step, nt), b=step // nt)

    @pl.when(step == n_tiles)
    def _():
        xs_ref, ws0_ref, bs0_ref, c0_ref, c1_ref, c2_ref = dec_in
        _sample_in_kernel(xs_ref, g_ref, win_ref, cw_ref, alog_ref, dtb_ref, vg_ref, ws0_ref, bs0_ref,
                          c0_ref, c1_ref, c2_ref, *dec_out)


def _prompt_tile(x_ref, g_ref, win_ref, cw_ref, alog_ref, dtb_ref, vg_ref, ws_ref, bsb_ref,
                 qkv_ref, gate_ref, gb_ref, ob_ref, ctail_ref, carry_ref, *, tm, i, b):
    sm = PROJ_ROWS
    n_sub = tm // sm

    @pl.when(i == 0)
    def _():
        carry_ref[...] = jnp.zeros(carry_ref.shape, F32)

    sub = lax.broadcasted_iota(jnp.int32, (SUBLANES, CONV_BLK), 0)
    ri = lax.broadcasted_iota(jnp.int32, (GM_CHUNK, GM_CHUNK), 0)
    ci = lax.broadcasted_iota(jnp.int32, (GM_CHUNK, GM_CHUNK), 1)
    nc = sm // GM_CHUNK
    hs, tails, mixes = {}, {}, {}

    def proj(s, lo, hi):
        if s not in hs:
            hs[s] = _rmsnorm(x_ref[s * sm:(s + 1) * sm, :], g_ref[...]).astype(BF16)
        return jnp.dot(hs[s], win_ref[:, lo:hi], preferred_element_type=F32)

    def conv_act(s, j, pj):
        rows = slice(s * sm, (s + 1) * sm)
        cols = slice(j * CONV_BLK, (j + 1) * CONV_BLK)
        cw = cw_ref[:, cols]
        prev = carry_ref[:, cols] if s == 0 else tails[s - 1, j]

        def shift(x, x_prev, k):
            sh = pltpu.roll(x, k, axis=0)
            head = jnp.where(sub < k, pltpu.roll(x_prev, k, axis=0), sh[0:SUBLANES])
            return jnp.concatenate([head, sh[SUBLANES:]], axis=0)

        w0, w1, w2, w3 = (cw[k:k + 1, :] for k in range(CONV_W))
        pj1 = shift(pj, prev, 1)
        far = pj * w1 + pj1 * w0
        far_prev = prev * w1 + pltpu.roll(prev, 1, axis=0) * w0
        y = (pj * w3 + pj1 * w2) + shift(far, far_prev, 2)
        tails[s, j] = pj[sm - SUBLANES:sm, :]
        if s == n_sub - 1:
            carry_ref[:, cols] = tails[s, j]
            for k in range(CONV_W - 1):
                row = sm - (CONV_W - 1) + k
                ctail_ref[k, pl.ds(b, 1), cols] = pj[row:row + 1, :]
        qkv_ref[rows, cols] = _qkv_activation(y, j)

    def gmlp_mix(s, pv):
        vb = _rmsnorm(pv, vg_ref[...]).astype(BF16)
        mixes[s] = []
        for hh in range(H_B):
            wm = jnp.where(ri >= ci, ws_ref[hh], 0.0).astype(BF16)
            cols = slice(hh * DH_B, (hh + 1) * DH_B)
            rhs = jnp.concatenate([vb[c * GM_CHUNK:(c + 1) * GM_CHUNK, cols] for c in range(nc)], axis=1)
            mixes[s].append(jnp.dot(wm, rhs, preferred_element_type=F32))

    def gmlp_gate(s, pu):
        for hh in range(H_B):
            cols = slice(hh * DH_B, (hh + 1) * DH_B)
            bias = bsb_ref[hh]
            for c in range(nc):
                src = slice(c * GM_CHUNK, (c + 1) * GM_CHUNK)
                dst = slice(s * sm + c * GM_CHUNK, s * sm + (c + 1) * GM_CHUNK)
                ob_ref[dst, cols] = pu[src, cols] * (mixes[s][hh][:, c * DH_B:(c + 1) * DH_B] + bias)

    def store_gb(s, pab):
        gb_ref[s * sm:(s + 1) * sm, :] = _decay_beta(pab, alog_ref[...], dtb_ref[...])

    def store_gate(s, pg):
        gate_ref[s * sm:(s + 1) * sm, :] = pg

    stages = []
    for s in range(n_sub):
        for j in range(QKV // CONV_BLK):
            stages.append((s, j * CONV_BLK, (j + 1) * CONV_BLK, functools.partial(conv_act, s, j)))
    for s in range(n_sub):
        stages.append((s, OFF_V, OFF_AB, functools.partial(gmlp_mix, s)))
        stages.append((s, OFF_U, OFF_V, functools.partial(gmlp_gate, s)))
        stages.append((s, OFF_AB, PROJ_PAD, functools.partial(store_gb, s)))
        stages.append((s, OFF_GATE, OFF_U, functools.partial(store_gate, s)))
    pending = proj(*stages[0][:3])
    for n, stage in enumerate(stages):
        ahead = proj(*stages[n + 1][:3]) if n + 1 < len(stages) else None
        stage[3](pending)
        pending = ahead


def _prompt_in(x2d, xs, conv_rows, layer, prm, later_weights, *, batch, seq, tm):
    nt = seq // tm
    rows = batch * seq
    d = x2d.shape[1]
    n = xs.shape[0]
    steps = batch * nt
    tile = lambda t: jnp.minimum(t, steps - 1)
    row_spec = lambda w: pl.BlockSpec((tm, w), lambda t: (tile(t), 0))
    dec_spec = lambda w: pl.BlockSpec((n, w), lambda t: (0, 0))
    conv_row = lambda j: pl.BlockSpec((None, None, n, QKV), lambda t: (layer, j, 0, 0))
    names = ("gmix", "win", "cw", "alog", "dtb", "vg", "ws", "bsb")
    slab_in, slab_out, slab_shape = [], [], []
    for w in later_weights:
        _, wr, wc = w.shape
        assert wr % (steps * 2 * SUBLANES) == 0
        slab_in.append(pl.BlockSpec((None, wr // steps, wc), lambda t: (layer, tile(t), 0)))
        slab_out.append(pl.BlockSpec((wr // steps, wc), lambda t: (tile(t), 0)))
        slab_shape.append(jax.ShapeDtypeStruct((wr, wc), BF16))
    dec_shape = lambda w: jax.ShapeDtypeStruct((n, w), F32)
    outs = pl.pallas_call(
        functools.partial(_prompt_in_kernel, tm=tm, nt=nt, n_tiles=steps, n_cast=len(later_weights)),
        grid=(steps + 1,),
        in_specs=[row_spec(d)] + [_layer_spec(prm[k], layer) for k in names] + slab_in
                 + [dec_spec(d), _layer_spec(prm["ws0"], layer), _layer_spec(prm["bs0"], layer)]
                 + [conv_row(j) for j in range(CONV_W - 1)],
        out_specs=[row_spec(QKV), row_spec(W_A), row_spec(AB_W), row_spec(W_B),
                   pl.BlockSpec((CONV_W - 1, batch, QKV), lambda t: (0, 0, 0))] + slab_out
                  + [dec_spec(QKV), dec_spec(W_A), dec_spec(AB_W), dec_spec(W_B), dec_spec(W_B),
                     pl.BlockSpec((CONV_W - 1, n, QKV), lambda t: (0, 0, 0))],
        out_shape=[jax.ShapeDtypeStruct((rows, QKV), F32), jax.ShapeDtypeStruct((rows, W_A), F32),
                   jax.ShapeDtypeStruct((rows, AB_W), F32), jax.ShapeDtypeStruct((rows, W_B), F32),
                   jax.ShapeDtypeStruct((CONV_W - 1, batch, QKV), F32)] + slab_shape
                  + [dec_shape(QKV), dec_shape(W_A), dec_shape(AB_W), dec_shape(W_B), dec_shape(W_B),
                     jax.ShapeDtypeStruct((CONV_W - 1, n, QKV), F32)],
        scratch_shapes=[pltpu.VMEM((SUBLANES, QKV), F32)],
        compiler_params=pltpu.CompilerParams(dimension_semantics=("arbitrary",),
                                             vmem_limit_bytes=VMEM_LIMIT),
        name="prompt_in",
    )(x2d, *[prm[k] for k in names], *later_weights, xs, prm["ws0"], prm["bs0"], conv_rows, conv_rows, conv_rows)
    n_cast = len(later_weights)
    return outs[:5], outs[5:5 + n_cast], outs[5 + n_cast:]


def _pair_blockdiag(m, top, bottom):
    zero = jnp.zeros_like(m)
    return jnp.concatenate([jnp.where(top, m, zero), jnp.where(bottom, m, zero)], axis=0)


def _unit_lower_inverses(lpairs, ri, cj, left):
    c = lpairs[0].shape[0]
    eye = (ri == cj).astype(F32)
    right = jnp.logical_not(left)
    lbs = [lp.astype(BF16) for lp in lpairs]
    xs = None
    s = 1
    while s < c:
        mask = ((ri // (2 * s)) == (cj // (2 * s))) & (((ri // s) % 2) == 1) & (((cj // s) % 2) == 0)
        if xs is None:
            xs = [eye - jnp.where(mask, lp, 0.0) for lp in lpairs]
        else:
            xbs = [x.astype(BF16) for x in xs]
            ys = [jnp.dot(xb, _pair_blockdiag(lb, left & mask, right & mask), preferred_element_type=F32)
                  for xb, lb in zip(xbs, lbs)]
            zs = [jnp.dot(y.astype(BF16), _pair_blockdiag(xb, left, right), preferred_element_type=F32)
                  for y, xb in zip(ys, xbs)]
            xs = [x - z for x, z in zip(xs, zs)]
        s *= 2
    return xs


def _delta_kernel(qkv_ref, gb_ref, gate_ref, og_ref, oa_ref, sout_ref,
                  s_ref, gcs_ref, u_ref, w_ref, qe_ref, qk_ref, kdt_ref, *, nb, tm, nt):
    i = pl.program_id(1)
    c = DN_CHUNK

    @pl.when(i == 0)
    def _():
        s_ref[...] = jnp.zeros(s_ref.shape, F32)

    assert 2 * c == LANES and H_A % 2 == 0
    ri = lax.broadcasted_iota(jnp.int32, (c, LANES), 0)
    lane = lax.broadcasted_iota(jnp.int32, (c, LANES), 1)
    left = lane < c
    cj = jnp.where(left, lane, lane - c)
    lower = ri >= cj
    strict = ri > cj
    og = og_ref[...]
    chains = [(b, hh) for b in range(nb) for hh in range(H_A)]
    pairs = [(b, p) for b in range(nb) for p in range(H_A // 2)]
    qcols = lambda hh: slice(hh * DK, (hh + 1) * DK)
    kcols = lambda hh: slice(H_A * DK + hh * DK, H_A * DK + (hh + 1) * DK)
    vcols = lambda hh: slice(2 * H_A * DK + hh * DV, 2 * H_A * DK + (hh + 1) * DV)
    pcols = lambda p: slice(p * LANES, (p + 1) * LANES)
    cat = jnp.concatenate
    bf = lambda a: a.astype(BF16)
    zero_bf = jnp.zeros((c, LANES), BF16)

    def prepare(n, carry):
        chunk_ids = [n * PREP_CHUNKS + e for e in range(PREP_CHUNKS) for _ in range(nb)]
        units = [(b, pl.ds(pl.multiple_of((n * PREP_CHUNKS + e) * c, c), c))
                 for e in range(PREP_CHUNKS) for b in range(nb)]
        chains = [(ui, hh) for ui in range(len(units)) for hh in range(H_A)]
        pairs = [(ui, p) for ui in range(len(units)) for p in range(H_A // 2)]
        gbcs, gcss, gcsts = [], [], []
        for b, rows in units:
            gbc = gb_ref[b, rows, :]
            gcs = gbc
            sh = 1
            while sh < c:
                gcs = gcs + jnp.where(ri >= sh, pltpu.roll(gcs, sh, axis=0), 0.0)
                sh *= 2
            gcs_ref[b, rows, :] = gcs
            gbcs.append(gbc)
            gcss.append(gcs)
            gcsts.append(gcs.T)
        qs, ks, vs, kbs, betas, g_cols = {}, {}, {}, {}, {}, {}
        for ui, hh in chains:
            b, rows = units[ui]
            qs[ui, hh] = qkv_ref[b, rows, qcols(hh)]
            ks[ui, hh] = qkv_ref[b, rows, kcols(hh)]
            vs[ui, hh] = qkv_ref[b, rows, vcols(hh)]
            g_cols[ui, hh] = jnp.broadcast_to(gcss[ui][:, hh:hh + 1], (c, LANES))
            betas[ui, hh] = jnp.broadcast_to(gbcs[ui][:, H_A + hh:H_A + hh + 1], (c, LANES))
            kbs[ui, hh] = ks[ui, hh] * betas[ui, hh]
        decays, kqs = [], []
        for ui, p in pairs:
            b, rows = units[ui]
            h0, h1 = 2 * p, 2 * p + 1
            g_col = jnp.where(left, g_cols[ui, h0], g_cols[ui, h1])
            g_row = jnp.broadcast_to(cat([gcsts[ui][h0:h0 + 1, :], gcsts[ui][h1:h1 + 1, :]], axis=1), (c, LANES))
            decays.append(jnp.where(lower, jnp.exp(jnp.where(lower, g_col - g_row, 0.0)), 0.0))
            lhs = cat([cat([bf(kbs[ui, h0]), bf(kbs[ui, h1])], axis=1),
                       cat([bf(qs[ui, h0]), bf(qs[ui, h1])], axis=1)], axis=0)
            rhs = cat([cat([bf(ks[ui, h0]), zero_bf], axis=1), cat([zero_bf, bf(ks[ui, h1])], axis=1)], axis=0)
            kqs.append(lax.dot_general(lhs, rhs, (((1,), (1,)), ((), ())), preferred_element_type=F32))
        lpairs = [jnp.where(strict, kq[:c] * decay, 0.0) for kq, decay in zip(kqs, decays)]
        ts = _unit_lower_inverses(lpairs, ri, cj, left)
        for j, (ui, p) in enumerate(pairs):
            b, rows = units[ui]
            h0, h1 = 2 * p, 2 * p + 1
            vb0 = bf(vs[ui, h0] * betas[ui, h0])
            vb1 = bf(vs[ui, h1] * betas[ui, h1])
            kg0 = bf(kbs[ui, h0] * jnp.exp(g_cols[ui, h0]))
            kg1 = bf(kbs[ui, h1] * jnp.exp(g_cols[ui, h1]))
            rhs = cat([cat([vb0, zero_bf, kg0, zero_bf], axis=1), cat([zero_bf, vb1, zero_bf, kg1], axis=1)], axis=0)
            uw = jnp.dot(bf(ts[j]), rhs, preferred_element_type=F32)
            u_ref[b, rows, qcols(h0)] = uw[:, 0:DV]
            u_ref[b, rows, qcols(h1)] = uw[:, DV:2 * DV]
            w_ref[b, rows, qcols(h0)] = uw[:, 2 * DV:3 * DV].astype(BF16)
            w_ref[b, rows, qcols(h1)] = uw[:, 3 * DV:4 * DV].astype(BF16)
            qk_ref[b, rows, pcols(p)] = (kqs[j][c:] * decays[j]).astype(BF16)
        for ui, hh in chains:
            b, rows = units[ui]
            g_col = g_cols[ui, hh]
            qe_ref[b, rows, qcols(hh)] = (qs[ui, hh] * jnp.exp(g_col)).astype(BF16)
            k_dec = ks[ui, hh] * jnp.exp(g_col[c - 1:c, :] - g_col)
            kdt_ref[b, hh, pl.ds(pl.multiple_of(chunk_ids[ui] * DK, DK), DK), :] = k_dec.T.astype(BF16)
        return carry

    def recur(n, carry):
        rows = pl.ds(pl.multiple_of(n * c, c), c)
        gcss = [gcs_ref[b, rows, :] for b in range(nb)]
        s_olds = [s_ref[b, hh] for b, hh in chains]
        s_bfs = [s.astype(BF16) for s in s_olds]
        wqs = [jnp.dot(cat([w_ref[b, rows, qcols(hh)], qe_ref[b, rows, qcols(hh)]], axis=0), s_bf,
                       preferred_element_type=F32) for (b, hh), s_bf in zip(chains, s_bfs)]
        v_bfs = [(u_ref[b, rows, qcols(hh)] - wq[:c]).astype(BF16) for (b, hh), wq in zip(chains, wqs)]
        kdt_rows = pl.ds(pl.multiple_of(n * DK, DK), DK)
        upds = [jnp.dot(kdt_ref[b, hh, kdt_rows, :], v_bf, preferred_element_type=F32)
                for (b, hh), v_bf in zip(chains, v_bfs)]
        for j, (b, hh) in enumerate(chains):
            g_last = jnp.broadcast_to(gcss[b][c - 1:c, hh:hh + 1], (1, LANES))
            s_ref[b, hh] = s_olds[j] * jnp.exp(g_last) + upds[j]
        qkvs = []
        for j, (b, p) in enumerate(pairs):
            v0, v1 = v_bfs[2 * j], v_bfs[2 * j + 1]
            rhs = cat([cat([v0, zero_bf], axis=1), cat([zero_bf, v1], axis=1)], axis=0)
            both = jnp.dot(qk_ref[b, rows, pcols(p)], rhs, preferred_element_type=F32)
            qkvs += [both[:, :DV], both[:, DV:]]
        for j, (b, hh) in enumerate(chains):
            o = wqs[j][c:] + qkvs[j]
            oa_ref[b, rows, qcols(hh)] = _rmsnorm(o, og) * _silu(gate_ref[b, rows, qcols(hh)])
        return carry

    lax.fori_loop(0, tm // (c * PREP_CHUNKS), prepare, 0)
    lax.fori_loop(0, tm // c, recur, 0, unroll=True)

    @pl.when(i == nt - 1)
    def _():
        sout_ref[...] = s_ref[...]


def _delta_prompt(qkv, gb, gate, layer, prm, *, nb, tm):
    batch, seq, _ = qkv.shape
    nt = seq // tm
    blk = lambda w: pl.BlockSpec((nb, tm, w), lambda b, i: (b, i, 0))
    return pl.pallas_call(
        functools.partial(_delta_kernel, nb=nb, tm=tm, nt=nt),
        grid=(batch // nb, nt),
        in_specs=[blk(QKV), blk(AB_W), blk(W_A), _layer_spec(prm["og"], layer)],
        out_specs=[blk(W_A), pl.BlockSpec((nb, H_A, DK, DV), lambda b, i: (b, 0, 0, 0))],
        out_shape=[jax.ShapeDtypeStruct((batch, seq, W_A), F32),
                   jax.ShapeDtypeStruct((batch, H_A, DK, DV), F32)],
        scratch_shapes=[pltpu.VMEM((nb, H_A, DK, DV), F32),
                        pltpu.VMEM((nb, tm, AB_W), F32),
                        pltpu.VMEM((nb, tm, W_A), F32),
                        pltpu.VMEM((nb, tm, W_A), BF16),
                        pltpu.VMEM((nb, tm, W_A), BF16),
                        pltpu.VMEM((nb, tm, H_A // 2 * LANES), BF16),
                        pltpu.VMEM((nb, H_A, tm // DN_CHUNK * DK, DN_CHUNK), BF16)],
        compiler_params=pltpu.CompilerParams(dimension_semantics=("arbitrary", "arbitrary"),
                                             vmem_limit_bytes=VMEM_LIMIT),
        name="delta_prompt",
    )(qkv, gb, gate, prm["og"])


def _out_ffn_kernel(x_ref, oa_ref, ob_ref, xs_ref, oas_ref, obs_ref, wo_ref, gf_ref, wup_ref, wdn_ref, gl_ref,
                    out_ref, outs_ref, *, n_tiles, final_norm):
    step = pl.program_id(0)
    weights = (wo_ref, gf_ref, wup_ref, wdn_ref, gl_ref)

    @pl.when(step < n_tiles)
    def _():
        _out_ffn_body(x_ref, oa_ref, ob_ref, *weights, out_ref, final_norm=final_norm)

    @pl.when(step == n_tiles)
    def _():
        _out_ffn_body(xs_ref, oas_ref, obs_ref, *weights, outs_ref, final_norm=final_norm)


def _out_ffn_body(x_ref, oa_ref, ob_ref, wo_ref, gf_ref, wup_ref, wdn_ref, gl_ref, out_ref, *, final_norm):
    y = jnp.dot(oa_ref[...].astype(BF16), wo_ref[0:W_A, :], preferred_element_type=F32)
    y = y + jnp.dot(ob_ref[...].astype(BF16), wo_ref[W_A:W_A + W_B, :], preferred_element_type=F32)
    x1 = x_ref[...] + y
    h = _rmsnorm(x1, gf_ref[...]).astype(BF16)
    d_ff = wup_ref.shape[1]
    ffn = None
    for j in range(d_ff // FF_BLOCK):
        cols = slice(j * FF_BLOCK, (j + 1) * FF_BLOCK)
        a = jnp.dot(h, wup_ref[:, cols], preferred_element_type=F32)
        a = jnp.square(jnp.maximum(a, 0.0)).astype(BF16)
        part = jnp.dot(a, wdn_ref[cols, :], preferred_element_type=F32)
        ffn = part if ffn is None else ffn + part
    x2 = x1 + ffn
    if final_norm:
        x2 = _rmsnorm(x2, gl_ref[...])
    out_ref[...] = x2


def _out_ffn(x2d, oa, ob, xs, oa_s, ob_s, layer, prm, weights, *, tm, final_norm):
    rows, d = x2d.shape
    n = xs.shape[0]
    n_tiles = rows // tm
    wo, wup, wdn = weights
    row_spec = lambda w: pl.BlockSpec((tm, w), lambda i: (jnp.minimum(i, n_tiles - 1), 0))
    dec_spec = lambda w: pl.BlockSpec((n, w), lambda i: (0, 0))
    once = lambda w: pl.BlockSpec(w.shape, lambda i: (0, 0), pipeline_mode=pl.Buffered(1))
    return pl.pallas_call(
        functools.partial(_out_ffn_kernel, n_tiles=n_tiles, final_norm=final_norm),
        grid=(n_tiles + 1,),
        in_specs=[row_spec(d), row_spec(W_A), row_spec(W_B), dec_spec(d), dec_spec(W_A), dec_spec(W_B),
                  once(wo), _layer_spec(prm["gffn"], layer), once(wup), once(wdn),
                  pl.BlockSpec((1, d), lambda i: (0, 0))],
        out_specs=[row_spec(d), dec_spec(d)],
        out_shape=[jax.ShapeDtypeStruct((rows, d), F32), jax.ShapeDtypeStruct((n, d), F32)],
        compiler_params=pltpu.CompilerParams(dimension_semantics=("arbitrary",),
                                             vmem_limit_bytes=VMEM_LIMIT),
        name="out_ffn",
    )(x2d, oa, ob, xs, oa_s, ob_s, wo, prm["gffn"], wup, wdn, prm["gl"])


def _sample_in_kernel(x_ref, g_ref, win_ref, cw_ref, alog_ref, dtb_ref, vg_ref, ws0_ref, bs0_ref,
                      c0_ref, c1_ref, c2_ref, qkv_ref, gate_ref, gb_ref, ob_ref, vb_ref, cnew_ref):
    h = _rmsnorm(x_ref[...], g_ref[...]).astype(BF16)
    p = jnp.dot(h, win_ref[...], preferred_element_type=F32)
    pq = p[:, :QKV]
    cw = cw_ref[...]
    c1 = c1_ref[...]
    c2 = c2_ref[...]
    y = c0_ref[...] * cw[0:1, :] + c1 * cw[1:2, :] + c2 * cw[2:3, :] + pq * cw[3:4, :]
    cnew_ref[0] = c1
    cnew_ref[1] = c2
    cnew_ref[2] = pq
    for j in range(QKV // CONV_BLK):
        cols = slice(j * CONV_BLK, (j + 1) * CONV_BLK)
        qkv_ref[:, cols] = _qkv_activation(y[:, cols], j)
    gate_ref[...] = p[:, OFF_GATE:OFF_U]
    gb_ref[...] = _decay_beta(p[:, OFF_AB:PROJ_PAD], alog_ref[...], dtb_ref[...])
    vb = _rmsnorm(p[:, OFF_V:OFF_AB], vg_ref[...])
    vb_ref[...] = vb
    ob_ref[...] = p[:, OFF_U:OFF_V] * (vb * ws0_ref[...] + bs0_ref[...])


def _delta_step_kernel(*refs, tb, chained):
    if chained:
        qkv_ref, gb_ref, gate_ref, og_ref, s_ref, _, oa_ref, snew_ref = refs
    else:
        qkv_ref, gb_ref, gate_ref, og_ref, s_ref, oa_ref, snew_all_ref = refs
        snew_ref = snew_all_ref.at[0]
        snew_all_ref[1:] = jnp.zeros((snew_all_ref.shape[0] - 1,) + snew_all_ref.shape[1:], F32)
    og = og_ref[...]
    gb = gb_ref[...]
    spread = (lax.broadcasted_iota(jnp.int32, (tb, tb * DV), 1) // DV
              == lax.broadcasted_iota(jnp.int32, (tb, tb * DV), 0)).astype(BF16)
    for hh in range(H_A):
        q = qkv_ref[:, hh * DK:(hh + 1) * DK]
        k = qkv_ref[:, H_A * DK + hh * DK:H_A * DK + (hh + 1) * DK]
        v = qkv_ref[:, 2 * H_A * DK + hh * DV:2 * H_A * DK + (hh + 1) * DV]
        q_cols = jnp.dot(q.T.astype(BF16), spread, preferred_element_type=F32)
        k_cols = jnp.dot(k.T.astype(BF16), spread, preferred_element_type=F32)
        decay = jnp.exp(gb[:, hh:hh + 1])
        beta = gb[:, H_A + hh:H_A + hh + 1]
        o_rows = []
        for t in range(tb):
            s = s_ref[t, hh] * decay[t:t + 1, :]
            k_col = k_cols[:, t * DV:(t + 1) * DV]
            q_col = q_cols[:, t * DV:(t + 1) * DV]
            kv = jnp.sum(s * k_col, axis=0, keepdims=True)
            delta = (v[t:t + 1, :] - kv) * beta[t:t + 1, :]
            s = s + k_col * delta
            snew_ref[t, hh] = s
            o_rows.append(jnp.sum(s * q_col, axis=0, keepdims=True))
        o = jnp.concatenate(o_rows, axis=0)
        gate = gate_ref[:, hh * DV:(hh + 1) * DV]
        oa_ref[:, hh * DV:(hh + 1) * DV] = _rmsnorm(o, og) * _silu(gate)


def _delta_step(qkv, gb, gate, state_all, new_all, layer, prm, *, tb):
    n = qkv.shape[0]
    depth = state_all.shape[0]
    row_spec = lambda w: pl.BlockSpec((tb, w), lambda i: (i, 0))
    st_spec = pl.BlockSpec((None, tb, H_A, DK, DV), lambda i: (layer, i, 0, 0, 0))
    chained = new_all is not None
    in_specs = [row_spec(QKV), row_spec(AB_W), row_spec(W_A), _layer_spec(prm["og"], layer), st_spec]
    args = [qkv, gb, gate, prm["og"], state_all]
    if chained:
        in_specs.append(pl.BlockSpec(memory_space=pl.ANY))
        args.append(new_all)
        new_spec = st_spec
    else:
        assert layer == 0
        new_spec = pl.BlockSpec((depth, tb, H_A, DK, DV), lambda i: (0, i, 0, 0, 0))
    return pl.pallas_call(
        functools.partial(_delta_step_kernel, tb=tb, chained=chained),
        grid=(n // tb,),
        in_specs=in_specs,
        out_specs=[row_spec(W_A), new_spec],
        out_shape=[jax.ShapeDtypeStruct((n, W_A), F32), jax.ShapeDtypeStruct(state_all.shape, F32)],
        input_output_aliases={len(args) - 1: 1} if chained else {},
        compiler_params=pltpu.CompilerParams(dimension_semantics=("arbitrary",),
                                             vmem_limit_bytes=VMEM_LIMIT),
        name="delta_step",
    )(*args)


def _reorder_win_kernel(wt_ref, out_ref):
    off_a = QKV + W_A
    off_u = off_a + 2 * H_A
    src_of = lambda dst: dst if dst < off_a else dst + 2 * H_A
    for dst in range(0, OFF_AB, LANES):
        src = src_of(dst)
        out_ref[:, dst:dst + LANES] = wt_ref[src:src + LANES, :].T.astype(BF16)
    ab = jnp.concatenate([wt_ref[off_a:off_u, :], jnp.zeros((AB_W - 2 * H_A, wt_ref.shape[1]), F32)], axis=0)
    out_ref[:, OFF_AB:PROJ_PAD] = ab.T.astype(BF16)


def _reorder_win(wt, *, cols):
    depth, width, d = wt.shape
    return pl.pallas_call(
        _reorder_win_kernel,
        grid=(depth, d // cols),
        in_specs=[pl.BlockSpec((None, width, cols), lambda l, i: (l, 0, i))],
        out_specs=pl.BlockSpec((None, cols, PROJ_PAD), lambda l, i: (l, i, 0)),
        out_shape=jax.ShapeDtypeStruct((depth, d, PROJ_PAD), BF16),
        compiler_params=pltpu.CompilerParams(dimension_semantics=("arbitrary", "arbitrary"),
                                             vmem_limit_bytes=VMEM_LIMIT),
        name="reorder_win",
    )(wt)


def _pick_tile(seq, candidates):
    for tm in candidates:
        if seq % tm == 0:
            return tm
    raise ValueError(f"prompt length must be a multiple of {candidates[-1]}")


def kernel(x_prompt, x_sample, state_delta, state_conv, norm_mix_g, w_in, conv_w, A_log, dt_bias, o_norm_g,
           v_norm_g, w_s, b_s, w_o, norm_ffn_g, w_up, w_down, norm_f_g):
    batch, seq, d = x_prompt.shape
    n_dec, dec_seq, _ = x_sample.shape
    depth = w_in.shape[0]
    assert dec_seq == 1 and seq % GM_CHUNK == 0 and n_dec % SUBLANES == 0
    assert seq % PROJ_ROWS == 0
    tm = _pick_tile(seq, (1024, 512, 256, 128))
    ptm = _pick_tile(seq, (4 * PROJ_ROWS, PROJ_ROWS))
    tb = _pick_tile(n_dec, (4 * SUBLANES, 2 * SUBLANES, SUBLANES))
    nb = next(n for n in (8, 4, 2, 1) if batch % n == 0)
    dtm = GM_CHUNK

    lane_pad = lambda v: jnp.pad(v.astype(F32), ((0, 0), (0, AB_W - v.shape[1]))).reshape(depth, 1, AB_W)
    prm = {
        "win": _reorder_win(jnp.swapaxes(w_in, 1, 2), cols=_pick_tile(d, (512, 256, LANES))),
        "gmix": norm_mix_g.reshape(depth, 1, d),
        "gffn": norm_ffn_g.reshape(depth, 1, d),
        "gl": norm_f_g.reshape(1, d),
        "cw": conv_w,
        "alog": lane_pad(A_log),
        "dtb": lane_pad(dt_bias),
        "vg": v_norm_g.reshape(depth, 1, W_B),
        "og": o_norm_g.reshape(depth, 1, DV),
        "ws": w_s,
        "bsb": jnp.broadcast_to(b_s[:, :, :, None], b_s.shape + (DH_B,)).astype(F32),
        "ws0": jnp.repeat(w_s[:, :, 0, 0], DH_B, axis=-1).reshape(depth, 1, W_B),
        "bs0": jnp.repeat(b_s[:, :, 0], DH_B, axis=-1).reshape(depth, 1, W_B),
    }
    conv_rows = jnp.swapaxes(state_conv, 1, 2)

    xp = x_prompt.reshape(batch * seq, d)
    xs = x_sample.reshape(n_dec, d)
    dp, cp, cs, vs = [], [], [], []
    new_state = None
    for l in range(depth):
        last = l == depth - 1
        (qkv, gate, gb, ob, ctail), ffn_w, (qkv_s, gate_s, gb_s, ob_s, vb_s, cnew) = _prompt_in(
            xp, xs, conv_rows, l, prm, (w_o, w_up, w_down), batch=batch, seq=seq, tm=ptm)
        oa, s_fin = _delta_prompt(qkv.reshape(batch, seq, QKV), gb.reshape(batch, seq, AB_W),
                                  gate.reshape(batch, seq, W_A), l, prm, nb=nb, tm=dtm)
        dp.append(s_fin)
        cp.append(ctail)

        oa_s, new_state = _delta_step(qkv_s, gb_s, gate_s, state_delta, new_state, l, prm,
                                      tb=tb // depth if new_state is None else tb)
        xp, xs = _out_ffn(xp, oa.reshape(batch * seq, W_A), ob, xs, oa_s, ob_s, l, prm, ffn_w, tm=tm,
                          final_norm=last)
        cs.append(cnew)
        vs.append(vb_s.reshape(n_dec, 1, W_B))

    return (xp.reshape(batch, seq, d), xs.reshape(n_dec, 1, d), jnp.stack(dp), jnp.swapaxes(jnp.stack(cp), 1, 2),
            new_state, jnp.swapaxes(jnp.stack(cs), 1, 2), jnp.stack(vs))
```

```python
import functools

import jax
import jax.numpy as jnp
from jax import lax
from jax.experimental import pallas as pl
from jax.experimental.pallas import tpu as pltpu

F32 = jnp.float32
BF16 = jnp.bfloat16

H_A = 4
DK = 128
DV = 128
W_A = H_A * DV
QKV = 2 * H_A * DK + H_A * DV
CONV_W = 4
DN_CHUNK = 64
H_B = 4
DH_B = 128
W_B = H_B * DH_B
GM_CHUNK = 128
EPS = 1e-6

LANES = 128
SUBLANES = 8
AB_W = LANES
OFF_GATE = QKV
OFF_U = OFF_GATE + W_A
OFF_V = OFF_U + W_B
OFF_AB = OFF_V + W_B
PROJ_PAD = OFF_AB + AB_W
CONV_BLK = H_A * DK

VMEM_LIMIT = 60 * 1024 * 1024
FF_BLOCK = 1024
PROJ_ROWS = 512


def _rmsnorm(x, g):
    return x * lax.rsqrt(jnp.mean(x * x, axis=-1, keepdims=True) + EPS) * g


def _silu(x):
    return x * jax.nn.sigmoid(x)


def _softplus(x):
    return jnp.maximum(x, 0.0) + jnp.log1p(jnp.exp(-jnp.abs(x)))


def _layer_spec(arr, layer):
    nd = arr.ndim - 1
    return pl.BlockSpec((None,) + arr.shape[1:], lambda *_: (layer,) + (0,) * nd,
                        pipeline_mode=pl.Buffered(1))


def _qkv_activation(y, j):
    a = _silu(y)
    if j >= 2:
        return a
    blocks = []
    for hh in range(H_A):
        blk = a[:, hh * DK:(hh + 1) * DK]
        inv = lax.rsqrt(jnp.sum(blk * blk, axis=-1, keepdims=True) + EPS)
        if j == 0:
            inv = inv * (DK ** -0.5)
        blocks.append(blk * inv)
    return jnp.concatenate(blocks, axis=1)


def _decay_beta(ab, alog, dtb):
    lane = lax.broadcasted_iota(jnp.int32, ab.shape, 1)
    gdec = -jnp.exp(alog) * _softplus(ab + dtb)
    beta = jax.nn.sigmoid(ab)
    return jnp.where(lane < H_A, gdec, beta)


def _prompt_in_kernel(x_ref, g_ref, win_ref, cw_ref, alog_ref, dtb_ref, vg_ref, ws_ref, bsb_ref, *rest,
                      tm, nt, n_tiles, n_cast):
    cast_in, rest = rest[:n_cast], rest[n_cast:]
    dec_in, rest = rest[:6], rest[6:]
    prompt_out, rest = rest[:5], rest[5:]
    cast_out, rest = rest[:n_cast], rest[n_cast:]
    dec_out, (carry_ref,) = rest[:6], rest[6:]
    for src, dst in zip(cast_in, cast_out):
        dst[...] = src[...].astype(BF16)
    step = pl.program_id(0)

    @pl.when(step < n_tiles)
    def _():
        _prompt_tile(x_ref, g_ref, win_ref, cw_ref, alog_ref, dtb_ref, vg_ref, ws_ref, bsb_ref, *prompt_out,
                     carry_ref, tm=tm, i=lax.rem(step, nt), b=step // nt)

    @pl.when(step == n_tiles)
    def _():
        xs_ref, ws0_ref, bs0_ref, c0_ref, c1_ref, c2_ref = dec_in
        _sample_in_kernel(xs_ref, g_ref, win_ref, cw_ref, alog_ref, dtb_ref, vg_ref, ws0_ref, bs0_ref,
                          c0_ref, c1_ref, c2_ref, *dec_out)


def _prompt_tile(x_ref, g_ref, win_ref, cw_ref, alog_ref, dtb_ref, vg_ref, ws_ref, bsb_ref,
                 qkv_ref, gate_ref, gb_ref, ob_ref, ctail_ref, carry_ref, *, tm, i, b):
    sm = PROJ_ROWS
    n_sub = tm // sm

    @pl.when(i == 0)
    def _():
        carry_ref[...] = jnp.zeros(carry_ref.shape, F32)

    sub = lax.broadcasted_iota(jnp.int32, (SUBLANES, CONV_BLK), 0)
    ri = lax.broadcasted_iota(jnp.int32, (GM_CHUNK, GM_CHUNK), 0)
    ci = lax.broadcasted_iota(jnp.int32, (GM_CHUNK, GM_CHUNK), 1)
    nc = sm // GM_CHUNK
    hs, tails, mixes = {}, {}, {}

    def proj(s, lo, hi):
        if s not in hs:
            hs[s] = _rmsnorm(x_ref[s * sm:(s + 1) * sm, :], g_ref[...]).astype(BF16)
        return jnp.dot(hs[s], win_ref[:, lo:hi], preferred_element_type=F32)

    def conv_act(s, j, pj):
        rows = slice(s * sm, (s + 1) * sm)
        cols = slice(j * CONV_BLK, (j + 1) * CONV_BLK)
        cw = cw_ref[:, cols]
        prev = carry_ref[:, cols] if s == 0 else tails[s - 1, j]

        def shift(x, x_prev, k):
            sh = pltpu.roll(x, k, axis=0)
            head = jnp.where(sub < k, pltpu.roll(x_prev, k, axis=0), sh[0:SUBLANES])
            return jnp.concatenate([head, sh[SUBLANES:]], axis=0)

        w0, w1, w2, w3 = (cw[k:k + 1, :] for k in range(CONV_W))
        pj1 = shift(pj, prev, 1)
        far = pj * w1 + pj1 * w0
        far_prev = prev * w1 + pltpu.roll(prev, 1, axis=0) * w0
        y = (pj * w3 + pj1 * w2) + shift(far, far_prev, 2)
        tails[s, j] = pj[sm - SUBLANES:sm, :]
        if s == n_sub - 1:
            carry_ref[:, cols] = tails[s, j]
            for k in range(CONV_W - 1):
                row = sm - (CONV_W - 1) + k
                ctail_ref[k, pl.ds(b, 1), cols] = pj[row:row + 1, :]
        qkv_ref[rows, cols] = _qkv_activation(y, j)

    def gmlp_mix(s, pv):
        vb = _rmsnorm(pv, vg_ref[...]).astype(BF16)
        mixes[s] = []
        for hh in range(H_B):
            wm = jnp.where(ri >= ci, ws_ref[hh], 0.0).astype(BF16)
            cols = slice(hh * DH_B, (hh + 1) * DH_B)
            rhs = jnp.concatenate([vb[c * GM_CHUNK:(c + 1) * GM_CHUNK, cols] for c in range(nc)], axis=1)
            mixes[s].append(jnp.dot(wm, rhs, preferred_element_type=F32))

    def gmlp_gate(s, pu):
        for hh in range(H_B):
            cols = slice(hh * DH_B, (hh + 1) * DH_B)
            bias = bsb_ref[hh]
            for c in range(nc):
                src = slice(c * GM_CHUNK, (c + 1) * GM_CHUNK)
                dst = slice(s * sm + c * GM_CHUNK, s * sm + (c + 1) * GM_CHUNK)
                ob_ref[dst, cols] = pu[src, cols] * (mixes[s][hh][:, c * DH_B:(c + 1) * DH_B] + bias)

    def store_gb(s, pab):
        gb_ref[s * sm:(s + 1) * sm, :] = _decay_beta(pab, alog_ref[...], dtb_ref[...])

    def store_gate(s, pg):
        gate_ref[s * sm:(s + 1) * sm, :] = pg

    stages = []
    for s in range(n_sub):
        for j in range(QKV // CONV_BLK):
            stages.append((s, j * CONV_BLK, (j + 1) * CONV_BLK, functools.partial(conv_act, s, j)))
    for s in range(n_sub):
        stages.append((s, OFF_V, OFF_AB, functools.partial(gmlp_mix, s)))
        stages.append((s, OFF_U, OFF_V, functools.partial(gmlp_gate, s)))
        stages.append((s, OFF_AB, PROJ_PAD, functools.partial(store_gb, s)))
        stages.append((s, OFF_GATE, OFF_U, functools.partial(store_gate, s)))
    pending = proj(*stages[0][:3])
    for n, stage in enumerate(stages):
        ahead = proj(*stages[n + 1][:3]) if n + 1 < len(stages) else None
        stage[3](pending)
        pending = ahead


def _prompt_in(x2d, xs, conv_rows, layer, prm, later_weights, *, batch, seq, tm):
    nt = seq // tm
    rows = batch * seq
    d = x2d.shape[1]
    n = xs.shape[0]
    steps = batch * nt
    tile = lambda t: jnp.minimum(t, steps - 1)
    row_spec = lambda w: pl.BlockSpec((tm, w), lambda t: (tile(t), 0))
    dec_spec = lambda w: pl.BlockSpec((n, w), lambda t: (0, 0))
    conv_row = lambda j: pl.BlockSpec((None, None, n, QKV), lambda t: (layer, j, 0, 0))
    names = ("gmix", "win", "cw", "alog", "dtb", "vg", "ws", "bsb")
    slab_in, slab_out, slab_shape = [], [], []
    for w in later_weights:
        _, wr, wc = w.shape
        assert wr % (steps * 2 * SUBLANES) == 0
        slab_in.append(pl.BlockSpec((None, wr // steps, wc), lambda t: (layer, tile(t), 0)))
        slab_out.append(pl.BlockSpec((wr // steps, wc), lambda t: (tile(t), 0)))
        slab_shape.append(jax.ShapeDtypeStruct((wr, wc), BF16))
    dec_shape = lambda w: jax.ShapeDtypeStruct((n, w), F32)
    outs = pl.pallas_call(
        functools.partial(_prompt_in_kernel, tm=tm, nt=nt, n_tiles=steps, n_cast=len(later_weights)),
        grid=(steps + 1,),
        in_specs=[row_spec(d)] + [_layer_spec(prm[k], layer) for k in names] + slab_in
                 + [dec_spec(d), _layer_spec(prm["ws0"], layer), _layer_spec(prm["bs0"], layer)]
                 + [conv_row(j) for j in range(CONV_W - 1)],
        out_specs=[row_spec(QKV), row_spec(W_A), row_spec(AB_W), row_spec(W_B),
                   pl.BlockSpec((CONV_W - 1, batch, QKV), lambda t: (0, 0, 0))] + slab_out
                  + [dec_spec(QKV), dec_spec(W_A), dec_spec(AB_W), dec_spec(W_B), dec_spec(W_B),
                     pl.BlockSpec((CONV_W - 1, n, QKV), lambda t: (0, 0, 0))],
        out_shape=[jax.ShapeDtypeStruct((rows, QKV), F32), jax.ShapeDtypeStruct((rows, W_A), F32),
                   jax.ShapeDtypeStruct((rows, AB_W), F32), jax.ShapeDtypeStruct((rows, W_B), F32),
                   jax.ShapeDtypeStruct((CONV_W - 1, batch, QKV), F32)] + slab_shape
                  + [dec_shape(QKV), dec_shape(W_A), dec_shape(AB_W), dec_shape(W_B), dec_shape(W_B),
                     jax.ShapeDtypeStruct((CONV_W - 1, n, QKV), F32)],
        scratch_shapes=[pltpu.VMEM((SUBLANES, QKV), F32)],
        compiler_params=pltpu.CompilerParams(dimension_semantics=("arbitrary",),
                                             vmem_limit_bytes=VMEM_LIMIT),
        name="prompt_in",
    )(x2d, *[prm[k] for k in names], *later_weights, xs, prm["ws0"], prm["bs0"], conv_rows, conv_rows, conv_rows)
    n_cast = len(later_weights)
    return outs[:5], outs[5:5 + n_cast], outs[5 + n_cast:]


def _pair_blockdiag(m, top, bottom):
    zero = jnp.zeros_like(m)
    return jnp.concatenate([jnp.where(top, m, zero), jnp.where(bottom, m, zero)], axis=0)


def _unit_lower_inverses(lpairs, ri, cj, left):
    c = lpairs[0].shape[0]
    eye = (ri == cj).astype(F32)
    right = jnp.logical_not(left)
    lbs = [lp.astype(BF16) for lp in lpairs]
    xs = None
    s = 1
    while s < c:
        mask = ((ri // (2 * s)) == (cj // (2 * s))) & (((ri // s) % 2) == 1) & (((cj // s) % 2) == 0)
        if xs is None:
            xs = [eye - jnp.where(mask, lp, 0.0) for lp in lpairs]
        else:
            xbs = [x.astype(BF16) for x in xs]
            ys = [jnp.dot(xb, _pair_blockdiag(lb, left & mask, right & mask), preferred_element_type=F32)
                  for xb, lb in zip(xbs, lbs)]
            yield
            zs = [jnp.dot(y.astype(BF16), _pair_blockdiag(xb, left, right), preferred_element_type=F32)
                  for y, xb in zip(ys, xbs)]
            xs = [x - z for x, z in zip(xs, zs)]
        yield
        s *= 2
    return xs


def _delta_kernel(qkv_ref, gb_ref, gate_ref, og_ref, oa_ref, sout_ref,
                  s_ref, gcs_ref, u_ref, w_ref, qe_ref, qk_ref, kdt_ref, *, nb, tm, nt):
    i = pl.program_id(1)
    c = DN_CHUNK

    @pl.when(i == 0)
    def _():
        s_ref[...] = jnp.zeros(s_ref.shape, F32)

    assert 2 * c == LANES and H_A % 2 == 0
    ri = lax.broadcasted_iota(jnp.int32, (c, LANES), 0)
    lane = lax.broadcasted_iota(jnp.int32, (c, LANES), 1)
    left = lane < c
    cj = jnp.where(left, lane, lane - c)
    lower = ri >= cj
    strict = ri > cj
    og = og_ref[...]
    chains = [(b, hh) for b in range(nb) for hh in range(H_A)]
    pairs = [(b, p) for b in range(nb) for p in range(H_A // 2)]
    qcols = lambda hh: slice(hh * DK, (hh + 1) * DK)
    kcols = lambda hh: slice(H_A * DK + hh * DK, H_A * DK + (hh + 1) * DK)
    vcols = lambda hh: slice(2 * H_A * DK + hh * DV, 2 * H_A * DK + (hh + 1) * DV)
    pcols = lambda p: slice(p * LANES, (p + 1) * LANES)
    cat = jnp.concatenate
    bf = lambda a: a.astype(BF16)
    zero_bf = jnp.zeros((c, LANES), BF16)

    def prepare(e):
        units = [(b, slice(e * c, (e + 1) * c)) for b in range(nb)]
        chains = [(ui, hh) for ui in range(len(units)) for hh in range(H_A)]
        pairs = [(ui, p) for ui in range(len(units)) for p in range(H_A // 2)]
        gbcs, gcss, gcsts = [], [], []
        for b, rows in units:
            gbc = gb_ref[b, rows, :]
            gcs = gbc
            sh = 1
            while sh < c:
                gcs = gcs + jnp.where(ri >= sh, pltpu.roll(gcs, sh, axis=0), 0.0)
                sh *= 2
            gcs_ref[b, rows, :] = gcs
            gbcs.append(gbc)
            gcss.append(gcs)
            gcsts.append(gcs.T)
        qs, ks, vs, kbs, betas, g_cols = {}, {}, {}, {}, {}, {}
        for ui, hh in chains:
            b, rows = units[ui]
            qs[ui, hh] = qkv_ref[b, rows, qcols(hh)]
            ks[ui, hh] = qkv_ref[b, rows, kcols(hh)]
            vs[ui, hh] = qkv_ref[b, rows, vcols(hh)]
            g_cols[ui, hh] = jnp.broadcast_to(gcss[ui][:, hh:hh + 1], (c, LANES))
            betas[ui, hh] = jnp.broadcast_to(gbcs[ui][:, H_A + hh:H_A + hh + 1], (c, LANES))
            kbs[ui, hh] = ks[ui, hh] * betas[ui, hh]
        decays, kqs = [], []
        for ui, p in pairs:
            b, rows = units[ui]
            h0, h1 = 2 * p, 2 * p + 1
            g_col = jnp.where(left, g_cols[ui, h0], g_cols[ui, h1])
            g_row = jnp.broadcast_to(cat([gcsts[ui][h0:h0 + 1, :], gcsts[ui][h1:h1 + 1, :]], axis=1), (c, LANES))
            decays.append(jnp.where(lower, jnp.exp(jnp.where(lower, g_col - g_row, 0.0)), 0.0))
            lhs = cat([cat([bf(kbs[ui, h0]), bf(kbs[ui, h1])], axis=1),
                       cat([bf(qs[ui, h0]), bf(qs[ui, h1])], axis=1)], axis=0)
            rhs = cat([cat([bf(ks[ui, h0]), zero_bf], axis=1), cat([zero_bf, bf(ks[ui, h1])], axis=1)], axis=0)
            kqs.append(lax.dot_general(lhs, rhs, (((1,), (1,)), ((), ())), preferred_element_type=F32))
        yield
        lpairs = [jnp.where(strict, kq[:c] * decay, 0.0) for kq, decay in zip(kqs, decays)]
        ts = yield from _unit_lower_inverses(lpairs, ri, cj, left)
        for j, (ui, p) in enumerate(pairs):
            b, rows = units[ui]
            h0, h1 = 2 * p, 2 * p + 1
            vb0 = bf(vs[ui, h0] * betas[ui, h0])
            vb1 = bf(vs[ui, h1] * betas[ui, h1])
            kg0 = bf(kbs[ui, h0] * jnp.exp(g_cols[ui, h0]))
            kg1 = bf(kbs[ui, h1] * jnp.exp(g_cols[ui, h1]))
            rhs = cat([cat([vb0, zero_bf, kg0, zero_bf], axis=1), cat([zero_bf, vb1, zero_bf, kg1], axis=1)], axis=0)
            uw = jnp.dot(bf(ts[j]), rhs, preferred_element_type=F32)
            u_ref[b, rows, qcols(h0)] = uw[:, 0:DV]
            u_ref[b, rows, qcols(h1)] = uw[:, DV:2 * DV]
            w_ref[b, rows, qcols(h0)] = uw[:, 2 * DV:3 * DV].astype(BF16)
            w_ref[b, rows, qcols(h1)] = uw[:, 3 * DV:4 * DV].astype(BF16)
            qk_ref[b, rows, pcols(p)] = (kqs[j][c:] * decays[j]).astype(BF16)
        for ui, hh in chains:
            b, rows = units[ui]
            g_col = g_cols[ui, hh]
            qe_ref[b, rows, qcols(hh)] = (qs[ui, hh] * jnp.exp(g_col)).astype(BF16)
            k_dec = ks[ui, hh] * jnp.exp(g_col[c - 1:c, :] - g_col)
            kdt_ref[b, hh, e * DK:(e + 1) * DK, :] = k_dec.T.astype(BF16)
        yield

    def recur(e):
        rows = slice(e * c, (e + 1) * c)
        gcss = [gcs_ref[b, rows, :] for b in range(nb)]
        s_olds = [s_ref[b, hh] for b, hh in chains]
        s_bfs = [s.astype(BF16) for s in s_olds]
        wqs = [jnp.dot(cat([w_ref[b, rows, qcols(hh)], qe_ref[b, rows, qcols(hh)]], axis=0), s_bf,
                       preferred_element_type=F32) for (b, hh), s_bf in zip(chains, s_bfs)]
        yield
        v_bfs = [(u_ref[b, rows, qcols(hh)] - wq[:c]).astype(BF16) for (b, hh), wq in zip(chains, wqs)]
        upds = [jnp.dot(kdt_ref[b, hh, e * DK:(e + 1) * DK, :], v_bf, preferred_element_type=F32)
                for (b, hh), v_bf in zip(chains, v_bfs)]
        for j, (b, hh) in enumerate(chains):
            g_last = jnp.broadcast_to(gcss[b][c - 1:c, hh:hh + 1], (1, LANES))
            s_ref[b, hh] = s_olds[j] * jnp.exp(g_last) + upds[j]
        qkvs = []
        for j, (b, p) in enumerate(pairs):
            v0, v1 = v_bfs[2 * j], v_bfs[2 * j + 1]
            rhs = cat([cat([v0, zero_bf], axis=1), cat([zero_bf, v1], axis=1)], axis=0)
            both = jnp.dot(qk_ref[b, rows, pcols(p)], rhs, preferred_element_type=F32)
            qkvs += [both[:, :DV], both[:, DV:]]
        yield
        for j, (b, hh) in enumerate(chains):
            o = wqs[j][c:] + qkvs[j]
            oa_ref[b, rows, qcols(hh)] = _rmsnorm(o, og) * _silu(gate_ref[b, rows, qcols(hh)])
        yield

    def interleave(main, side, ratio):
        end = object()
        live = {id(main): True, id(side): True}

        def advance(gen):
            if live[id(gen)] and next(gen, end) is end:
                live[id(gen)] = False

        while live[id(main)] or live[id(side)]:
            for _ in range(ratio):
                advance(main)
            advance(side)

    n_chunks = tm // c
    for _ in prepare(0):
        pass
    for e in range(1, n_chunks):
        interleave(prepare(e), recur(e - 1), ratio=3)
    for _ in recur(n_chunks - 1):
        pass

    @pl.when(i == nt - 1)
    def _():
        sout_ref[...] = s_ref[...]


def _delta_prompt(qkv, gb, gate, layer, prm, *, nb, tm):
    batch, seq, _ = qkv.shape
    nt = seq // tm
    blk = lambda w: pl.BlockSpec((nb, tm, w), lambda b, i: (b, i, 0))
    return pl.pallas_call(
        functools.partial(_delta_kernel, nb=nb, tm=tm, nt=nt),
        grid=(batch // nb, nt),
        in_specs=[blk(QKV), blk(AB_W), blk(W_A), _layer_spec(prm["og"], layer)],
        out_specs=[blk(W_A), pl.BlockSpec((nb, H_A, DK, DV), lambda b, i: (b, 0, 0, 0))],
        out_shape=[jax.ShapeDtypeStruct((batch, seq, W_A), F32),
                   jax.ShapeDtypeStruct((batch, H_A, DK, DV), F32)],
        scratch_shapes=[pltpu.VMEM((nb, H_A, DK, DV), F32),
                        pltpu.VMEM((nb, tm, AB_W), F32),
                        pltpu.VMEM((nb, tm, W_A), F32),
                        pltpu.VMEM((nb, tm, W_A), BF16),
                        pltpu.VMEM((nb, tm, W_A), BF16),
                        pltpu.VMEM((nb, tm, H_A // 2 * LANES), BF16),
                        pltpu.VMEM((nb, H_A, tm // DN_CHUNK * DK, DN_CHUNK), BF16)],
        compiler_params=pltpu.CompilerParams(dimension_semantics=("arbitrary", "arbitrary"),
                                             vmem_limit_bytes=VMEM_LIMIT),
        name="delta_prompt",
    )(qkv, gb, gate, prm["og"])


def _out_ffn_kernel(x_ref, oa_ref, ob_ref, xs_ref, oas_ref, obs_ref, wo_ref, gf_ref, wup_ref, wdn_ref, gl_ref,
                    out_ref, outs_ref, *, n_tiles, final_norm):
    step = pl.program_id(0)
    weights = (wo_ref, gf_ref, wup_ref, wdn_ref, gl_ref)

    @pl.when(step < n_tiles)
    def _():
        _out_ffn_body(x_ref, oa_ref, ob_ref, *weights, out_ref, final_norm=final_norm)

    @pl.when(step == n_tiles)
    def _():
        _out_ffn_body(xs_ref, oas_ref, obs_ref, *weights, outs_ref, final_norm=final_norm)


def _out_ffn_body(x_ref, oa_ref, ob_ref, wo_ref, gf_ref, wup_ref, wdn_ref, gl_ref, out_ref, *, final_norm):
    y = jnp.dot(oa_ref[...].astype(BF16), wo_ref[0:W_A, :], preferred_element_type=F32)
    y = y + jnp.dot(ob_ref[...].astype(BF16), wo_ref[W_A:W_A + W_B, :], preferred_element_type=F32)
    x1 = x_ref[...] + y
    h = _rmsnorm(x1, gf_ref[...]).astype(BF16)
    d_ff = wup_ref.shape[1]
    ffn = None
    for j in range(d_ff // FF_BLOCK):
        cols = slice(j * FF_BLOCK, (j + 1) * FF_BLOCK)
        a = jnp.dot(h, wup_ref[:, cols], preferred_element_type=F32)
        a = jnp.square(jnp.maximum(a, 0.0)).astype(BF16)
        part = jnp.dot(a, wdn_ref[cols, :], preferred_element_type=F32)
        ffn = part if ffn is None else ffn + part
    x2 = x1 + ffn
    if final_norm:
        x2 = _rmsnorm(x2, gl_ref[...])
    out_ref[...] = x2


def _out_ffn(x2d, oa, ob, xs, oa_s, ob_s, layer, prm, weights, *, tm, final_norm):
    rows, d = x2d.shape
    n = xs.shape[0]
    n_tiles = rows // tm
    wo, wup, wdn = weights
    row_spec = lambda w: pl.BlockSpec((tm, w), lambda i: (jnp.minimum(i, n_tiles - 1), 0))
    dec_spec = lambda w: pl.BlockSpec((n, w), lambda i: (0, 0))
    once = lambda w: pl.BlockSpec(w.shape, lambda i: (0, 0), pipeline_mode=pl.Buffered(1))
    return pl.pallas_call(
        functools.partial(_out_ffn_kernel, n_tiles=n_tiles, final_norm=final_norm),
        grid=(n_tiles + 1,),
        in_specs=[row_spec(d), row_spec(W_A), row_spec(W_B), dec_spec(d), dec_spec(W_A), dec_spec(W_B),
                  once(wo), _layer_spec(prm["gffn"], layer), once(wup), once(wdn),
                  pl.BlockSpec((1, d), lambda i: (0, 0))],
        out_specs=[row_spec(d), dec_spec(d)],
        out_shape=[jax.ShapeDtypeStruct((rows, d), F32), jax.ShapeDtypeStruct((n, d), F32)],
        compiler_params=pltpu.CompilerParams(dimension_semantics=("arbitrary",),
                                             vmem_limit_bytes=VMEM_LIMIT),
        name="out_ffn",
    )(x2d, oa, ob, xs, oa_s, ob_s, wo, prm["gffn"], wup, wdn, prm["gl"])


def _sample_in_kernel(x_ref, g_ref, win_ref, cw_ref, alog_ref, dtb_ref, vg_ref, ws0_ref, bs0_ref,
                      c0_ref, c1_ref, c2_ref, qkv_ref, gate_ref, gb_ref, ob_ref, vb_ref, cnew_ref):
    h = _rmsnorm(x_ref[...], g_ref[...]).astype(BF16)
    p = jnp.dot(h, win_ref[...], preferred_element_type=F32)
    pq = p[:, :QKV]
    cw = cw_ref[...]
    c1 = c1_ref[...]
    c2 = c2_ref[...]
    y = c0_ref[...] * cw[0:1, :] + c1 * cw[1:2, :] + c2 * cw[2:3, :] + pq * cw[3:4, :]
    cnew_ref[0] = c1
    cnew_ref[1] = c2
    cnew_ref[2] = pq
    for j in range(QKV // CONV_BLK):
        cols = slice(j * CONV_BLK, (j + 1) * CONV_BLK)
        qkv_ref[:, cols] = _qkv_activation(y[:, cols], j)
    gate_ref[...] = p[:, OFF_GATE:OFF_U]
    gb_ref[...] = _decay_beta(p[:, OFF_AB:PROJ_PAD], alog_ref[...], dtb_ref[...])
    vb = _rmsnorm(p[:, OFF_V:OFF_AB], vg_ref[...])
    vb_ref[...] = vb
    ob_ref[...] = p[:, OFF_U:OFF_V] * (vb * ws0_ref[...] + bs0_ref[...])


def _delta_step_kernel(*refs, tb, chained):
    if chained:
        qkv_ref, gb_ref, gate_ref, og_ref, s_ref, _, oa_ref, snew_ref = refs
    else:
        qkv_ref, gb_ref, gate_ref, og_ref, s_ref, oa_ref, snew_all_ref = refs
        snew_ref = snew_all_ref.at[0]
        snew_all_ref[1:] = jnp.zeros((snew_all_ref.shape[0] - 1,) + snew_all_ref.shape[1:], F32)
    og = og_ref[...]
    gb = gb_ref[...]
    spread = (lax.broadcasted_iota(jnp.int32, (tb, tb * DV), 1) // DV
              == lax.broadcasted_iota(jnp.int32, (tb, tb * DV), 0)).astype(BF16)
    for hh in range(H_A):
        q = qkv_ref[:, hh * DK:(hh + 1) * DK]
        k = qkv_ref[:, H_A * DK + hh * DK:H_A * DK + (hh + 1) * DK]
        v = qkv_ref[:, 2 * H_A * DK + hh * DV:2 * H_A * DK + (hh + 1) * DV]
        q_cols = jnp.dot(q.T.astype(BF16), spread, preferred_element_type=F32)
        k_cols = jnp.dot(k.T.astype(BF16), spread, preferred_element_type=F32)
        decay = jnp.exp(gb[:, hh:hh + 1])
        beta = gb[:, H_A + hh:H_A + hh + 1]
        o_rows = []
        for t in range(tb):
            s = s_ref[t, hh] * decay[t:t + 1, :]
            k_col = k_cols[:, t * DV:(t + 1) * DV]
            q_col = q_cols[:, t * DV:(t + 1) * DV]
            kv = jnp.sum(s * k_col, axis=0, keepdims=True)
            delta = (v[t:t + 1, :] - kv) * beta[t:t + 1, :]
            s = s + k_col * delta
            snew_ref[t, hh] = s
            o_rows.append(jnp.sum(s * q_col, axis=0, keepdims=True))
        o = jnp.concatenate(o_rows, axis=0)
        gate = gate_ref[:, hh * DV:(hh + 1) * DV]
        oa_ref[:, hh * DV:(hh + 1) * DV] = _rmsnorm(o, og) * _silu(gate)


def _delta_step(qkv, gb, gate, state_all, new_all, layer, prm, *, tb):
    n = qkv.shape[0]
    depth = state_all.shape[0]
    row_spec = lambda w: pl.BlockSpec((tb, w), lambda i: (i, 0))
    st_spec = pl.BlockSpec((None, tb, H_A, DK, DV), lambda i: (layer, i, 0, 0, 0))
    chained = new_all is not None
    in_specs = [row_spec(QKV), row_spec(AB_W), row_spec(W_A), _layer_spec(prm["og"], layer), st_spec]
    args = [qkv, gb, gate, prm["og"], state_all]
    if chained:
        in_specs.append(pl.BlockSpec(memory_space=pl.ANY))
        args.append(new_all)
        new_spec = st_spec
    else:
        assert layer == 0
        new_spec = pl.BlockSpec((depth, tb, H_A, DK, DV), lambda i: (0, i, 0, 0, 0))
    return pl.pallas_call(
        functools.partial(_delta_step_kernel, tb=tb, chained=chained),
        grid=(n // tb,),
        in_specs=in_specs,
        out_specs=[row_spec(W_A), new_spec],
        out_shape=[jax.ShapeDtypeStruct((n, W_A), F32), jax.ShapeDtypeStruct(state_all.shape, F32)],
        input_output_aliases={len(args) - 1: 1} if chained else {},
        compiler_params=pltpu.CompilerParams(dimension_semantics=("arbitrary",),
                                             vmem_limit_bytes=VMEM_LIMIT),
        name="delta_step",
    )(*args)


def _reorder_win_kernel(wt_ref, out_ref):
    off_a = QKV + W_A
    off_u = off_a + 2 * H_A
    src_of = lambda dst: dst if dst < off_a else dst + 2 * H_A
    for dst in range(0, OFF_AB, LANES):
        src = src_of(dst)
        out_ref[:, dst:dst + LANES] = wt_ref[src:src + LANES, :].T.astype(BF16)
    ab = jnp.concatenate([wt_ref[off_a:off_u, :], jnp.zeros((AB_W - 2 * H_A, wt_ref.shape[1]), F32)], axis=0)
    out_ref[:, OFF_AB:PROJ_PAD] = ab.T.astype(BF16)


def _reorder_win(wt, *, cols):
    depth, width, d = wt.shape
    return pl.pallas_call(
        _reorder_win_kernel,
        grid=(depth, d // cols),
        in_specs=[pl.BlockSpec((None, width, cols), lambda l, i: (l, 0, i))],
        out_specs=pl.BlockSpec((None, cols, PROJ_PAD), lambda l, i: (l, i, 0)),
        out_shape=jax.ShapeDtypeStruct((depth, d, PROJ_PAD), BF16),
        compiler_params=pltpu.CompilerParams(dimension_semantics=("arbitrary", "arbitrary"),
                                             vmem_limit_bytes=VMEM_LIMIT),
        name="reorder_win",
    )(wt)


def _pick_tile(seq, candidates):
    for tm in candidates:
        if seq % tm == 0:
            return tm
    raise ValueError(f"prompt length must be a multiple of {candidates[-1]}")


def kernel(x_prompt, x_sample, state_delta, state_conv, norm_mix_g, w_in, conv_w, A_log, dt_bias, o_norm_g,
           v_norm_g, w_s, b_s, w_o, norm_ffn_g, w_up, w_down, norm_f_g):
    batch, seq, d = x_prompt.shape
    n_dec, dec_seq, _ = x_sample.shape
    depth = w_in.shape[0]
    assert dec_seq == 1 and seq % GM_CHUNK == 0 and n_dec % SUBLANES == 0
    assert seq % PROJ_ROWS == 0
    tm = _pick_tile(seq, (1024, 512, 256, 128))
    ptm = _pick_tile(seq, (2 * PROJ_ROWS, PROJ_ROWS))
    tb = _pick_tile(n_dec, (4 * SUBLANES, 2 * SUBLANES, SUBLANES))
    nb = next(n for n in (8, 4, 2, 1) if batch % n == 0)
    dtm = GM_CHUNK

    lane_pad = lambda v: jnp.pad(v.astype(F32), ((0, 0), (0, AB_W - v.shape[1]))).reshape(depth, 1, AB_W)
    prm = {
        "win": _reorder_win(jnp.swapaxes(w_in, 1, 2), cols=256 if d % 256 == 0 else LANES),
        "gmix": norm_mix_g.reshape(depth, 1, d),
        "gffn": norm_ffn_g.reshape(depth, 1, d),
        "gl": norm_f_g.reshape(1, d),
        "cw": conv_w,
        "alog": lane_pad(A_log),
        "dtb": lane_pad(dt_bias),
        "vg": v_norm_g.reshape(depth, 1, W_B),
        "og": o_norm_g.reshape(depth, 1, DV),
        "ws": w_s,
        "bsb": jnp.broadcast_to(b_s[:, :, :, None], b_s.shape + (DH_B,)).astype(F32),
        "ws0": jnp.repeat(w_s[:, :, 0, 0], DH_B, axis=-1).reshape(depth, 1, W_B),
        "bs0": jnp.repeat(b_s[:, :, 0], DH_B, axis=-1).reshape(depth, 1, W_B),
    }
    conv_rows = jnp.swapaxes(state_conv, 1, 2)

    xp = x_prompt.reshape(batch * seq, d)
    xs = x_sample.reshape(n_dec, d)
    dp, cp, cs, vs = [], [], [], []
    new_state = None
    for l in range(depth):
        last = l == depth - 1
        (qkv, gate, gb, ob, ctail), ffn_w, (qkv_s, gate_s, gb_s, ob_s, vb_s, cnew) = _prompt_in(
            xp, xs, conv_rows, l, prm, (w_o, w_up, w_down), batch=batch, seq=seq, tm=ptm)
        oa, s_fin = _delta_prompt(qkv.reshape(batch, seq, QKV), gb.reshape(batch, seq, AB_W),
                                  gate.reshape(batch, seq, W_A), l, prm, nb=nb, tm=dtm)
        dp.append(s_fin)
        cp.append(ctail)

        oa_s, new_state = _delta_step(qkv_s, gb_s, gate_s, state_delta, new_state, l, prm,
                                      tb=tb // depth if new_state is None else tb)
        xp, xs = _out_ffn(xp, oa.reshape(batch * seq, W_A), ob, xs, oa_s, ob_s, l, prm, ffn_w, tm=tm,
                          final_norm=last)
        cs.append(cnew)
        vs.append(vb_s.reshape(n_dec, 1, W_B))

    return (xp.reshape(batch, seq, d), xs.reshape(n_dec, 1, d), jnp.stack(dp), jnp.swapaxes(jnp.stack(cp), 1, 2),
            new_state, jnp.swapaxes(jnp.stack(cs), 1, 2), jnp.stack(vs))
```

```python
import functools

import jax
import jax.numpy as jnp
from jax import lax
from jax.experimental import pallas as pl
from jax.experimental.pallas import tpu as pltpu

F32 = jnp.float32
BF16 = jnp.bfloat16

H_A = 4
DK = 128
DV = 128
W_A = H_A * DV
QKV = 2 * H_A * DK + H_A * DV
CONV_W = 4
DN_CHUNK = 64
H_B = 4
DH_B = 128
W_B = H_B * DH_B
GM_CHUNK = 128
EPS = 1e-6

LANES = 128
SUBLANES = 8
AB_W = LANES
OFF_GATE = QKV
OFF_U = OFF_GATE + W_A
OFF_V = OFF_U + W_B
OFF_AB = OFF_V + W_B
PROJ_PAD = OFF_AB + AB_W
CONV_BLK = H_A * DK

VMEM_LIMIT = 60 * 1024 * 1024
FF_BLOCK = 1024
PROJ_ROWS = 512


def _rmsnorm(x, g):
    return x * lax.rsqrt(jnp.mean(x * x, axis=-1, keepdims=True) + EPS) * g


def _silu(x):
    return x * jax.nn.sigmoid(x)


def _softplus(x):
    return jnp.maximum(x, 0.0) + jnp.log1p(jnp.exp(-jnp.abs(x)))


def _layer_spec(arr, layer):
    nd = arr.ndim - 1
    return pl.BlockSpec((None,) + arr.shape[1:], lambda *_: (layer,) + (0,) * nd,
                        pipeline_mode=pl.Buffered(1))


def _qkv_activation(y, j):
    a = _silu(y)
    if j >= 2:
        return a
    blocks = []
    for hh in range(H_A):
        blk = a[:, hh * DK:(hh + 1) * DK]
        inv = lax.rsqrt(jnp.sum(blk * blk, axis=-1, keepdims=True) + EPS)
        if j == 0:
            inv = inv * (DK ** -0.5)
        blocks.append(blk * inv)
    return jnp.concatenate(blocks, axis=1)


def _decay_beta(ab, alog, dtb):
    lane = lax.broadcasted_iota(jnp.int32, ab.shape, 1)
    gdec = -jnp.exp(alog) * _softplus(ab + dtb)
    beta = jax.nn.sigmoid(ab)
    return jnp.where(lane < H_A, gdec, beta)


def _prompt_in_kernel(x_ref, g_ref, win_ref, cw_ref, alog_ref, dtb_ref, vg_ref, ws_ref, bsb_ref, *rest,
                      tm, nt, n_tiles, n_cast):
    cast_in, rest = rest[:n_cast], rest[n_cast:]
    dec_in, rest = rest[:6], rest[6:]
    prompt_out, rest = rest[:5], rest[5:]
    cast_out, rest = rest[:n_cast], rest[n_cast:]
    dec_out, (carry_ref,) = rest[:6], rest[6:]
    for src, dst in zip(cast_in, cast_out):
        dst[...] = src[...].astype(BF16)
    step = pl.program_id(0)

    @pl.when(step < n_tiles)
    def _():
        _prompt_tile(x_ref, g_ref, win_ref, cw_ref, alog_ref, dtb_ref, vg_ref, ws_ref, bsb_ref, *prompt_out,
                     carry_ref, tm=tm, i=lax.rem(step, nt), b=step // nt)

    @pl.when(step == n_tiles)
    def _():
        xs_ref, ws0_ref, bs0_ref, c0_ref, c1_ref, c2_ref = dec_in
        _sample_in_kernel(xs_ref, g_ref, win_ref, cw_ref, alog_ref, dtb_ref, vg_ref, ws0_ref, bs0_ref,
                          c0_ref, c1_ref, c2_ref, *dec_out)


def _prompt_tile(x_ref, g_ref, win_ref, cw_ref, alog_ref, dtb_ref, vg_ref, ws_ref, bsb_ref,
                 qkv_ref, gate_ref, gb_ref, ob_ref, ctail_ref, carry_ref, *, tm, i, b):
    sm = PROJ_ROWS
    n_sub = tm // sm

    @pl.when(i == 0)
    def _():
        carry_ref[...] = jnp.zeros(carry_ref.shape, F32)

    sub = lax.broadcasted_iota(jnp.int32, (SUBLANES, CONV_BLK), 0)
    ri = lax.broadcasted_iota(jnp.int32, (GM_CHUNK, GM_CHUNK), 0)
    ci = lax.broadcasted_iota(jnp.int32, (GM_CHUNK, GM_CHUNK), 1)
    nc = sm // GM_CHUNK
    hs, tails, mixes = {}, {}, {}

    def proj(s, lo, hi):
        if s not in hs:
            hs[s] = _rmsnorm(x_ref[s * sm:(s + 1) * sm, :], g_ref[...]).astype(BF16)
        return jnp.dot(hs[s], win_ref[:, lo:hi], preferred_element_type=F32)

    def conv_act(s, j, pj):
        rows = slice(s * sm, (s + 1) * sm)
        cols = slice(j * CONV_BLK, (j + 1) * CONV_BLK)
        cw = cw_ref[:, cols]
        prev = carry_ref[:, cols] if s == 0 else tails[s - 1, j]

        def shift(x, x_prev, k):
            sh = pltpu.roll(x, k, axis=0)
            head = jnp.where(sub < k, pltpu.roll(x_prev, k, axis=0), sh[0:SUBLANES])
            return jnp.concatenate([head, sh[SUBLANES:]], axis=0)

        w0, w1, w2, w3 = (cw[k:k + 1, :] for k in range(CONV_W))
        pj1 = shift(pj, prev, 1)
        far = pj * w1 + pj1 * w0
        far_prev = prev * w1 + pltpu.roll(prev, 1, axis=0) * w0
        y = (pj * w3 + pj1 * w2) + shift(far, far_prev, 2)
        tails[s, j] = pj[sm - SUBLANES:sm, :]
        if s == n_sub - 1:
            carry_ref[:, cols] = tails[s, j]
            for k in range(CONV_W - 1):
                row = sm - (CONV_W - 1) + k
                ctail_ref[k, pl.ds(b, 1), cols] = pj[row:row + 1, :]
        qkv_ref[rows, cols] = _qkv_activation(y, j)

    def gmlp_mix(s, pv):
        vb = _rmsnorm(pv, vg_ref[...]).astype(BF16)
        mixes[s] = []
        for hh in range(H_B):
            wm = jnp.where(ri >= ci, ws_ref[hh], 0.0).astype(BF16)
            cols = slice(hh * DH_B, (hh + 1) * DH_B)
            rhs = jnp.concatenate([vb[c * GM_CHUNK:(c + 1) * GM_CHUNK, cols] for c in range(nc)], axis=1)
            mixes[s].append(jnp.dot(wm, rhs, preferred_element_type=F32))

    def gmlp_gate(s, pu):
        for hh in range(H_B):
            cols = slice(hh * DH_B, (hh + 1) * DH_B)
            bias = bsb_ref[hh]
            for c in range(nc):
                src = slice(c * GM_CHUNK, (c + 1) * GM_CHUNK)
                dst = slice(s * sm + c * GM_CHUNK, s * sm + (c + 1) * GM_CHUNK)
                ob_ref[dst, cols] = pu[src, cols] * (mixes[s][hh][:, c * DH_B:(c + 1) * DH_B] + bias)

    def store_gb(s, pab):
        gb_ref[s * sm:(s + 1) * sm, :] = _decay_beta(pab, alog_ref[...], dtb_ref[...])

    def store_gate(s, pg):
        gate_ref[s * sm:(s + 1) * sm, :] = pg

    stages = []
    for s in range(n_sub):
        for j in range(QKV // CONV_BLK):
            stages.append((s, j * CONV_BLK, (j + 1) * CONV_BLK, functools.partial(conv_act, s, j)))
    for s in range(n_sub):
        stages.append((s, OFF_V, OFF_AB, functools.partial(gmlp_mix, s)))
        stages.append((s, OFF_U, OFF_V, functools.partial(gmlp_gate, s)))
        stages.append((s, OFF_AB, PROJ_PAD, functools.partial(store_gb, s)))
        stages.append((s, OFF_GATE, OFF_U, functools.partial(store_gate, s)))
    pending = proj(*stages[0][:3])
    for n, stage in enumerate(stages):
        ahead = proj(*stages[n + 1][:3]) if n + 1 < len(stages) else None
        stage[3](pending)
        pending = ahead


def _prompt_in(x2d, xs, conv_rows, layer, prm, later_weights, *, batch, seq, tm):
    nt = seq // tm
    rows = batch * seq
    d = x2d.shape[1]
    n = xs.shape[0]
    steps = batch * nt
    tile = lambda t: jnp.minimum(t, steps - 1)
    row_spec = lambda w: pl.BlockSpec((tm, w), lambda t: (tile(t), 0))
    dec_spec = lambda w: pl.BlockSpec((n, w), lambda t: (0, 0))
    conv_row = lambda j: pl.BlockSpec((None, None, n, QKV), lambda t: (layer, j, 0, 0))
    names = ("gmix", "win", "cw", "alog", "dtb", "vg", "ws", "bsb")
    slab_in, slab_out, slab_shape = [], [], []
    for w in later_weights:
        _, wr, wc = w.shape
        assert wr % (steps * 2 * SUBLANES) == 0
        slab_in.append(pl.BlockSpec((None, wr // steps, wc), lambda t: (layer, tile(t), 0)))
        slab_out.append(pl.BlockSpec((wr // steps, wc), lambda t: (tile(t), 0)))
        slab_shape.append(jax.ShapeDtypeStruct((wr, wc), BF16))
    dec_shape = lambda w: jax.ShapeDtypeStruct((n, w), F32)
    outs = pl.pallas_call(
        functools.partial(_prompt_in_kernel, tm=tm, nt=nt, n_tiles=steps, n_cast=len(later_weights)),
        grid=(steps + 1,),
        in_specs=[row_spec(d)] + [_layer_spec(prm[k], layer) for k in names] + slab_in
                 + [dec_spec(d), _layer_spec(prm["ws0"], layer), _layer_spec(prm["bs0"], layer)]
                 + [conv_row(j) for j in range(CONV_W - 1)],
        out_specs=[row_spec(QKV), row_spec(W_A), row_spec(AB_W), row_spec(W_B),
                   pl.BlockSpec((CONV_W - 1, batch, QKV), lambda t: (0, 0, 0))] + slab_out
                  + [dec_spec(QKV), dec_spec(W_A), dec_spec(AB_W), dec_spec(W_B), dec_spec(W_B),
                     pl.BlockSpec((CONV_W - 1, n, QKV), lambda t: (0, 0, 0))],
        out_shape=[jax.ShapeDtypeStruct((rows, QKV), F32), jax.ShapeDtypeStruct((rows, W_A), F32),
                   jax.ShapeDtypeStruct((rows, AB_W), F32), jax.ShapeDtypeStruct((rows, W_B), F32),
                   jax.ShapeDtypeStruct((CONV_W - 1, batch, QKV), F32)] + slab_shape
                  + [dec_shape(QKV), dec_shape(W_A), dec_shape(AB_W), dec_shape(W_B), dec_shape(W_B),
                     jax.ShapeDtypeStruct((CONV_W - 1, n, QKV), F32)],
        scratch_shapes=[pltpu.VMEM((SUBLANES, QKV), F32)],
        compiler_params=pltpu.CompilerParams(dimension_semantics=("arbitrary",),
                                             vmem_limit_bytes=VMEM_LIMIT),
        name="prompt_in",
    )(x2d, *[prm[k] for k in names], *later_weights, xs, prm["ws0"], prm["bs0"], conv_rows, conv_rows, conv_rows)
    n_cast = len(later_weights)
    return outs[:5], outs[5:5 + n_cast], outs[5 + n_cast:]


def _pair_blockdiag(m, top, bottom):
    zero = jnp.zeros_like(m)
    return jnp.concatenate([jnp.where(top, m, zero), jnp.where(bottom, m, zero)], axis=0)


def _unit_lower_inverses(lpairs, ri, cj, left):
    c = lpairs[0].shape[0]
    eye = (ri == cj).astype(F32)
    right = jnp.logical_not(left)
    lbs = [lp.astype(BF16) for lp in lpairs]
    xs = None
    s = 1
    while s < c:
        mask = ((ri // (2 * s)) == (cj // (2 * s))) & (((ri // s) % 2) == 1) & (((cj // s) % 2) == 0)
        if xs is None:
            xs = [eye - jnp.where(mask, lp, 0.0) for lp in lpairs]
        else:
            xbs = [x.astype(BF16) for x in xs]
            ys = [jnp.dot(xb, _pair_blockdiag(lb, left & mask, right & mask), preferred_element_type=F32)
                  for xb, lb in zip(xbs, lbs)]
            yield
            zs = [jnp.dot(y.astype(BF16), _pair_blockdiag(xb, left, right), preferred_element_type=F32)
                  for y, xb in zip(ys, xbs)]
            xs = [x - z for x, z in zip(xs, zs)]
        yield
        s *= 2
    return xs


def _delta_kernel(qkv_ref, gb_ref, gate_ref, og_ref, oa_ref, sout_ref,
                  s_ref, gcs_ref, u_ref, w_ref, qe_ref, qk_ref, kdt_ref, *, nb, tm, nt):
    i = pl.program_id(1)
    c = DN_CHUNK

    @pl.when(i == 0)
    def _():
        s_ref[...] = jnp.zeros(s_ref.shape, F32)

    assert 2 * c == LANES and H_A % 2 == 0
    ri = lax.broadcasted_iota(jnp.int32, (c, LANES), 0)
    lane = lax.broadcasted_iota(jnp.int32, (c, LANES), 1)
    left = lane < c
    cj = jnp.where(left, lane, lane - c)
    lower = ri >= cj
    strict = ri > cj
    og = og_ref[...]
    chains = [(b, hh) for b in range(nb) for hh in range(H_A)]
    pairs = [(b, p) for b in range(nb) for p in range(H_A // 2)]
    qcols = lambda hh: slice(hh * DK, (hh + 1) * DK)
    kcols = lambda hh: slice(H_A * DK + hh * DK, H_A * DK + (hh + 1) * DK)
    vcols = lambda hh: slice(2 * H_A * DK + hh * DV, 2 * H_A * DK + (hh + 1) * DV)
    pcols = lambda p: slice(p * LANES, (p + 1) * LANES)
    cat = jnp.concatenate
    bf = lambda a: a.astype(BF16)
    zero_bf = jnp.zeros((c, LANES), BF16)

    def prepare(e):
        units = [(b, slice(e * c, (e + 1) * c)) for b in range(nb)]
        chains = [(ui, hh) for ui in range(len(units)) for hh in range(H_A)]
        pairs = [(ui, p) for ui in range(len(units)) for p in range(H_A // 2)]
        gbcs, gcss, gcsts = [], [], []
        for b, rows in units:
            gbc = gb_ref[b, rows, :]
            gcs = gbc
            sh = 1
            while sh < c:
                gcs = gcs + jnp.where(ri >= sh, pltpu.roll(gcs, sh, axis=0), 0.0)
                sh *= 2
            gcs_ref[b, rows, :] = gcs
            gbcs.append(gbc)
            gcss.append(gcs)
            gcsts.append(gcs.T)
        qs, ks, vs, kbs, betas, g_cols = {}, {}, {}, {}, {}, {}
        for ui, hh in chains:
            b, rows = units[ui]
            qs[ui, hh] = qkv_ref[b, rows, qcols(hh)]
            ks[ui, hh] = qkv_ref[b, rows, kcols(hh)]
            vs[ui, hh] = qkv_ref[b, rows, vcols(hh)]
            g_cols[ui, hh] = jnp.broadcast_to(gcss[ui][:, hh:hh + 1], (c, LANES))
            betas[ui, hh] = jnp.broadcast_to(gbcs[ui][:, H_A + hh:H_A + hh + 1], (c, LANES))
            kbs[ui, hh] = ks[ui, hh] * betas[ui, hh]
        decays, kqs = [], []
        for ui, p in pairs:
            b, rows = units[ui]
            h0, h1 = 2 * p, 2 * p + 1
            g_col = jnp.where(left, g_cols[ui, h0], g_cols[ui, h1])
            g_row = jnp.broadcast_to(cat([gcsts[ui][h0:h0 + 1, :], gcsts[ui][h1:h1 + 1, :]], axis=1), (c, LANES))
            decays.append(jnp.where(lower, jnp.exp(jnp.where(lower, g_col - g_row, 0.0)), 0.0))
            lhs = cat([cat([bf(kbs[ui, h0]), bf(kbs[ui, h1])], axis=1),
                       cat([bf(qs[ui, h0]), bf(qs[ui, h1])], axis=1)], axis=0)
            rhs = cat([cat([bf(ks[ui, h0]), zero_bf], axis=1), cat([zero_bf, bf(ks[ui, h1])], axis=1)], axis=0)
            kqs.append(lax.dot_general(lhs, rhs, (((1,), (1,)), ((), ())), preferred_element_type=F32))
        yield
        lpairs = [jnp.where(strict, kq[:c] * decay, 0.0) for kq, decay in zip(kqs, decays)]
        ts = yield from _unit_lower_inverses(lpairs, ri, cj, left)
        for j, (ui, p) in enumerate(pairs):
            b, rows = units[ui]
            h0, h1 = 2 * p, 2 * p + 1
            vb0 = bf(vs[ui, h0] * betas[ui, h0])
            vb1 = bf(vs[ui, h1] * betas[ui, h1])
            kg0 = bf(kbs[ui, h0] * jnp.exp(g_cols[ui, h0]))
            kg1 = bf(kbs[ui, h1] * jnp.exp(g_cols[ui, h1]))
            rhs = cat([cat([vb0, zero_bf, kg0, zero_bf], axis=1), cat([zero_bf, vb1, zero_bf, kg1], axis=1)], axis=0)
            uw = jnp.dot(bf(ts[j]), rhs, preferred_element_type=F32)
            u_ref[b, rows, qcols(h0)] = uw[:, 0:DV]
            u_ref[b, rows, qcols(h1)] = uw[:, DV:2 * DV]
            w_ref[b, rows, qcols(h0)] = uw[:, 2 * DV:3 * DV].astype(BF16)
            w_ref[b, rows, qcols(h1)] = uw[:, 3 * DV:4 * DV].astype(BF16)
            qk_ref[b, rows, pcols(p)] = (kqs[j][c:] * decays[j]).astype(BF16)
        for ui, hh in chains:
            b, rows = units[ui]
            g_col = g_cols[ui, hh]
            qe_ref[b, rows, qcols(hh)] = (qs[ui, hh] * jnp.exp(g_col)).astype(BF16)
            k_dec = ks[ui, hh] * jnp.exp(g_col[c - 1:c, :] - g_col)
            kdt_ref[b, hh, e * DK:(e + 1) * DK, :] = k_dec.T.astype(BF16)
        yield

    def recur(e):
        rows = slice(e * c, (e + 1) * c)
        gcss = [gcs_ref[b, rows, :] for b in range(nb)]
        s_olds = [s_ref[b, hh] for b, hh in chains]
        s_bfs = [s.astype(BF16) for s in s_olds]
        wqs = [jnp.dot(cat([w_ref[b, rows, qcols(hh)], qe_ref[b, rows, qcols(hh)]], axis=0), s_bf,
                       preferred_element_type=F32) for (b, hh), s_bf in zip(chains, s_bfs)]
        yield
        v_bfs = [(u_ref[b, rows, qcols(hh)] - wq[:c]).astype(BF16) for (b, hh), wq in zip(chains, wqs)]
        upds = [jnp.dot(kdt_ref[b, hh, e * DK:(e + 1) * DK, :], v_bf, preferred_element_type=F32)
                for (b, hh), v_bf in zip(chains, v_bfs)]
        for j, (b, hh) in enumerate(chains):
            g_last = jnp.broadcast_to(gcss[b][c - 1:c, hh:hh + 1], (1, LANES))
            s_ref[b, hh] = s_olds[j] * jnp.exp(g_last) + upds[j]
        qkvs = []
        for j, (b, p) in enumerate(pairs):
            v0, v1 = v_bfs[2 * j], v_bfs[2 * j + 1]
            rhs = cat([cat([v0, zero_bf], axis=1), cat([zero_bf, v1], axis=1)], axis=0)
            both = jnp.dot(qk_ref[b, rows, pcols(p)], rhs, preferred_element_type=F32)
            qkvs += [both[:, :DV], both[:, DV:]]
        yield
        for j, (b, hh) in enumerate(chains):
            o = wqs[j][c:] + qkvs[j]
            oa_ref[b, rows, qcols(hh)] = _rmsnorm(o, og) * _silu(gate_ref[b, rows, qcols(hh)])
        yield

    def interleave(main, side, ratio):
        end = object()
        live = {id(main): True, id(side): True}

        def advance(gen):
            if live[id(gen)] and next(gen, end) is end:
                live[id(gen)] = False

        while live[id(main)] or live[id(side)]:
            advance(side)
            for _ in range(ratio):
                advance(main)

    n_chunks = tm // c
    for _ in prepare(0):
        pass
    for e in range(1, n_chunks):
        interleave(prepare(e), recur(e - 1), ratio=1)
    for _ in recur(n_chunks - 1):
        pass

    @pl.when(i == nt - 1)
    def _():
        sout_ref[...] = s_ref[...]


def _delta_prompt(qkv, gb, gate, layer, prm, *, nb, tm):
    batch, seq, _ = qkv.shape
    nt = seq // tm
    blk = lambda w: pl.BlockSpec((nb, tm, w), lambda b, i: (b, i, 0))
    return pl.pallas_call(
        functools.partial(_delta_kernel, nb=nb, tm=tm, nt=nt),
        grid=(batch // nb, nt),
        in_specs=[blk(QKV), blk(AB_W), blk(W_A), _layer_spec(prm["og"], layer)],
        out_specs=[blk(W_A), pl.BlockSpec((nb, H_A, DK, DV), lambda b, i: (b, 0, 0, 0))],
        out_shape=[jax.ShapeDtypeStruct((batch, seq, W_A), F32),
                   jax.ShapeDtypeStruct((batch, H_A, DK, DV), F32)],
        scratch_shapes=[pltpu.VMEM((nb, H_A, DK, DV), F32),
                        pltpu.VMEM((nb, tm, AB_W), F32),
                        pltpu.VMEM((nb, tm, W_A), F32),
                        pltpu.VMEM((nb, tm, W_A), BF16),
                        pltpu.VMEM((nb, tm, W_A), BF16),
                        pltpu.VMEM((nb, tm, H_A // 2 * LANES), BF16),
                        pltpu.VMEM((nb, H_A, tm // DN_CHUNK * DK, DN_CHUNK), BF16)],
        compiler_params=pltpu.CompilerParams(dimension_semantics=("arbitrary", "arbitrary"),
                                             vmem_limit_bytes=VMEM_LIMIT),
        name="delta_prompt",
    )(qkv, gb, gate, prm["og"])


def _out_ffn_kernel(x_ref, oa_ref, ob_ref, xs_ref, oas_ref, obs_ref, wo_ref, gf_ref, wup_ref, wdn_ref, gl_ref,
                    out_ref, outs_ref, *, n_tiles, final_norm):
    step = pl.program_id(0)
    weights = (wo_ref, gf_ref, wup_ref, wdn_ref, gl_ref)

    @pl.when(step < n_tiles)
    def _():
        _out_ffn_body(x_ref, oa_ref, ob_ref, *weights, out_ref, final_norm=final_norm)

    @pl.when(step == n_tiles)
    def _():
        _out_ffn_body(xs_ref, oas_ref, obs_ref, *weights, outs_ref, final_norm=final_norm)


def _out_ffn_body(x_ref, oa_ref, ob_ref, wo_ref, gf_ref, wup_ref, wdn_ref, gl_ref, out_ref, *, final_norm):
    y = jnp.dot(oa_ref[...].astype(BF16), wo_ref[0:W_A, :], preferred_element_type=F32)
    y = y + jnp.dot(ob_ref[...].astype(BF16), wo_ref[W_A:W_A + W_B, :], preferred_element_type=F32)
    x1 = x_ref[...] + y
    h = _rmsnorm(x1, gf_ref[...]).astype(BF16)
    d_ff = wup_ref.shape[1]
    ffn = None
    for j in range(d_ff // FF_BLOCK):
        cols = slice(j * FF_BLOCK, (j + 1) * FF_BLOCK)
        a = jnp.dot(h, wup_ref[:, cols], preferred_element_type=F32)
        a = jnp.square(jnp.maximum(a, 0.0)).astype(BF16)
        part = jnp.dot(a, wdn_ref[cols, :], preferred_element_type=F32)
        ffn = part if ffn is None else ffn + part
    x2 = x1 + ffn
    if final_norm:
        x2 = _rmsnorm(x2, gl_ref[...])
    out_ref[...] = x2


def _out_ffn(x2d, oa, ob, xs, oa_s, ob_s, layer, prm, weights, *, tm, final_norm):
    rows, d = x2d.shape
    n = xs.shape[0]
    n_tiles = rows // tm
    wo, wup, wdn = weights
    row_spec = lambda w: pl.BlockSpec((tm, w), lambda i: (jnp.minimum(i, n_tiles - 1), 0))
    dec_spec = lambda w: pl.BlockSpec((n, w), lambda i: (0, 0))
    once = lambda w: pl.BlockSpec(w.shape, lambda i: (0, 0), pipeline_mode=pl.Buffered(1))
    return pl.pallas_call(
        functools.partial(_out_ffn_kernel, n_tiles=n_tiles, final_norm=final_norm),
        grid=(n_tiles + 1,),
        in_specs=[row_spec(d), row_spec(W_A), row_spec(W_B), dec_spec(d), dec_spec(W_A), dec_spec(W_B),
                  once(wo), _layer_spec(prm["gffn"], layer), once(wup), once(wdn),
                  pl.BlockSpec((1, d), lambda i: (0, 0))],
        out_specs=[row_spec(d), dec_spec(d)],
        out_shape=[jax.ShapeDtypeStruct((rows, d), F32), jax.ShapeDtypeStruct((n, d), F32)],
        compiler_params=pltpu.CompilerParams(dimension_semantics=("arbitrary",),
                                             vmem_limit_bytes=VMEM_LIMIT),
        name="out_ffn",
    )(x2d, oa, ob, xs, oa_s, ob_s, wo, prm["gffn"], wup, wdn, prm["gl"])


def _sample_in_kernel(x_ref, g_ref, win_ref, cw_ref, alog_ref, dtb_ref, vg_ref, ws0_ref, bs0_ref,
                      c0_ref, c1_ref, c2_ref, qkv_ref, gate_ref, gb_ref, ob_ref, vb_ref, cnew_ref):
    h = _rmsnorm(x_ref[...], g_ref[...]).astype(BF16)
    p = jnp.dot(h, win_ref[...], preferred_element_type=F32)
    pq = p[:, :QKV]
    cw = cw_ref[...]
    c1 = c1_ref[...]
    c2 = c2_ref[...]
    y = c0_ref[...] * cw[0:1, :] + c1 * cw[1:2, :] + c2 * cw[2:3, :] + pq * cw[3:4, :]
    cnew_ref[0] = c1
    cnew_ref[1] = c2
    cnew_ref[2] = pq
    for j in range(QKV // CONV_BLK):
        cols = slice(j * CONV_BLK, (j + 1) * CONV_BLK)
        qkv_ref[:, cols] = _qkv_activation(y[:, cols], j)
    gate_ref[...] = p[:, OFF_GATE:OFF_U]
    gb_ref[...] = _decay_beta(p[:, OFF_AB:PROJ_PAD], alog_ref[...], dtb_ref[...])
    vb = _rmsnorm(p[:, OFF_V:OFF_AB], vg_ref[...])
    vb_ref[...] = vb
    ob_ref[...] = p[:, OFF_U:OFF_V] * (vb * ws0_ref[...] + bs0_ref[...])


def _delta_step_kernel(*refs, tb, chained):
    if chained:
        qkv_ref, gb_ref, gate_ref, og_ref, s_ref, _, oa_ref, snew_ref = refs
    else:
        qkv_ref, gb_ref, gate_ref, og_ref, s_ref, oa_ref, snew_all_ref = refs
        snew_ref = snew_all_ref.at[0]
        snew_all_ref[1:] = jnp.zeros((snew_all_ref.shape[0] - 1,) + snew_all_ref.shape[1:], F32)
    og = og_ref[...]
    gb = gb_ref[...]
    spread = (lax.broadcasted_iota(jnp.int32, (tb, tb * DV), 1) // DV
              == lax.broadcasted_iota(jnp.int32, (tb, tb * DV), 0)).astype(BF16)
    for hh in range(H_A):
        q = qkv_ref[:, hh * DK:(hh + 1) * DK]
        k = qkv_ref[:, H_A * DK + hh * DK:H_A * DK + (hh + 1) * DK]
        v = qkv_ref[:, 2 * H_A * DK + hh * DV:2 * H_A * DK + (hh + 1) * DV]
        q_cols = jnp.dot(q.T.astype(BF16), spread, preferred_element_type=F32)
        k_cols = jnp.dot(k.T.astype(BF16), spread, preferred_element_type=F32)
        decay = jnp.exp(gb[:, hh:hh + 1])
        beta = gb[:, H_A + hh:H_A + hh + 1]
        o_rows = []
        for t in range(tb):
            s = s_ref[t, hh] * decay[t:t + 1, :]
            k_col = k_cols[:, t * DV:(t + 1) * DV]
            q_col = q_cols[:, t * DV:(t + 1) * DV]
            kv = jnp.sum(s * k_col, axis=0, keepdims=True)
            delta = (v[t:t + 1, :] - kv) * beta[t:t + 1, :]
            s = s + k_col * delta
            snew_ref[t, hh] = s
            o_rows.append(jnp.sum(s * q_col, axis=0, keepdims=True))
        o = jnp.concatenate(o_rows, axis=0)
        gate = gate_ref[:, hh * DV:(hh + 1) * DV]
        oa_ref[:, hh * DV:(hh + 1) * DV] = _rmsnorm(o, og) * _silu(gate)


def _delta_step(qkv, gb, gate, state_all, new_all, layer, prm, *, tb):
    n = qkv.shape[0]
    depth = state_all.shape[0]
    row_spec = lambda w: pl.BlockSpec((tb, w), lambda i: (i, 0))
    st_spec = pl.BlockSpec((None, tb, H_A, DK, DV), lambda i: (layer, i, 0, 0, 0))
    chained = new_all is not None
    in_specs = [row_spec(QKV), row_spec(AB_W), row_spec(W_A), _layer_spec(prm["og"], layer), st_spec]
    args = [qkv, gb, gate, prm["og"], state_all]
    if chained:
        in_specs.append(pl.BlockSpec(memory_space=pl.ANY))
        args.append(new_all)
        new_spec = st_spec
    else:
        assert layer == 0
        new_spec = pl.BlockSpec((depth, tb, H_A, DK, DV), lambda i: (0, i, 0, 0, 0))
    return pl.pallas_call(
        functools.partial(_delta_step_kernel, tb=tb, chained=chained),
        grid=(n // tb,),
        in_specs=in_specs,
        out_specs=[row_spec(W_A), new_spec],
        out_shape=[jax.ShapeDtypeStruct((n, W_A), F32), jax.ShapeDtypeStruct(state_all.shape, F32)],
        input_output_aliases={len(args) - 1: 1} if chained else {},
        compiler_params=pltpu.CompilerParams(dimension_semantics=("arbitrary",),
                                             vmem_limit_bytes=VMEM_LIMIT),
        name="delta_step",
    )(*args)


def _reorder_win_kernel(wt_ref, out_ref):
    off_a = QKV + W_A
    off_u = off_a + 2 * H_A
    src_of = lambda dst: dst if dst < off_a else dst + 2 * H_A
    for dst in range(0, OFF_AB, LANES):
        src = src_of(dst)
        out_ref[:, dst:dst + LANES] = wt_ref[src:src + LANES, :].T.astype(BF16)
    ab = jnp.concatenate([wt_ref[off_a:off_u, :], jnp.zeros((AB_W - 2 * H_A, wt_ref.shape[1]), F32)], axis=0)
    out_ref[:, OFF_AB:PROJ_PAD] = ab.T.astype(BF16)


def _reorder_win(wt, *, cols):
    depth, width, d = wt.shape
    return pl.pallas_call(
        _reorder_win_kernel,
        grid=(depth, d // cols),
        in_specs=[pl.BlockSpec((None, width, cols), lambda l, i: (l, 0, i))],
        out_specs=pl.BlockSpec((None, cols, PROJ_PAD), lambda l, i: (l, i, 0)),
        out_shape=jax.ShapeDtypeStruct((depth, d, PROJ_PAD), BF16),
        compiler_params=pltpu.CompilerParams(dimension_semantics=("arbitrary", "arbitrary"),
                                             vmem_limit_bytes=VMEM_LIMIT),
        name="reorder_win",
    )(wt)


def _pick_tile(seq, candidates):
    for tm in candidates:
        if seq % tm == 0:
            return tm
    raise ValueError(f"prompt length must be a multiple of {candidates[-1]}")


def kernel(x_prompt, x_sample, state_delta, state_conv, norm_mix_g, w_in, conv_w, A_log, dt_bias, o_norm_g,
           v_norm_g, w_s, b_s, w_o, norm_ffn_g, w_up, w_down, norm_f_g):
    batch, seq, d = x_prompt.shape
    n_dec, dec_seq, _ = x_sample.shape
    depth = w_in.shape[0]
    assert dec_seq == 1 and seq % GM_CHUNK == 0 and n_dec % SUBLANES == 0
    assert seq % PROJ_ROWS == 0
    tm = _pick_tile(seq, (1024, 512, 256, 128))
    ptm = _pick_tile(seq, (2 * PROJ_ROWS, PROJ_ROWS))
    tb = _pick_tile(n_dec, (4 * SUBLANES, 2 * SUBLANES, SUBLANES))
    nb = next(n for n in (8, 4, 2, 1) if batch % n == 0)
    dtm = GM_CHUNK

    lane_pad = lambda v: jnp.pad(v.astype(F32), ((0, 0), (0, AB_W - v.shape[1]))).reshape(depth, 1, AB_W)
    prm = {
        "win": _reorder_win(jnp.swapaxes(w_in, 1, 2), cols=256 if d % 256 == 0 else LANES),
        "gmix": norm_mix_g.reshape(depth, 1, d),
        "gffn": norm_ffn_g.reshape(depth, 1, d),
        "gl": norm_f_g.reshape(1, d),
        "cw": conv_w,
        "alog": lane_pad(A_log),
        "dtb": lane_pad(dt_bias),
        "vg": v_norm_g.reshape(depth, 1, W_B),
        "og": o_norm_g.reshape(depth, 1, DV),
        "ws": w_s,
        "bsb": jnp.broadcast_to(b_s[:, :, :, None], b_s.shape + (DH_B,)).astype(F32),
        "ws0": jnp.repeat(w_s[:, :, 0, 0], DH_B, axis=-1).reshape(depth, 1, W_B),
        "bs0": jnp.repeat(b_s[:, :, 0], DH_B, axis=-1).reshape(depth, 1, W_B),
    }
    conv_rows = jnp.swapaxes(state_conv, 1, 2)

    xp = x_prompt.reshape(batch * seq, d)
    xs = x_sample.reshape(n_dec, d)
    dp, cp, cs, vs = [], [], [], []
    new_state = None
    for l in range(depth):
        last = l == depth - 1
        (qkv, gate, gb, ob, ctail), ffn_w, (qkv_s, gate_s, gb_s, ob_s, vb_s, cnew) = _prompt_in(
            xp, xs, conv_rows, l, prm, (w_o, w_up, w_down), batch=batch, seq=seq, tm=ptm)
        oa, s_fin = _delta_prompt(qkv.reshape(batch, seq, QKV), gb.reshape(batch, seq, AB_W),
                                  gate.reshape(batch, seq, W_A), l, prm, nb=nb, tm=dtm)
        dp.append(s_fin)
        cp.append(ctail)

        oa_s, new_state = _delta_step(qkv_s, gb_s, gate_s, state_delta, new_state, l, prm,
                                      tb=tb // depth if new_state is None else tb)
        xp, xs = _out_ffn(xp, oa.reshape(batch * seq, W_A), ob, xs, oa_s, ob_s, l, prm, ffn_w, tm=tm,
                          final_norm=last)
        cs.append(cnew)
        vs.append(vb_s.reshape(n_dec, 1, W_B))

    return (xp.reshape(batch, seq, d), xs.reshape(n_dec, 1, d), jnp.stack(dp), jnp.swapaxes(jnp.stack(cp), 1, 2),
            new_state, jnp.swapaxes(jnp.stack(cs), 1, 2), jnp.stack(vs))
```
